```python
import math
import jax, jax.numpy as jnp
from jax import lax
import numpy as np

D_MODEL = 1024
BATCH = 16
SEQ = 4096
DEPTH = 2

N_META = 16
CHUNK = 64
N_PAD = CHUNK - N_META
GDN_HEADS = D_MODEL // 128
GDN_DK = 128
GDN_DV = 128
GDN_QK = GDN_HEADS * GDN_DK
GDN_V = GDN_HEADS * GDN_DV
CONV_K = 5
CONV_DIM = 2 * GDN_QK + GDN_V
RET_HEADS = D_MODEL // 256
RET_DK = 256
RET_DV = 512
RET_QK = RET_HEADS * RET_DK
RET_V = RET_HEADS * RET_DV
ROPE_BASE = 10000.0
FFN_HIDDEN = -(-8 * D_MODEL // (3 * 256)) * 256
SPLIT_SIZES = (CONV_DIM, GDN_V, 2 * GDN_HEADS, 2 * GDN_HEADS, RET_QK, RET_QK, RET_V, RET_V, D_MODEL, D_MODEL)
N_IN = CONV_DIM + GDN_V + 4 * GDN_HEADS + 2 * RET_QK + 2 * RET_V + 2 * D_MODEL
EPS = 1e-6

kernel_name = "hybrid_gdn_retention_encoder"


def rms(x):
    xf = x.astype(jnp.float32)
    return xf * lax.rsqrt(jnp.mean(xf * xf, axis=-1, keepdims=True) + EPS)


def rms_norm(x, gain):
    return (rms(x) * gain.astype(jnp.float32)).astype(x.dtype)


def l2norm(x):
    return x * lax.rsqrt(jnp.sum(x * x, axis=-1, keepdims=True) + EPS)


def split_points(sizes):
    return [int(v) for v in np.cumsum(np.array(sizes))[:-1]]


def short_conv(x, w):
    c = x.shape[-1]
    pad = (w.shape[0] - 1) // 2
    y = lax.conv_general_dilated(x, w[:, None, :].astype(x.dtype), window_strides=(1,),
                                 padding=[(pad, pad)], dimension_numbers=('NWC', 'WIO', 'NWC'),
                                 feature_group_count=c)
    return jax.nn.silu(y)


def rope(t, pos):
    half = t.shape[-1] // 2
    inv = ROPE_BASE ** (-jnp.arange(half, dtype=jnp.float32) / half)
    ang = pos.astype(jnp.float32)[:, None] * inv[None, :]
    cos = jnp.cos(ang)[None, :, None, :]
    sin = jnp.sin(ang)[None, :, None, :]
    t1, t2 = t[..., :half], t[..., half:]
    return jnp.concatenate([t1 * cos - t2 * sin, t1 * sin + t2 * cos], axis=-1)


def pad_front(t):
    pads = [(0, 0)] * t.ndim
    pads[1] = (N_PAD, 0)
    return jnp.pad(t, pads)


def flip(t):
    return jnp.flip(t, axis=1)


def to_chunks(t):
    b, lp, h, d = t.shape
    return t.reshape(b, lp // CHUNK, CHUNK, h, d).transpose(1, 0, 3, 2, 4)


def from_chunks(t):
    n, b, h, c, d = t.shape
    return t.transpose(1, 0, 3, 2, 4).reshape(b, n * c, h, d)


def gated_delta_chunked(q, k, v, g, beta):
    dk = q.shape[-1]
    b, _, h, dv = v.shape
    q = to_chunks(q * (dk ** -0.5))
    k = to_chunks(k)
    v = to_chunks(v)
    g = to_chunks(g[..., None])[..., 0]
    beta = to_chunks(beta[..., None])[..., 0]
    gc = jnp.cumsum(g, axis=-1)
    idx = jnp.arange(CHUNK)
    lower_incl = idx[:, None] >= idx[None, :]
    lower_strict = idx[:, None] > idx[None, :]
    diff = gc[..., :, None] - gc[..., None, :]
    decay = jnp.where(lower_incl, jnp.exp(jnp.where(lower_incl, diff, 0.0)), 0.0)
    kb = k * beta[..., None]
    m = jnp.einsum('nbhid,nbhjd->nbhij', kb, k) * decay * lower_strict
    a = m + jnp.eye(CHUNK, dtype=m.dtype)
    u = lax.linalg.triangular_solve(a, v * beta[..., None], left_side=True, lower=True, unit_diagonal=True)
    w = lax.linalg.triangular_solve(a, kb * jnp.exp(gc)[..., None], left_side=True, lower=True, unit_diagonal=True)
    qk = jnp.einsum('nbhid,nbhjd->nbhij', q, k) * decay
    q_dec = q * jnp.exp(gc)[..., None]
    g_last = gc[..., -1]
    k_dec = k * jnp.exp(g_last[..., None] - gc)[..., None]

    def step(s, xs):
        u_i, w_i, qk_i, qd_i, kd_i, gl_i = xs
        v_new = u_i - jnp.einsum('bhcd,bhde->bhce', w_i, s)
        o = jnp.einsum('bhcd,bhde->bhce', qd_i, s) + jnp.einsum('bhij,bhje->bhie', qk_i, v_new)
        s = s * jnp.exp(gl_i)[..., None, None] + jnp.einsum('bhcd,bhce->bhde', kd_i, v_new)
        return s, o

    s0 = jnp.zeros((b, h, dk, dv), jnp.float32)
    _, o = lax.scan(step, s0, (u, w, qk, q_dec, k_dec, g_last))
    return from_chunks(o)


def retention_chunked(q, k, v, log_gamma):
    dk = q.shape[-1]
    b, _, h, dv = v.shape
    q = to_chunks(q)
    k = to_chunks(k * (dk ** -0.5))
    v = to_chunks(v)
    pos = jnp.arange(CHUNK, dtype=jnp.float32)
    lg = log_gamma.astype(jnp.float32)[:, None]
    lower_incl = pos[:, None] >= pos[None, :]
    rel = jnp.where(lower_incl, pos[:, None] - pos[None, :], 0.0)
    intra = jnp.where(lower_incl, jnp.exp(rel[None] * lg[..., None]), 0.0)
    qk = jnp.einsum('nbhid,nbhjd->nbhij', q, k) * intra
    q_dec = q * jnp.exp(lg * (pos + 1.0))[..., None]
    k_dec = k * jnp.exp(lg * (CHUNK - 1.0 - pos))[..., None]
    chunk_decay = jnp.exp(lg * CHUNK)[..., None]

    def step(r, xs):
        qd_i, kd_i, qk_i, v_i = xs
        o = jnp.einsum('bhcd,bhde->bhce', qd_i, r) + jnp.einsum('bhij,bhje->bhie', qk_i, v_i)
        r = r * chunk_decay + jnp.einsum('bhcd,bhce->bhde', kd_i, v_i)
        return r, o

    r0 = jnp.zeros((b, h, dk, dv), jnp.float32)
    _, o = lax.scan(step, r0, (q_dec, k_dec, qk, v))
    return from_chunks(o)


def token_mixer(h, w_in, conv_w, a_log, dt_bias, gdn_gain, ret_logit, w_up_a, w_up_b, w_out):
    b, l, _ = h.shape
    f32 = jnp.float32
    proj = h @ w_in.astype(h.dtype)
    (qkv_a, z_a, a_in, b_in, q_b, k_b, v_b, g_b, gate_a, gate_b) = jnp.split(proj, split_points(SPLIT_SIZES), axis=-1)

    qkv = short_conv(qkv_a, conv_w).astype(f32)
    qa, ka, va = jnp.split(qkv, [GDN_QK, 2 * GDN_QK], axis=-1)
    qa = l2norm(qa.reshape(b, l, GDN_HEADS, GDN_DK))
    ka = l2norm(ka.reshape(b, l, GDN_HEADS, GDN_DK))
    va = va.reshape(b, l, GDN_HEADS, GDN_DV)
    a_in = a_in.astype(f32).reshape(b, l, 2, GDN_HEADS)
    g = -jnp.exp(a_log.astype(f32)) * jax.nn.softplus(a_in + dt_bias.astype(f32))
    beta = jax.nn.sigmoid(b_in.astype(f32).reshape(b, l, 2, GDN_HEADS))
    qp, kp, vp, gp, bp = pad_front(qa), pad_front(ka), pad_front(va), pad_front(g), pad_front(beta)
    o_fwd = gated_delta_chunked(qp, kp, vp, gp[:, :, 0], bp[:, :, 0])
    o_bwd = flip(gated_delta_chunked(flip(qp), flip(kp), flip(vp), flip(gp[:, :, 1]), flip(bp[:, :, 1])))
    o_a = (o_fwd + o_bwd)[:, N_PAD:]
    o_a = rms(o_a) * gdn_gain.astype(f32) * jax.nn.silu(z_a.astype(f32).reshape(b, l, GDN_HEADS, GDN_DV))
    y_a = o_a.reshape(b, l, GDN_V).astype(h.dtype) @ w_up_a.astype(h.dtype)

    pos = jnp.arange(l)
    qr = rope(q_b.astype(f32).reshape(b, l, RET_HEADS, RET_DK), pos)
    kr = rope(k_b.astype(f32).reshape(b, l, RET_HEADS, RET_DK), pos)
    vr = v_b.astype(f32).reshape(b, l, RET_HEADS, RET_DV)
    log_gamma = jax.nn.log_sigmoid(ret_logit.astype(f32))
    qp, kp, vp = pad_front(qr), pad_front(kr), pad_front(vr)
    r_fwd = retention_chunked(qp, kp, vp, log_gamma[0])
    r_bwd = flip(retention_chunked(flip(qp), flip(kp), flip(vp), log_gamma[1]))
    o_b = rms((r_fwd + r_bwd)[:, N_PAD:])
    o_b = o_b.reshape(b, l, RET_V) * jax.nn.silu(g_b.astype(f32))
    y_b = o_b.astype(h.dtype) @ w_up_b.astype(h.dtype)

    merged = jax.nn.sigmoid(gate_a) * y_a + jax.nn.sigmoid(gate_b) * y_b
    return merged @ w_out.astype(h.dtype)


def swiglu(h, w_ffn_in, w_ffn_out):
    gate, up = jnp.split(h @ w_ffn_in.astype(h.dtype), 2, axis=-1)
    return (jax.nn.silu(gate) * up) @ w_ffn_out.astype(h.dtype)


def _fwd_setup_inputs(seed: int = 0) -> dict:
    key = jax.random.key(seed)
    ks = jax.random.split(key, 20)
    f32 = jnp.float32

    def dense(k, shape, fan_in):
        return jax.random.normal(k, shape, f32) * (fan_in ** -0.5)

    def gain(k, shape):
        return 1.0 + 0.02 * jax.random.normal(k, shape, f32)

    x = jax.random.normal(ks[0], (BATCH, SEQ, D_MODEL), f32)
    meta_tokens = jax.random.normal(ks[1], (N_META, D_MODEL), f32)
    norm_mix = gain(ks[2], (DEPTH, D_MODEL))
    w_in = dense(ks[3], (DEPTH, D_MODEL, N_IN), D_MODEL)
    conv_w = dense(ks[4], (DEPTH, CONV_K, CONV_DIM), CONV_K)
    gdn_a_log = jnp.log(jax.random.uniform(ks[5], (DEPTH, 2, GDN_HEADS), f32, 1.0, 16.0))
    dt = jnp.exp(jax.random.uniform(ks[6], (DEPTH, 2, GDN_HEADS), f32, math.log(1e-3), math.log(1e-1)))
    gdn_dt_bias = dt + jnp.log(-jnp.expm1(-dt))
    gdn_norm = gain(ks[7], (DEPTH, GDN_DV))
    base_logit = jnp.log(2.0 ** (5.0 + jnp.arange(RET_HEADS, dtype=f32)) - 1.0)
    ret_decay_logit = base_logit + 0.1 * jax.random.normal(ks[8], (DEPTH, 2, RET_HEADS), f32)
    w_up_a = dense(ks[9], (DEPTH, GDN_V, D_MODEL), GDN_V)
    w_up_b = dense(ks[10], (DEPTH, RET_V, D_MODEL), RET_V)
    w_out = dense(ks[11], (DEPTH, D_MODEL, D_MODEL), D_MODEL)
    norm_ffn = gain(ks[12], (DEPTH, D_MODEL))
    w_ffn_in = dense(ks[13], (DEPTH, D_MODEL, 2 * FFN_HIDDEN), D_MODEL)
    w_ffn_out = dense(ks[14], (DEPTH, FFN_HIDDEN, D_MODEL), FFN_HIDDEN)
    norm_final = gain(ks[15], (D_MODEL,))
    return {"x": x, "meta_tokens": meta_tokens, "norm_mix": norm_mix, "w_in": w_in, "conv_w": conv_w,
            "gdn_a_log": gdn_a_log, "gdn_dt_bias": gdn_dt_bias, "gdn_norm": gdn_norm,
            "ret_decay_logit": ret_decay_logit, "w_up_a": w_up_a, "w_up_b": w_up_b, "w_out": w_out,
            "norm_ffn": norm_ffn, "w_ffn_in": w_ffn_in, "w_ffn_out": w_ffn_out, "norm_final": norm_final}


def _fwd_reference(x, meta_tokens, norm_mix, w_in, conv_w, gdn_a_log, gdn_dt_bias, gdn_norm, ret_decay_logit,
              w_up_a, w_up_b, w_out, norm_ffn, w_ffn_in, w_ffn_out, norm_final):
    b = x.shape[0]
    meta = jnp.broadcast_to(meta_tokens.astype(x.dtype)[None], (b, N_META, x.shape[-1]))
    h = jnp.concatenate([meta, x], axis=1)
    for i in range(DEPTH):
        h = h + token_mixer(rms_norm(h, norm_mix[i]), w_in[i], conv_w[i], gdn_a_log[i], gdn_dt_bias[i],
                            gdn_norm[i], ret_decay_logit[i], w_up_a[i], w_up_b[i], w_out[i])
        h = h + swiglu(rms_norm(h, norm_ffn[i]), w_ffn_in[i], w_ffn_out[i])
    h = rms_norm(h, norm_final)
    return h[:, N_META:]


import jax as _jax
import jax.numpy as _jnp

TWIN_FORMAT = 'train_step'
FWD_PARAMS = ['x', 'meta_tokens', 'norm_mix', 'w_in', 'conv_w', 'gdn_a_log', 'gdn_dt_bias', 'gdn_norm', 'ret_decay_logit', 'w_up_a', 'w_up_b', 'w_out', 'norm_ffn', 'w_ffn_in', 'w_ffn_out', 'norm_final']
TWIN_WEIGHTS = ['meta_tokens', 'norm_mix', 'w_in', 'conv_w', 'gdn_a_log', 'gdn_dt_bias', 'gdn_norm', 'ret_decay_logit', 'w_up_a', 'w_up_b', 'w_out', 'norm_ffn', 'w_ffn_in', 'w_ffn_out', 'norm_final']
TWIN_DIFF_INPUT = 'x'
TWIN_INPUTS = ['x', 'meta_tokens', 'norm_mix', 'w_in', 'conv_w', 'gdn_a_log', 'gdn_dt_bias', 'gdn_norm', 'ret_decay_logit', 'w_up_a', 'w_up_b', 'w_out', 'norm_ffn', 'w_ffn_in', 'w_ffn_out', 'norm_final', 'loss_target', 'm_meta_tokens', 'm_norm_mix', 'm_w_in', 'm_conv_w', 'm_gdn_a_log', 'm_gdn_dt_bias', 'm_gdn_norm', 'm_ret_decay_logit', 'm_w_up_a', 'm_w_up_b', 'm_w_out', 'm_norm_ffn', 'm_w_ffn_in', 'm_w_ffn_out', 'm_norm_final', 'v_meta_tokens', 'v_norm_mix', 'v_w_in', 'v_conv_w', 'v_gdn_a_log', 'v_gdn_dt_bias', 'v_gdn_norm', 'v_ret_decay_logit', 'v_w_up_a', 'v_w_up_b', 'v_w_out', 'v_norm_ffn', 'v_w_ffn_in', 'v_w_ffn_out', 'v_norm_final']
TWIN_OUTPUTS = ['loss', 'grad_x', 'grad_meta_tokens', 'grad_norm_mix', 'grad_w_in', 'grad_conv_w', 'grad_gdn_a_log', 'grad_gdn_dt_bias', 'grad_gdn_norm', 'grad_ret_decay_logit', 'grad_w_up_a', 'grad_w_up_b', 'grad_w_out', 'grad_norm_ffn', 'grad_w_ffn_in', 'grad_w_ffn_out', 'grad_norm_final', 'delta_meta_tokens', 'delta_norm_mix', 'delta_w_in', 'delta_conv_w', 'delta_gdn_a_log', 'delta_gdn_dt_bias', 'delta_gdn_norm', 'delta_ret_decay_logit', 'delta_w_up_a', 'delta_w_up_b', 'delta_w_out', 'delta_norm_ffn', 'delta_w_ffn_in', 'delta_w_ffn_out', 'delta_norm_final', 'new_m_meta_tokens', 'new_m_norm_mix', 'new_m_w_in', 'new_m_conv_w', 'new_m_gdn_a_log', 'new_m_gdn_dt_bias', 'new_m_gdn_norm', 'new_m_ret_decay_logit', 'new_m_w_up_a', 'new_m_w_up_b', 'new_m_w_out', 'new_m_norm_ffn', 'new_m_w_ffn_in', 'new_m_w_ffn_out', 'new_m_norm_final', 'new_v_meta_tokens', 'new_v_norm_mix', 'new_v_w_in', 'new_v_conv_w', 'new_v_gdn_a_log', 'new_v_gdn_dt_bias', 'new_v_gdn_norm', 'new_v_ret_decay_logit', 'new_v_w_up_a', 'new_v_w_up_b', 'new_v_w_out', 'new_v_norm_ffn', 'new_v_w_ffn_in', 'new_v_w_ffn_out', 'new_v_norm_final']
TWIN_LEAF_KINDS = {'loss': 'loss', 'grad_x': 'grad_x', 'grad_meta_tokens': 'grad_w', 'grad_norm_mix': 'grad_w', 'grad_w_in': 'grad_w', 'grad_conv_w': 'grad_w', 'grad_gdn_a_log': 'grad_w', 'grad_gdn_dt_bias': 'grad_w', 'grad_gdn_norm': 'grad_w', 'grad_ret_decay_logit': 'grad_w', 'grad_w_up_a': 'grad_w', 'grad_w_up_b': 'grad_w', 'grad_w_out': 'grad_w', 'grad_norm_ffn': 'grad_w', 'grad_w_ffn_in': 'grad_w', 'grad_w_ffn_out': 'grad_w', 'grad_norm_final': 'grad_w', 'delta_meta_tokens': 'delta_w', 'delta_norm_mix': 'delta_w', 'delta_w_in': 'delta_w', 'delta_conv_w': 'delta_w', 'delta_gdn_a_log': 'delta_w', 'delta_gdn_dt_bias': 'delta_w', 'delta_gdn_norm': 'delta_w', 'delta_ret_decay_logit': 'delta_w', 'delta_w_up_a': 'delta_w', 'delta_w_up_b': 'delta_w', 'delta_w_out': 'delta_w', 'delta_norm_ffn': 'delta_w', 'delta_w_ffn_in': 'delta_w', 'delta_w_ffn_out': 'delta_w', 'delta_norm_final': 'delta_w', 'new_m_meta_tokens': 'new_m', 'new_m_norm_mix': 'new_m', 'new_m_w_in': 'new_m', 'new_m_conv_w': 'new_m', 'new_m_gdn_a_log': 'new_m', 'new_m_gdn_dt_bias': 'new_m', 'new_m_gdn_norm': 'new_m', 'new_m_ret_decay_logit': 'new_m', 'new_m_w_up_a': 'new_m', 'new_m_w_up_b': 'new_m', 'new_m_w_out': 'new_m', 'new_m_norm_ffn': 'new_m', 'new_m_w_ffn_in': 'new_m', 'new_m_w_ffn_out': 'new_m', 'new_m_norm_final': 'new_m', 'new_v_meta_tokens': 'new_v', 'new_v_norm_mix': 'new_v', 'new_v_w_in': 'new_v', 'new_v_conv_w': 'new_v', 'new_v_gdn_a_log': 'new_v', 'new_v_gdn_dt_bias': 'new_v', 'new_v_gdn_norm': 'new_v', 'new_v_ret_decay_logit': 'new_v', 'new_v_w_up_a': 'new_v', 'new_v_w_up_b': 'new_v', 'new_v_w_out': 'new_v', 'new_v_norm_ffn': 'new_v', 'new_v_w_ffn_in': 'new_v', 'new_v_w_ffn_out': 'new_v', 'new_v_norm_final': 'new_v'}


def _forward(args):
    return _fwd_reference(*[args[k] for k in FWD_PARAMS])


def _output_shape():
    out = _jax.eval_shape(lambda: _forward(_fwd_setup_inputs(0)))
    return out.shape, out.dtype

N_MICROBATCH = 1
ADAM_LR = 0.001
ADAM_B1 = 0.9
ADAM_B2 = 0.999
ADAM_EPS = 1e-08
ADAM_WD = 0.01
ADAM_STEP = 10
PER_EXAMPLE_BATCH_AXIS = {'x': 0, 'loss_target': 0}
SHARED_INPUTS = []
_WEIGHT_DTYPES = {'meta_tokens': _jnp.float32, 'norm_mix': _jnp.float32, 'w_in': _jnp.float32, 'conv_w': _jnp.float32, 'gdn_a_log': _jnp.float32, 'gdn_dt_bias': _jnp.float32, 'gdn_norm': _jnp.float32, 'ret_decay_logit': _jnp.float32, 'w_up_a': _jnp.float32, 'w_up_b': _jnp.float32, 'w_out': _jnp.float32, 'norm_ffn': _jnp.float32, 'w_ffn_in': _jnp.float32, 'w_ffn_out': _jnp.float32, 'norm_final': _jnp.float32}
MOMENT_SCALE = {'meta_tokens': 8.265758e-03, 'norm_mix': 2.277957e-01, 'w_in': 6.469901e-02, 'conv_w': 6.038256e-02, 'gdn_a_log': 2.669452e-01, 'gdn_dt_bias': 2.614954e-01, 'gdn_norm': 2.449413e-01, 'ret_decay_logit': 9.426384e-01, 'w_up_a': 8.544924e-02, 'w_up_b': 8.551622e-02, 'w_out': 1.206661e-01, 'norm_ffn': 1.836389e-01, 'w_ffn_in': 7.770505e-02, 'w_ffn_out': 1.267660e-01, 'norm_final': 6.388864e+01}


def _to_microbatches(a, axis):
    t = _jnp.moveaxis(a, axis, 0)
    t = t.reshape((N_MICROBATCH, t.shape[0] // N_MICROBATCH) + t.shape[1:])
    return _jnp.moveaxis(t, 1, axis + 1)


def setup_inputs(seed: int = 0) -> dict:
    inp = _fwd_setup_inputs(seed)
    key = _jax.random.fold_in(_jax.random.key(seed), 7919)
    shape, _ = _output_shape()
    out = dict(inp)
    out["loss_target"] = _jax.random.normal(_jax.random.fold_in(key, 0), shape, _jnp.float32)
    for i, name in enumerate(TWIN_WEIGHTS):
        w = inp[name].astype(_jnp.float32)
        if MOMENT_SCALE is None:
            s = _jnp.sqrt(_jnp.mean(_jnp.square(w)) + 1e-30)
        else:
            s = MOMENT_SCALE[name]
        km, kv = _jax.random.split(_jax.random.fold_in(key, i + 1))
        out[name] = w
        out["m_" + name] = s * _jax.random.normal(km, w.shape, _jnp.float32)
        out["v_" + name] = (s * s) * _jax.random.uniform(kv, w.shape, _jnp.float32, 0.5, 1.5)
    if N_MICROBATCH > 1:
        for name, axis in PER_EXAMPLE_BATCH_AXIS.items():
            out[name] = _to_microbatches(out[name], axis)
    return {'x': out['x'], 'meta_tokens': out['meta_tokens'], 'norm_mix': out['norm_mix'], 'w_in': out['w_in'], 'conv_w': out['conv_w'], 'gdn_a_log': out['gdn_a_log'], 'gdn_dt_bias': out['gdn_dt_bias'], 'gdn_norm': out['gdn_norm'], 'ret_decay_logit': out['ret_decay_logit'], 'w_up_a': out['w_up_a'], 'w_up_b': out['w_up_b'], 'w_out': out['w_out'], 'norm_ffn': out['norm_ffn'], 'w_ffn_in': out['w_ffn_in'], 'w_ffn_out': out['w_ffn_out'], 'norm_final': out['norm_final'], 'loss_target': out['loss_target'], 'm_meta_tokens': out['m_meta_tokens'], 'm_norm_mix': out['m_norm_mix'], 'm_w_in': out['m_w_in'], 'm_conv_w': out['m_conv_w'], 'm_gdn_a_log': out['m_gdn_a_log'], 'm_gdn_dt_bias': out['m_gdn_dt_bias'], 'm_gdn_norm': out['m_gdn_norm'], 'm_ret_decay_logit': out['m_ret_decay_logit'], 'm_w_up_a': out['m_w_up_a'], 'm_w_up_b': out['m_w_up_b'], 'm_w_out': out['m_w_out'], 'm_norm_ffn': out['m_norm_ffn'], 'm_w_ffn_in': out['m_w_ffn_in'], 'm_w_ffn_out': out['m_w_ffn_out'], 'm_norm_final': out['m_norm_final'], 'v_meta_tokens': out['v_meta_tokens'], 'v_norm_mix': out['v_norm_mix'], 'v_w_in': out['v_w_in'], 'v_conv_w': out['v_conv_w'], 'v_gdn_a_log': out['v_gdn_a_log'], 'v_gdn_dt_bias': out['v_gdn_dt_bias'], 'v_gdn_norm': out['v_gdn_norm'], 'v_ret_decay_logit': out['v_ret_decay_logit'], 'v_w_up_a': out['v_w_up_a'], 'v_w_up_b': out['v_w_up_b'], 'v_w_out': out['v_w_out'], 'v_norm_ffn': out['v_norm_ffn'], 'v_w_ffn_in': out['v_w_ffn_in'], 'v_w_ffn_out': out['v_w_ffn_out'], 'v_norm_final': out['v_norm_final']}


def _loss(weights, diff, rest, loss_target):
    with _jax.named_scope("forward"):
        args = {**rest, TWIN_DIFF_INPUT: diff, **{k: w.astype(_WEIGHT_DTYPES[k]) for k, w in weights.items()}}
        y = _forward(args)
    with _jax.named_scope("loss_head"):
        err = _jnp.square(y.astype(_jnp.float32) - loss_target)
        return 0.5 * _jnp.sum(_jnp.mean(err, axis=-1)) if err.ndim else 0.5 * err


def _adamw(w, g, m, v):
    m = ADAM_B1 * m + (1.0 - ADAM_B1) * g
    v = ADAM_B2 * v + (1.0 - ADAM_B2) * _jnp.square(g)
    m_hat = m / (1.0 - ADAM_B1 ** ADAM_STEP)
    v_hat = v / (1.0 - ADAM_B2 ** ADAM_STEP)
    delta = -ADAM_LR * (m_hat / (_jnp.sqrt(v_hat) + ADAM_EPS) + ADAM_WD * w)
    return delta, m, v


def reference(x, meta_tokens, norm_mix, w_in, conv_w, gdn_a_log, gdn_dt_bias, gdn_norm, ret_decay_logit, w_up_a, w_up_b, w_out, norm_ffn, w_ffn_in, w_ffn_out, norm_final, loss_target, m_meta_tokens, m_norm_mix, m_w_in, m_conv_w, m_gdn_a_log, m_gdn_dt_bias, m_gdn_norm, m_ret_decay_logit, m_w_up_a, m_w_up_b, m_w_out, m_norm_ffn, m_w_ffn_in, m_w_ffn_out, m_norm_final, v_meta_tokens, v_norm_mix, v_w_in, v_conv_w, v_gdn_a_log, v_gdn_dt_bias, v_gdn_norm, v_ret_decay_logit, v_w_up_a, v_w_up_b, v_w_out, v_norm_ffn, v_w_ffn_in, v_w_ffn_out, v_norm_final):
    given = dict(x=x, meta_tokens=meta_tokens, norm_mix=norm_mix, w_in=w_in, conv_w=conv_w, gdn_a_log=gdn_a_log, gdn_dt_bias=gdn_dt_bias, gdn_norm=gdn_norm, ret_decay_logit=ret_decay_logit, w_up_a=w_up_a, w_up_b=w_up_b, w_out=w_out, norm_ffn=norm_ffn, w_ffn_in=w_ffn_in, w_ffn_out=w_ffn_out, norm_final=norm_final, loss_target=loss_target, m_meta_tokens=m_meta_tokens, m_norm_mix=m_norm_mix, m_w_in=m_w_in, m_conv_w=m_conv_w, m_gdn_a_log=m_gdn_a_log, m_gdn_dt_bias=m_gdn_dt_bias, m_gdn_norm=m_gdn_norm, m_ret_decay_logit=m_ret_decay_logit, m_w_up_a=m_w_up_a, m_w_up_b=m_w_up_b, m_w_out=m_w_out, m_norm_ffn=m_norm_ffn, m_w_ffn_in=m_w_ffn_in, m_w_ffn_out=m_w_ffn_out, m_norm_final=m_norm_final, v_meta_tokens=v_meta_tokens, v_norm_mix=v_norm_mix, v_w_in=v_w_in, v_conv_w=v_conv_w, v_gdn_a_log=v_gdn_a_log, v_gdn_dt_bias=v_gdn_dt_bias, v_gdn_norm=v_gdn_norm, v_ret_decay_logit=v_ret_decay_logit, v_w_up_a=v_w_up_a, v_w_up_b=v_w_up_b, v_w_out=v_w_out, v_norm_ffn=v_norm_ffn, v_w_ffn_in=v_w_ffn_in, v_w_ffn_out=v_w_ffn_out, v_norm_final=v_norm_final)
    weights = {n: given[n] for n in TWIN_WEIGHTS}
    shared = {n: given[n] for n in SHARED_INPUTS}
    per_example = {n: given[n] for n in ['x']}
    grad_fn = _jax.value_and_grad(_loss, argnums=(0, 1))

    def one_microbatch(ex, loss_target):
        ex = dict(ex)
        diff = ex.pop(TWIN_DIFF_INPUT)
        return grad_fn(weights, diff, {**shared, **ex}, loss_target)

    if N_MICROBATCH == 1:
        loss, (grad_w, grad_x) = one_microbatch(per_example, given["loss_target"])
    else:
        def body(carry, xs):
            loss_sum, grad_sum = carry
            l_k, (gw_k, gx_k) = one_microbatch(xs[0], xs[1])
            with _jax.named_scope("update"):
                return (loss_sum + l_k, _jax.tree.map(_jnp.add, grad_sum, gw_k)), gx_k

        init = (_jnp.zeros((), _jnp.float32), _jax.tree.map(_jnp.zeros_like, weights))
        (loss, grad_w), grad_x = _jax.lax.scan(body, init, (per_example, given["loss_target"]))
    with _jax.named_scope("update"):
        delta_w, new_m, new_v = {}, {}, {}
        for n in TWIN_WEIGHTS:
            delta_w[n], new_m[n], new_v[n] = _adamw(weights[n], grad_w[n], given["m_" + n], given["v_" + n])
    return (loss, grad_x, *[grad_w[n] for n in TWIN_WEIGHTS], *[delta_w[n] for n in TWIN_WEIGHTS],
            *[new_m[n] for n in TWIN_WEIGHTS], *[new_v[n] for n in TWIN_WEIGHTS])
```

```python
import functools
import math

import jax
import jax.numpy as jnp
from jax import lax
from jax.experimental import pallas as pl
from jax.experimental.pallas import tpu as pltpu

F32 = jnp.float32
BF16 = jnp.bfloat16
HIGHEST = lax.Precision.HIGHEST

LANES = 128
CHUNK = 64
N_META = 16
N_PAD = CHUNK - N_META
CONV_K = 5
EPS = 1e-6
ROPE_BASE = 10000.0
N_DEV = 8
VMEM_LIMIT = 56 * 1024 * 1024

ADAM_LR, ADAM_B1, ADAM_B2, ADAM_EPS, ADAM_WD, ADAM_STEP = 0.001, 0.9, 0.999, 1e-08, 0.01, 10


def _tile(n, cap, mult):
    if n <= cap:
        return n
    best = None
    for t in range(mult, cap + 1, mult):
        if n % t == 0:
            best = t
    assert best is not None, (n, cap, mult)
    return best


def _params(sem):
    return pltpu.CompilerParams(dimension_semantics=sem, vmem_limit_bytes=VMEM_LIMIT)


def _raw_dot(a, b, ca, cb, exact):
    if exact:
        return lax.dot_general(a, b, (((ca,), (cb,)), ((), ())), precision=HIGHEST, preferred_element_type=F32)
    return lax.dot_general(a.astype(BF16), b.astype(BF16), (((ca,), (cb,)), ((), ())), preferred_element_type=F32)


@functools.partial(jax.custom_vjp, nondiff_argnums=(2, 3, 4))
def _dot(a, b, ca, cb, exact):
    return _raw_dot(a, b, ca, cb, exact)


def _dot_fwd(a, b, ca, cb, exact):
    return _raw_dot(a, b, ca, cb, exact), (a, b)


def _dot_bwd(ca, cb, exact, res, g):
    a, b = res
    if ca == 1:
        da = _raw_dot(g, b, 1, 1 if cb == 0 else 0, exact)
    else:
        da = _raw_dot(b, g, 1 if cb == 0 else 0, 1, exact)
    if cb == 0:
        db = _raw_dot(a, g, 0 if ca == 1 else 1, 0, exact)
    else:
        db = _raw_dot(g, a, 0, 0 if ca == 1 else 1, exact)
    return da, db


_dot.defvjp(_dot_fwd, _dot_bwd)


def _bdot(a, b, ca=1, cb=0):
    return _dot(a, b, ca, cb, False)


def _xdot(a, b, ca=1, cb=0):
    return _dot(a, b, ca, cb, True)


def matmul(a, b, *, ta=False, tb=False, add=None, name):
    m, k = (a.shape[1], a.shape[0]) if ta else a.shape
    k2, n = (b.shape[1], b.shape[0]) if tb else b.shape
    assert k == k2, (a.shape, b.shape, ta, tb)
    tm = _tile(m, 1040, 128 if ta else 8)
    tn = _tile(n, 512, 128)
    tk = _tile(k, 1664, 128 if (not ta or tb) else 16)
    nk = k // tk
    ca, cb = (0 if ta else 1), (1 if tb else 0)
    a_spec = pl.BlockSpec((tk, tm), lambda i, j, kk: (kk, i)) if ta else pl.BlockSpec((tm, tk), lambda i, j, kk: (i, kk))
    b_spec = pl.BlockSpec((tn, tk), lambda i, j, kk: (j, kk)) if tb else pl.BlockSpec((tk, tn), lambda i, j, kk: (kk, j))
    o_spec = pl.BlockSpec((tm, tn), lambda i, j, kk: (i, j))
    has_add = add is not None

    def body(*refs):
        a_ref, b_ref = refs[0], refs[1]
        add_ref = refs[2] if has_add else None
        o_ref = refs[3] if has_add else refs[2]
        part = _raw_dot(a_ref[...], b_ref[...], ca, cb, False)
        if nk == 1:
            o_ref[...] = part + add_ref[...] if has_add else part
            return
        acc_ref = refs[-1]
        kk = pl.program_id(2)

        @pl.when(kk == 0)
        def _():
            acc_ref[...] = part

        @pl.when(kk > 0)
        def _():
            acc_ref[...] += part

        @pl.when(kk == nk - 1)
        def _():
            o_ref[...] = acc_ref[...] + add_ref[...] if has_add else acc_ref[...]

    ins = [a, b] + ([add] if has_add else [])
    in_specs = [a_spec, b_spec] + ([o_spec] if has_add else [])
    return pl.pallas_call(
        body, name=name, grid=(m // tm, n // tn, nk), in_specs=in_specs, out_specs=o_spec,
        out_shape=jax.ShapeDtypeStruct((m, n), F32),
        scratch_shapes=[pltpu.VMEM((tm, tn), F32)] if nk > 1 else [],
        compiler_params=_params(("parallel", "parallel", "arbitrary")),
    )(*ins)


class Row:
    def __init__(self, arr, bc, off=0, per_head=True, diff=True):
        self.arr, self.bc, self.off, self.per_head, self.diff = arr, bc, off, per_head, diff


class Tab:
    def __init__(self, arr):
        self.arr = arr


def _row_specs(rows, tabs, pars, tm):
    specs = []
    for r in rows:
        specs.append(pl.BlockSpec((1, tm, r.bc), functools.partial(
            lambda b, i, h, off, ph: (b, i, off + (h if ph else 0)), off=r.off, ph=r.per_head)))
    for t in tabs:
        specs.append(pl.BlockSpec((tm, t.arr.shape[1]), lambda b, i, h: (i, 0)))
    for p in pars:
        specs.append(pl.BlockSpec(p.shape, lambda b, i, h: (0, 0)))
    return specs


def rowwise(name, fn, rows, tabs, pars, outs, nh, tm):
    bsz, lp = rows[0].arr.shape[:2]
    nr, nt, npar = len(rows), len(tabs), len(pars)

    def body(*refs):
        t0 = pl.program_id(1) * tm
        ins = [refs[k][0] for k in range(nr)] + [refs[nr + k][...] for k in range(nt + npar)]
        res = fn(t0, *ins)
        for o_ref, o in zip(refs[nr + nt + npar:], res):
            o_ref[0] = o

    return pl.pallas_call(
        body, name=name, grid=(bsz, lp // tm, nh),
        in_specs=_row_specs(rows, tabs, pars, tm),
        out_specs=[pl.BlockSpec((1, tm, bc), lambda b, i, h: (b, i, h)) for _, bc in outs],
        out_shape=[jax.ShapeDtypeStruct((bsz, lp, c), F32) for c, _ in outs],
        compiler_params=_params(("parallel", "parallel", "parallel")),
    )(*[r.arr for r in rows], *[t.arr for t in tabs], *pars)


def rowwise_vjp(name, fn, rows, tabs, pars, couts, nh, tm, adds=None):
    bsz, lp = rows[0].arr.shape[:2]
    nr, nt, npar, nco = len(rows), len(tabs), len(pars), len(couts)
    adds = adds or {}
    add_keys = sorted(adds)
    diff_idx = [k for k, r in enumerate(rows) if r.diff]
    for k in diff_idx:
        assert rows[k].per_head or nh == 1

    def body(*refs):
        b, i, h = pl.program_id(0), pl.program_id(1), pl.program_id(2)
        t0 = i * tm
        pos = 0
        row_v = [refs[k][0] for k in range(nr)]
        pos += nr
        tab_v = [refs[pos + k][...] for k in range(nt)]
        pos += nt
        par_v = [refs[pos + k][...] for k in range(npar)]
        pos += npar
        co_v = [refs[pos + k][0] for k in range(nco)]
        pos += nco
        add_v = {key: refs[pos + k][0] for k, key in enumerate(add_keys)}
        pos += len(add_keys)
        drow_refs = refs[pos:pos + len(diff_idx)]
        dpar_refs = refs[pos + len(diff_idx):]

        def f(dvals, pvals):
            full = list(row_v)
            for k, v in zip(diff_idx, dvals):
                full[k] = v
            return tuple(fn(t0, *full, *tab_v, *pvals))

        _, pull = jax.vjp(f, [row_v[k] for k in diff_idx], par_v)
        d_rows, d_pars = pull(tuple(co_v))
        for ref, k, d in zip(drow_refs, diff_idx, d_rows):
            ref[0] = d + add_v[k] if k in add_v else d
        first = jnp.logical_and(jnp.logical_and(b == 0, i == 0), h == 0)
        for ref, d in zip(dpar_refs, d_pars):
            @pl.when(first)
            def _(ref=ref, d=d):
                ref[...] = d

            @pl.when(jnp.logical_not(first))
            def _(ref=ref, d=d):
                ref[...] += d

    out_block = lambda bc: pl.BlockSpec((1, tm, bc), lambda b, i, h: (b, i, h))
    in_specs = _row_specs(rows, tabs, pars, tm)
    in_specs += [out_block(c.shape[2] // nh) for c in couts]
    in_specs += [out_block(rows[k].bc) for k in add_keys]
    out_specs = [out_block(rows[k].bc) for k in diff_idx]
    out_specs += [pl.BlockSpec(p.shape, lambda b, i, h: (0, 0)) for p in pars]
    out_shape = [jax.ShapeDtypeStruct((bsz, lp, nh * rows[k].bc), F32) for k in diff_idx]
    out_shape += [jax.ShapeDtypeStruct(p.shape, F32) for p in pars]
    res = pl.pallas_call(
        body, name=name, grid=(bsz, lp // tm, nh), in_specs=in_specs, out_specs=out_specs, out_shape=out_shape,
        compiler_params=_params(("arbitrary", "arbitrary", "arbitrary")),
    )(*[r.arr for r in rows], *[t.arr for t in tabs], *pars, *couts, *[adds[k] for k in add_keys])
    return res[:len(diff_idx)], res[len(diff_idx):]


def _real_rows(t0, tm):
    return (t0 + lax.broadcasted_iota(jnp.int32, (tm, 1), 0)) >= N_PAD


def f_rms(t0, x, gain):
    return (x * lax.rsqrt(jnp.mean(x * x, axis=-1, keepdims=True) + EPS) * gain,)


def make_f_gb(heads):
    def f_gb(t0, ab, alog, dtb):
        lane = lax.broadcasted_iota(jnp.int32, ab.shape, 1)
        g = -jnp.exp(alog) * jax.nn.softplus(ab + dtb)
        beta = jax.nn.sigmoid(ab)
        out = jnp.where(lane < 2 * heads, g, jnp.where(lane < 4 * heads, beta, 0.0))
        return (jnp.where(_real_rows(t0, ab.shape[0]), out, 0.0),)
    return f_gb


def f_gdn_out(t0, o, z, gain):
    on = o * lax.rsqrt(jnp.mean(o * o, axis=-1, keepdims=True) + EPS)
    return (on * gain * jax.nn.silu(z),)


def f_ret_out(t0, o, g):
    on = o * lax.rsqrt(jnp.mean(o * o, axis=-1, keepdims=True) + EPS)
    return (on * jax.nn.silu(g),)


def f_merge(t0, ga, gb, ya, yb):
    return (jax.nn.sigmoid(ga) * ya + jax.nn.sigmoid(gb) * yb,)


def f_swiglu(t0, gate, up):
    return (jax.nn.silu(gate) * up,)


def make_f_rope(sign):
    def f_rope(t0, x, cos, sin):
        half = x.shape[1] // 2
        x1, x2 = x[:, :half], x[:, half:]
        s = sin * sign
        return (jnp.concatenate([x1 * cos - x2 * s, x1 * s + x2 * cos], axis=1),)
    return f_rope


def loss_head(h, gain, target, tm):
    bsz, lp, d = h.shape

    def body(h_ref, g_ref, t_ref, dh_ref, dg_ref, loss_ref):
        b, i = pl.program_id(0), pl.program_id(1)
        rows = (i * tm + lax.broadcasted_iota(jnp.int32, (tm, 1), 0)) >= CHUNK
        tgt = t_ref[0]

        def f(x, gain_v):
            y = f_rms(0, x, gain_v)[0]
            err = jnp.where(rows, y - tgt, 0.0)
            return 0.5 * jnp.sum(jnp.mean(err * err, axis=-1, keepdims=True), keepdims=True)

        val, pull = jax.vjp(f, h_ref[0], g_ref[...])
        dh, dg = pull(jnp.ones((1, 1), F32))
        dh_ref[0] = dh
        first = jnp.logical_and(b == 0, i == 0)
        val_row = jnp.broadcast_to(val, (1, LANES))

        @pl.when(first)
        def _():
            dg_ref[...] = dg
            loss_ref[...] = val_row

        @pl.when(jnp.logical_not(first))
        def _():
            dg_ref[...] += dg
            loss_ref[...] += val_row

    blk = pl.BlockSpec((1, tm, d), lambda b, i: (b, i, 0))
    return pl.pallas_call(
        body, name="loss_head", grid=(bsz, lp // tm),
        in_specs=[blk, pl.BlockSpec((1, d), lambda b, i: (0, 0)), blk],
        out_specs=[blk, pl.BlockSpec((1, d), lambda b, i: (0, 0)), pl.BlockSpec((1, LANES), lambda b, i: (0, 0))],
        out_shape=[jax.ShapeDtypeStruct((bsz, lp, d), F32), jax.ShapeDtypeStruct((1, d), F32),
                   jax.ShapeDtypeStruct((1, LANES), F32)],
        compiler_params=_params(("arbitrary", "arbitrary")),
    )(h, gain, target)


def _conv_pre(x, w_ref):
    lp = x.shape[0]
    acc = w_ref[2:3, :] * x
    for k in (0, 1, 3, 4):
        acc = acc + w_ref[k:k + 1, :] * pltpu.roll(x, (2 - k) % lp, 0)
    return acc


def conv_fwd(proj, off, w, l2, name):
    bsz, lp, _ = proj.shape
    d = w.shape[1]

    def body(x_ref, w_ref, o_ref):
        x = x_ref[0]
        s = jnp.where(_real_rows(0, lp), jax.nn.silu(_conv_pre(x, w_ref)), 0.0)
        if l2:
            s = s * lax.rsqrt(jnp.sum(s * s, axis=-1, keepdims=True) + EPS)
        o_ref[0] = s

    return pl.pallas_call(
        body, name=name, grid=(d // LANES, bsz),
        in_specs=[pl.BlockSpec((1, lp, LANES), lambda j, b: (b, 0, off + j)),
                  pl.BlockSpec((CONV_K, LANES), lambda j, b: (0, j))],
        out_specs=pl.BlockSpec((1, lp, LANES), lambda j, b: (b, 0, j)),
        out_shape=jax.ShapeDtypeStruct((bsz, lp, d), F32),
        compiler_params=_params(("parallel", "parallel")),
    )(proj, w)


def conv_bwd(proj, off, w, dy, l2, name):
    bsz, lp, _ = proj.shape
    d = w.shape[1]

    def body(x_ref, w_ref, dy_ref, dx_ref, dw_ref):
        b = pl.program_id(1)
        x, g = x_ref[0], dy_ref[0]
        real = _real_rows(0, lp)
        c = _conv_pre(x, w_ref)
        sg = jax.nn.sigmoid(c)
        s = jnp.where(real, c * sg, 0.0)
        if l2:
            r = lax.rsqrt(jnp.sum(s * s, axis=-1, keepdims=True) + EPS)
            g = r * g - s * (r * r * r) * jnp.sum(g * s, axis=-1, keepdims=True)
        dc = jnp.where(real, g * (sg * (1.0 + c * (1.0 - sg))), 0.0)
        dx = w_ref[2:3, :] * dc
        for k in (0, 1, 3, 4):
            dx = dx + w_ref[k:k + 1, :] * pltpu.roll(dc, (k - 2) % lp, 0)
        dx_ref[0] = jnp.where(real, dx, 0.0)
        tap_row = lax.broadcasted_iota(jnp.int32, (CONV_K, LANES), 0)
        dw = jnp.zeros((CONV_K, LANES), F32)
        for k in range(CONV_K):
            xs = x if k == 2 else pltpu.roll(x, (2 - k) % lp, 0)
            dw = dw + jnp.where(tap_row == k, jnp.sum(dc * xs, axis=0, keepdims=True), 0.0)

        @pl.when(b == 0)
        def _():
            dw_ref[...] = dw

        @pl.when(b > 0)
        def _():
            dw_ref[...] += dw

    blk = pl.BlockSpec((1, lp, LANES), lambda j, b: (b, 0, j))
    return pl.pallas_call(
        body, name=name, grid=(d // LANES, bsz),
        in_specs=[pl.BlockSpec((1, lp, LANES), lambda j, b: (b, 0, off + j)),
                  pl.BlockSpec((CONV_K, LANES), lambda j, b: (0, j)), blk],
        out_specs=[blk, pl.BlockSpec((CONV_K, LANES), lambda j, b: (0, j))],
        out_shape=[jax.ShapeDtypeStruct((bsz, lp, d), F32), jax.ShapeDtypeStruct((CONV_K, d), F32)],
        compiler_params=_params(("parallel", "arbitrary")),
    )(proj, w, dy)


def _tri_masks(rev):
    ii = lax.broadcasted_iota(jnp.int32, (CHUNK, CHUNK), 0)
    jj = lax.broadcasted_iota(jnp.int32, (CHUNK, CHUNK), 1)
    return ((ii <= jj), (ii < jj)) if rev else ((ii >= jj), (ii > jj))


def _lane_pick(block, lane):
    sel = lax.broadcasted_iota(jnp.int32, block.shape, 1) == lane
    return jnp.sum(jnp.where(sel, block, 0.0), axis=1, keepdims=True)


def _gdn_chunk(q, k, v, gb, s, g_lane, b_lane, rev):
    dk = q.shape[1]
    incl, strict = _tri_masks(rev)
    tri = incl.astype(F32)
    g = _lane_pick(gb, g_lane)
    beta = _lane_pick(gb, b_lane)
    q = q * (dk ** -0.5)
    gc_sq = _xdot(tri, jnp.broadcast_to(g, (CHUNK, CHUNK)))
    gc_sq_t = _xdot(jnp.broadcast_to(g, (CHUNK, CHUNK)), tri, 0, 1)
    gc = _xdot(tri, jnp.broadcast_to(g, (CHUNK, dk)))
    g_last = jnp.sum(g, axis=0, keepdims=True)
    decay = jnp.where(incl, jnp.exp(jnp.where(incl, gc_sq - gc_sq_t, 0.0)), 0.0)
    kb = k * beta
    m = jnp.where(strict, _bdot(kb, k, 1, 1) * decay, 0.0)
    eye = (lax.broadcasted_iota(jnp.int32, (CHUNK, CHUNK), 0)
           == lax.broadcasted_iota(jnp.int32, (CHUNK, CHUNK), 1)).astype(F32)
    p = -m
    inv = eye + p
    for _ in range(5):
        p = _xdot(p, p)
        inv = inv + _xdot(inv, p)
    e_gc = jnp.exp(gc)
    u = _bdot(inv, v * beta)
    w = _bdot(inv, kb * e_gc)
    qk = _bdot(q, k, 1, 1) * decay
    q_dec = q * e_gc
    k_dec = k * jnp.exp(g_last - gc)
    v_new = u - _bdot(w, s)
    o = _bdot(q_dec, s) + _bdot(qk, v_new)
    s_new = s * jnp.exp(g_last) + _bdot(k_dec, v_new, 0, 0)
    return o, s_new


def gdn_fwd(q, k, v, gb, heads, direction, o_prev, name):
    bsz, lp, d = q.shape
    nc = lp // CHUNK
    rev = direction == 1
    cm = (lambda n: nc - 1 - n) if rev else (lambda n: n)
    has_prev = o_prev is not None

    def body(*refs):
        q_ref, k_ref, v_ref, gb_ref = refs[:4]
        prev_ref = refs[4] if has_prev else None
        o_ref, st_ref, s_ref = refs[-3], refs[-2], refs[-1]
        n = pl.program_id(1)

        @pl.when(n == 0)
        def _():
            s_ref[...] = jnp.zeros_like(s_ref)

        gbv = gb_ref[0]
        for h in range(heads):
            sl = slice(h * LANES, (h + 1) * LANES)
            s = s_ref[h]
            st_ref[0, 0, h] = s
            o, s_new = _gdn_chunk(q_ref[0, :, sl], k_ref[0, :, sl], v_ref[0, :, sl], gbv, s,
                                  direction * heads + h, 2 * heads + direction * heads + h, rev)
            s_ref[h] = s_new
            o_ref[0, :, sl] = o + prev_ref[0, :, sl] if has_prev else o

    blk = pl.BlockSpec((1, CHUNK, d), lambda b, n: (b, cm(n), 0))
    gblk = pl.BlockSpec((1, CHUNK, LANES), lambda b, n: (b, cm(n), 0))
    st_blk = pl.BlockSpec((1, 1, heads, LANES, LANES), lambda b, n: (b, cm(n), 0, 0, 0))
    return pl.pallas_call(
        body, name=name, grid=(bsz, nc),
        in_specs=[blk, blk, blk, gblk] + ([blk] if has_prev else []),
        out_specs=[blk, st_blk],
        out_shape=[jax.ShapeDtypeStruct((bsz, lp, d), F32),
                   jax.ShapeDtypeStruct((bsz, nc, heads, LANES, LANES), F32)],
        scratch_shapes=[pltpu.VMEM((heads, LANES, LANES), F32)],
        compiler_params=_params(("parallel", "arbitrary")),
    )(q, k, v, gb, *([o_prev] if has_prev else []))


def gdn_bwd(q, k, v, gb, states, do, heads, direction, prev, name):
    bsz, lp, d = q.shape
    nc = lp // CHUNK
    rev = direction == 1
    cm = (lambda n: n) if rev else (lambda n: nc - 1 - n)
    has_prev = prev is not None

    def body(*refs):
        q_ref, k_ref, v_ref, gb_ref, st_ref, do_ref = refs[:6]
        prev_refs = refs[6:10] if has_prev else None
        dq_ref, dk_ref, dv_ref, dgb_ref, ds_ref = refs[-5:]
        n = pl.program_id(1)

        @pl.when(n == 0)
        def _():
            ds_ref[...] = jnp.zeros_like(ds_ref)

        gbv = gb_ref[0]
        dgb = prev_refs[3][0] if has_prev else jnp.zeros((CHUNK, LANES), F32)
        for h in range(heads):
            sl = slice(h * LANES, (h + 1) * LANES)
            f = functools.partial(_gdn_chunk, g_lane=direction * heads + h,
                                  b_lane=2 * heads + direction * heads + h, rev=rev)
            _, pull = jax.vjp(f, q_ref[0, :, sl], k_ref[0, :, sl], v_ref[0, :, sl], gbv, st_ref[0, 0, h])
            dq, dk, dv, dgb_h, ds = pull((do_ref[0, :, sl], ds_ref[h]))
            ds_ref[h] = ds
            dgb = dgb + dgb_h
            if has_prev:
                dq, dk, dv = dq + prev_refs[0][0, :, sl], dk + prev_refs[1][0, :, sl], dv + prev_refs[2][0, :, sl]
            dq_ref[0, :, sl] = dq
            dk_ref[0, :, sl] = dk
            dv_ref[0, :, sl] = dv
        dgb_ref[0] = dgb

    blk = pl.BlockSpec((1, CHUNK, d), lambda b, n: (b, cm(n), 0))
    gblk = pl.BlockSpec((1, CHUNK, LANES), lambda b, n: (b, cm(n), 0))
    st_blk = pl.BlockSpec((1, 1, heads, LANES, LANES), lambda b, n: (b, cm(n), 0, 0, 0))
    big = jax.ShapeDtypeStruct((bsz, lp, d), F32)
    return pl.pallas_call(
        body, name=name, grid=(bsz, nc),
        in_specs=[blk, blk, blk, gblk, st_blk, blk] + ([blk, blk, blk, gblk] if has_prev else []),
        out_specs=[blk, blk, blk, gblk],
        out_shape=[big, big, big, jax.ShapeDtypeStruct((bsz, lp, LANES), F32)],
        scratch_shapes=[pltpu.VMEM((heads, LANES, LANES), F32)],
        compiler_params=_params(("parallel", "arbitrary")),
    )(q, k, v, gb, states, do, *(list(prev) if has_prev else []))


def _ret_chunk(q, k, v, r, logit, lane, rev):
    dk = q.shape[1]
    lg = jax.nn.log_sigmoid(_lane_pick(logit, lane))
    k = k * (dk ** -0.5)
    ii = lax.broadcasted_iota(jnp.int32, (CHUNK, CHUNK), 0)
    jj = lax.broadcasted_iota(jnp.int32, (CHUNK, CHUNK), 1)
    pos = lax.broadcasted_iota(jnp.int32, (CHUNK, 1), 0)
    if rev:
        incl, rel = ii <= jj, (jj - ii)
        seen = (CHUNK - 1 - pos)
    else:
        incl, rel = ii >= jj, (ii - jj)
        seen = pos
    relf = jnp.where(incl, rel, 0).astype(F32)
    seenf = seen.astype(F32)
    intra = jnp.where(incl, jnp.exp(relf * lg), 0.0)
    qk = _bdot(q, k, 1, 1) * intra
    q_dec = q * jnp.exp(lg * (seenf + 1.0))
    k_dec = k * jnp.exp(lg * (CHUNK - 1.0 - seenf))
    o = _bdot(q_dec, r) + _bdot(qk, v)
    r_new = r * jnp.exp(lg * CHUNK) + _bdot(k_dec, v, 0, 0)
    return o, r_new


def ret_fwd(qk, v_arr, v_off, logit, heads, direction, o_prev, name):
    bsz, lp, d2 = qk.shape
    d = d2 // 2
    dkh, dvh = d // heads, 2 * d // heads
    nc = lp // CHUNK
    rev = direction == 1
    cm = (lambda n: nc - 1 - n) if rev else (lambda n: n)
    has_prev = o_prev is not None
    v_cb = v_off * LANES // (2 * d)
    assert v_cb * 2 * d == v_off * LANES

    def body(*refs):
        q_ref, k_ref, v_ref, lg_ref = refs[:4]
        prev_ref = refs[4] if has_prev else None
        o_ref, st_ref, r_ref = refs[-3], refs[-2], refs[-1]
        n = pl.program_id(1)

        @pl.when(n == 0)
        def _():
            r_ref[...] = jnp.zeros_like(r_ref)

        lgv = lg_ref[...]
        for h in range(heads):
            ks, vs = slice(h * dkh, (h + 1) * dkh), slice(h * dvh, (h + 1) * dvh)
            r = r_ref[h]
            st_ref[0, 0, h] = r
            o, r_new = _ret_chunk(q_ref[0, :, ks], k_ref[0, :, ks], v_ref[0, :, vs], r, lgv, h, rev)
            r_ref[h] = r_new
            o_ref[0, :, vs] = o + prev_ref[0, :, vs] if has_prev else o

    qblk = pl.BlockSpec((1, CHUNK, d), lambda b, n: (b, cm(n), 0))
    kblk = pl.BlockSpec((1, CHUNK, d), lambda b, n: (b, cm(n), 1))
    vblk = pl.BlockSpec((1, CHUNK, 2 * d), lambda b, n: (b, cm(n), v_cb))
    oblk = pl.BlockSpec((1, CHUNK, 2 * d), lambda b, n: (b, cm(n), 0))
    st_blk = pl.BlockSpec((1, 1, heads, dkh, dvh), lambda b, n: (b, cm(n), 0, 0, 0))
    return pl.pallas_call(
        body, name=name, grid=(bsz, nc),
        in_specs=[qblk, kblk, vblk, pl.BlockSpec((1, LANES), lambda b, n: (0, 0))] + ([oblk] if has_prev else []),
        out_specs=[oblk, st_blk],
        out_shape=[jax.ShapeDtypeStruct((bsz, lp, 2 * d), F32),
                   jax.ShapeDtypeStruct((bsz, nc, heads, dkh, dvh), F32)],
        scratch_shapes=[pltpu.VMEM((heads, dkh, dvh), F32)],
        compiler_params=_params(("parallel", "arbitrary")),
    )(qk, qk, v_arr, logit, *([o_prev] if has_prev else []))


def ret_bwd(qk, v_arr, v_off, logit, states, do, heads, direction, prev, name):
    bsz, lp, d2 = qk.shape
    d = d2 // 2
    dkh, dvh = d // heads, 2 * d // heads
    nc = lp // CHUNK
    rev = direction == 1
    cm = (lambda n: n) if rev else (lambda n: nc - 1 - n)
    has_prev = prev is not None
    v_cb = v_off * LANES // (2 * d)

    def body(*refs):
        q_ref, k_ref, v_ref, lg_ref, st_ref, do_ref = refs[:6]
        prev_refs = refs[6:9] if has_prev else None
        dq_ref, dk_ref, dv_ref, dlg_ref, dr_ref = refs[-5:]
        b, n = pl.program_id(0), pl.program_id(1)

        @pl.when(n == 0)
        def _():
            dr_ref[...] = jnp.zeros_like(dr_ref)

        lgv = lg_ref[...]
        dlg = jnp.zeros((1, LANES), F32)
        for h in range(heads):
            ks, vs = slice(h * dkh, (h + 1) * dkh), slice(h * dvh, (h + 1) * dvh)
            f = functools.partial(_ret_chunk, lane=h, rev=rev)
            _, pull = jax.vjp(f, q_ref[0, :, ks], k_ref[0, :, ks], v_ref[0, :, vs], st_ref[0, 0, h], lgv)
            dq, dk, dv, dr, dlg_h = pull((do_ref[0, :, vs], dr_ref[h]))
            dr_ref[h] = dr
            dlg = dlg + dlg_h
            if has_prev:
                dq, dk, dv = dq + prev_refs[0][0, :, ks], dk + prev_refs[1][0, :, ks], dv + prev_refs[2][0, :, vs]
            dq_ref[0, :, ks] = dq
            dk_ref[0, :, ks] = dk
            dv_ref[0, :, vs] = dv
        first = jnp.logical_and(b == 0, n == 0)

        @pl.when(first)
        def _():
            dlg_ref[...] = dlg

        @pl.when(jnp.logical_not(first))
        def _():
            dlg_ref[...] += dlg

    qblk = pl.BlockSpec((1, CHUNK, d), lambda b, n: (b, cm(n), 0))
    kblk = pl.BlockSpec((1, CHUNK, d), lambda b, n: (b, cm(n), 1))
    vblk = pl.BlockSpec((1, CHUNK, 2 * d), lambda b, n: (b, cm(n), v_cb))
    oblk = pl.BlockSpec((1, CHUNK, 2 * d), lambda b, n: (b, cm(n), 0))
    lblk = pl.BlockSpec((1, LANES), lambda b, n: (0, 0))
    st_blk = pl.BlockSpec((1, 1, heads, dkh, dvh), lambda b, n: (b, cm(n), 0, 0, 0))
    return pl.pallas_call(
        body, name=name, grid=(bsz, nc),
        in_specs=[qblk, kblk, vblk, lblk, st_blk, oblk] + ([qblk, qblk, oblk] if has_prev else []),
        out_specs=[qblk, qblk, oblk, lblk],
        out_shape=[jax.ShapeDtypeStruct((bsz, lp, d), F32), jax.ShapeDtypeStruct((bsz, lp, d), F32),
                   jax.ShapeDtypeStruct((bsz, lp, 2 * d), F32), jax.ShapeDtypeStruct((1, LANES), F32)],
        scratch_shapes=[pltpu.VMEM((heads, dkh, dvh), F32)],
        compiler_params=_params(("arbitrary", "arbitrary")),
    )(qk, qk, v_arr, logit, states, do, *(list(prev) if has_prev else []))


def _flip(v, bit):
    return 1 - v if bit else v


def _peer(x, y, c, off):
    return (_flip(x, off & 4), _flip(y, off & 2), _flip(c, off & 1))


def all_gather_hbm(block, name):
    r, lanes = block.shape

    def body(x_ref, out_ref, send_sems, recv_sems, local_sem):
        x, y, c = lax.axis_index("x"), lax.axis_index("y"), lax.axis_index("c")
        me, sibling = (x, y, c), (x, y, 1 - c)
        chips = [(1 - x, y), (x, 1 - y), (1 - x, 1 - y)]

        def slot(px, py, pc):
            return out_ref.at[4 * px + 2 * py + pc]

        def copy(k, block_of, to, src=None):
            return pltpu.make_async_remote_copy(
                src_ref=slot(*block_of) if src is None else src, dst_ref=slot(*block_of),
                send_sem=send_sems.at[k], recv_sem=recv_sems.at[k], device_id=to, device_id_type=pl.DeviceIdType.MESH)

        mine = pltpu.make_async_copy(x_ref, slot(*me), local_sem)
        mine.start()
        first = [copy(0, me, sibling, src=x_ref)]
        first += [copy(1 + j, me, (*chip, c), src=x_ref) for j, chip in enumerate(chips)]
        for cp in first:
            cp.start()
        passed = [copy(4 + j, (*chip, c), sibling) for j, chip in enumerate(chips)]
        for j, chip in enumerate(chips):
            copy(1 + j, (*chip, c), me).wait_recv()
            passed[j].start()
        copy(0, sibling, me).wait_recv()
        for j, chip in enumerate(chips):
            copy(4 + j, (*chip, 1 - c), me).wait_recv()
        for cp in first + passed:
            cp.wait_send()
        mine.wait()

    return pl.pallas_call(
        body, name=name, out_shape=jax.ShapeDtypeStruct((N_DEV, r, lanes), block.dtype),
        in_specs=[pl.BlockSpec(memory_space=pl.ANY)], out_specs=pl.BlockSpec(memory_space=pl.ANY),
        scratch_shapes=[pltpu.SemaphoreType.DMA((7,)), pltpu.SemaphoreType.DMA((7,)), pltpu.SemaphoreType.DMA],
    )(block)


def all_gather_small(block, name):
    r, lanes = block.shape

    def body(x_ref, out_ref, send_sems, recv_sems):
        x, y, c = lax.axis_index("x"), lax.axis_index("y"), lax.axis_index("c")
        me = 4 * x + 2 * y + c
        out_ref[me] = x_ref[...]
        copies = []
        for off in range(1, N_DEV):
            copies.append(pltpu.make_async_remote_copy(
                src_ref=x_ref, dst_ref=out_ref.at[me], send_sem=send_sems.at[off - 1], recv_sem=recv_sems.at[off - 1],
                device_id=_peer(x, y, c, off), device_id_type=pl.DeviceIdType.MESH))
        for cp in copies:
            cp.start()
        for cp in copies:
            cp.wait()

    return pl.pallas_call(
        body, name=name, out_shape=jax.ShapeDtypeStruct((N_DEV, r, lanes), block.dtype),
        in_specs=[pl.BlockSpec(memory_space=pltpu.VMEM)], out_specs=pl.BlockSpec(memory_space=pltpu.VMEM),
        scratch_shapes=[pltpu.SemaphoreType.DMA((7,)), pltpu.SemaphoreType.DMA((7,))],
    )(block)


def all_to_all_hbm(pieces, name):
    def body(x_ref, out_ref, send_sems, recv_sems, local_sem):
        x, y, c = lax.axis_index("x"), lax.axis_index("y"), lax.axis_index("c")
        me = 4 * x + 2 * y + c
        mine = pltpu.make_async_copy(x_ref.at[me], out_ref.at[me], local_sem)
        mine.start()
        copies = []
        for off in range(1, N_DEV):
            px, py, pc = _peer(x, y, c, off)
            copies.append(pltpu.make_async_remote_copy(
                src_ref=x_ref.at[4 * px + 2 * py + pc], dst_ref=out_ref.at[me],
                send_sem=send_sems.at[off - 1], recv_sem=recv_sems.at[off - 1],
                device_id=(px, py, pc), device_id_type=pl.DeviceIdType.MESH))
        for cp in copies:
            cp.start()
        for cp in copies:
            cp.wait()
        mine.wait()

    return pl.pallas_call(
        body, name=name, out_shape=jax.ShapeDtypeStruct(pieces.shape, pieces.dtype),
        in_specs=[pl.BlockSpec(memory_space=pl.ANY)], out_specs=pl.BlockSpec(memory_space=pl.ANY),
        scratch_shapes=[pltpu.SemaphoreType.DMA((7,)), pltpu.SemaphoreType.DMA((7,)), pltpu.SemaphoreType.DMA],
    )(pieces)


def adamw(w, m, v, g8, name):
    r = w.shape[0]
    tm = _tile(r, 1024, 8)

    def body(w_ref, m_ref, v_ref, g_ref, g_out, d_out, m_out, v_out):
        g = g_ref[0]
        for s in range(1, N_DEV):
            g = g + g_ref[s]
        mn = ADAM_B1 * m_ref[...] + (1.0 - ADAM_B1) * g
        vn = ADAM_B2 * v_ref[...] + (1.0 - ADAM_B2) * (g * g)
        m_hat = mn / (1.0 - ADAM_B1 ** ADAM_STEP)
        v_hat = vn / (1.0 - ADAM_B2 ** ADAM_STEP)
        g_out[...] = g
        d_out[...] = -ADAM_LR * (m_hat / (jnp.sqrt(v_hat) + ADAM_EPS) + ADAM_WD * w_ref[...])
        m_out[...] = mn
        v_out[...] = vn

    blk = pl.BlockSpec((tm, LANES), lambda i: (i, 0))
    shp = jax.ShapeDtypeStruct((r, LANES), F32)
    return pl.pallas_call(
        body, name=name, grid=(r // tm,),
        in_specs=[blk, blk, blk, pl.BlockSpec((N_DEV, tm, LANES), lambda i: (0, i, 0))],
        out_specs=[blk, blk, blk, blk], out_shape=[shp, shp, shp, shp],
        compiler_params=_params(("parallel",)),
    )(w, m, v, g8)


def _pack(blocks, rows_mult):
    flat = jnp.concatenate([b.reshape(-1) for b in blocks])
    unit = rows_mult * LANES
    total = -(-flat.shape[0] // unit) * unit
    return jnp.pad(flat, (0, total - flat.shape[0])).reshape(-1, LANES)


def _unpack(packed, shapes):
    flat = packed.reshape(-1)
    out, pos = [], 0
    for s in shapes:
        n = math.prod(s)
        out.append(flat[pos:pos + n].reshape(s))
        pos += n
    return out


def _gathered_full(gathered, shapes, axes):
    per_dev = [_unpack(gathered[d], shapes) for d in range(N_DEV)]
    return [jnp.concatenate([per_dev[d][k] for d in range(N_DEV)], axis=axes[k]) for k in range(len(shapes))]


def _pieces_by_dest(fulls, axes, rows_mult):
    packs = []
    for d in range(N_DEV):
        blocks = []
        for f, ax in zip(fulls, axes):
            n = f.shape[ax] // N_DEV
            blocks.append(lax.slice_in_dim(f, d * n, (d + 1) * n, axis=ax))
        packs.append(_pack(blocks, rows_mult))
    return jnp.stack(packs)


class Layout:
    def __init__(self, d):
        self.d = d
        self.h = d // 128
        self.hr = d // 256
        self.z = 3 * d
        self.qb = 4 * d
        self.vb = 6 * d
        self.gb = 8 * d
        self.ga = 10 * d
        self.gbt = 11 * d
        self.ab = 12 * d
        self.used = 12 * d + LANES
        self.np = -(-self.used // 512) * 512

    def relayout_w_in(self, w):
        d, h4 = self.d, 4 * self.h
        return jnp.concatenate([w[:, :4 * d], w[:, 4 * d + h4:], w[:, 4 * d:4 * d + h4],
                                jnp.zeros((d, self.np - 12 * d - h4), w.dtype)], axis=1)

    def unlayout_w_in(self, w):
        d, h4 = self.d, 4 * self.h
        return jnp.concatenate([w[:, :4 * d], w[:, 12 * d:12 * d + h4], w[:, 4 * d:12 * d]], axis=1)


def _lane_row(vec):
    return jnp.pad(vec.reshape(-1), (0, LANES - vec.size)).reshape(1, LANES)


def _rope_tables(lp, half):
    inv = ROPE_BASE ** (-jnp.arange(half, dtype=F32) / half)
    pos = (jnp.arange(lp) - N_PAD).astype(F32)
    ang = pos[:, None] * inv[None, :]
    return jnp.cos(ang), jnp.sin(ang)


def local_step(x, target, meta, p):
    bsz, seq, d = x.shape
    lay = Layout(d)
    h_gdn, h_ret = lay.h, lay.hr
    lp = seq + CHUNK
    t_all = bsz * lp
    depth = p["w_up_a"].shape[0]
    ff = p["w_ffn_out"].shape[1]
    tm = _tile(lp, 512, 8)
    tmw = _tile(lp, 256, 8)
    cb = lambda cols: cols // LANES
    flat = lambda a: a.reshape(t_all, a.shape[-1])
    unflat = lambda a: a.reshape(bsz, lp, a.shape[-1])
    cos, sin = _rope_tables(lp, LANES)
    f_gb = make_f_gb(h_gdn)
    rope_f, rope_b = make_f_rope(1.0), make_f_rope(-1.0)

    head = jnp.concatenate([jnp.zeros((N_PAD, d), F32), meta], axis=0)
    h = jnp.concatenate([jnp.broadcast_to(head[None], (bsz, CHUNK, d)), x], axis=1)
    tgt = jnp.pad(target, ((0, 0), (CHUNK, 0), (0, 0)))

    saved = []
    for l in range(depth):
        s = {"h_in": h}
        nm = lambda k: f"l{l}_{k}"
        g_mix, g_ffn = p["norm_mix"][l][None], p["norm_ffn"][l][None]
        alog, dtb = _lane_row(p["gdn_a_log"][l]), _lane_row(p["gdn_dt_bias"][l])
        gain_a = p["gdn_norm"][l][None]
        logits = [_lane_row(p["ret_decay_logit"][l][0]), _lane_row(p["ret_decay_logit"][l][1])]
        cw = p["conv_w"][l]
        (hn,) = rowwise(nm("rms_mix"), f_rms, [Row(h, d)], [], [g_mix], [(d, d)], 1, tm)
        proj = unflat(matmul(flat(hn), p["w_in"][l], name=nm("mm_in")))
        qa = conv_fwd(proj, 0, cw[:, :d], True, nm("conv_q"))
        ka = conv_fwd(proj, cb(d), cw[:, d:2 * d], True, nm("conv_k"))
        va = conv_fwd(proj, cb(2 * d), cw[:, 2 * d:], False, nm("conv_v"))
        (gb,) = rowwise(nm("gb"), f_gb, [Row(proj, LANES, cb(lay.ab))], [], [alog, dtb], [(LANES, LANES)], 1, tm)
        o0, st_a0 = gdn_fwd(qa, ka, va, gb, h_gdn, 0, None, nm("gdn_f0"))
        oa, st_a1 = gdn_fwd(qa, ka, va, gb, h_gdn, 1, o0, nm("gdn_f1"))
        (oan,) = rowwise(nm("gdn_out"), f_gdn_out, [Row(oa, LANES), Row(proj, LANES, cb(lay.z))], [], [gain_a],
                         [(d, LANES)], h_gdn, tm)
        ya = unflat(matmul(flat(oan), p["w_up_a"][l], name=nm("mm_up_a")))
        (qkr,) = rowwise(nm("rope"), rope_f, [Row(proj, 2 * LANES, cb(lay.qb) // 2)], [Tab(cos), Tab(sin)], [],
                         [(2 * d, 2 * LANES)], 2 * h_ret, tm)
        r0, st_b0 = ret_fwd(qkr, proj, cb(lay.vb), logits[0], h_ret, 0, None, nm("ret_f0"))
        ob, st_b1 = ret_fwd(qkr, proj, cb(lay.vb), logits[1], h_ret, 1, r0, nm("ret_f1"))
        (obn,) = rowwise(nm("ret_out"), f_ret_out, [Row(ob, 4 * LANES), Row(proj, 4 * LANES, cb(lay.gb) // 4)], [], [],
                         [(2 * d, 4 * LANES)], h_ret, tmw)
        yb = unflat(matmul(flat(obn), p["w_up_b"][l], name=nm("mm_up_b")))
        (mg,) = rowwise(nm("merge"), f_merge,
                        [Row(proj, LANES, cb(lay.ga)), Row(proj, LANES, cb(lay.gbt)), Row(ya, LANES), Row(yb, LANES)],
                        [], [], [(d, LANES)], cb(d), tm)
        h_mid = unflat(matmul(flat(mg), p["w_out"][l], add=flat(h), name=nm("mm_out")))
        (hn2,) = rowwise(nm("rms_ffn"), f_rms, [Row(h_mid, d)], [], [g_ffn], [(d, d)], 1, tm)
        ffp = unflat(matmul(flat(hn2), p["w_ffn_in"][l], name=nm("mm_ffn_in")))
        fbc = _tile(ff, 1536, LANES)
        nfb = ff // fbc
        (act,) = rowwise(nm("swiglu"), f_swiglu, [Row(ffp, fbc), Row(ffp, fbc, nfb)], [], [], [(ff, fbc)], nfb, tmw)
        h = unflat(matmul(flat(act), p["w_ffn_out"][l], add=flat(h_mid), name=nm("mm_ffn_out")))
        s.update(hn=hn, proj=proj, qa=qa, ka=ka, va=va, gb=gb, st_a=(st_a0, st_a1), oa=oa, oan=oan, ya=ya, qkr=qkr,
                 st_b=(st_b0, st_b1), ob=ob, obn=obn, yb=yb, mg=mg, h_mid=h_mid, hn2=hn2, ffp=ffp, act=act,
                 fbc=fbc, nfb=nfb, logits=logits, alog=alog, dtb=dtb, gain_a=gain_a, cw=cw, g_mix=g_mix, g_ffn=g_ffn)
        saved.append(s)

    dh, d_final, loss_row = loss_head(h, p["norm_final"][None], tgt, tm)

    grads = {k: [None] * depth for k in ("norm_mix", "w_in", "conv_w", "gdn_a_log", "gdn_dt_bias", "gdn_norm",
                                          "ret_decay_logit", "w_up_a", "w_up_b", "w_out", "norm_ffn", "w_ffn_in",
                                          "w_ffn_out")}
    for l in reversed(range(depth)):
        s = saved[l]
        nm = lambda k: f"l{l}_{k}"
        proj = s["proj"]
        dhf = flat(dh)
        grads["w_ffn_out"][l] = matmul(flat(s["act"]), dhf, ta=True, name=nm("mmg_ffn_out"))
        dact = unflat(matmul(dhf, p["w_ffn_out"][l], tb=True, name=nm("mmb_ffn_out")))
        fbc, nfb = s["fbc"], s["nfb"]
        (dgate, dup), _ = rowwise_vjp(nm("swiglu_b"), f_swiglu, [Row(s["ffp"], fbc), Row(s["ffp"], fbc, nfb)], [], [],
                                      [dact], nfb, tmw)
        dffp = jnp.concatenate([dgate, dup], axis=-1)
        grads["w_ffn_in"][l] = matmul(flat(s["hn2"]), flat(dffp), ta=True, name=nm("mmg_ffn_in"))
        dhn2 = unflat(matmul(flat(dffp), p["w_ffn_in"][l], tb=True, name=nm("mmb_ffn_in")))
        (dh_mid,), (dg_ffn,) = rowwise_vjp(nm("rms_ffn_b"), f_rms, [Row(s["h_mid"], d)], [], [s["g_ffn"]], [dhn2], 1, tmw,
                                           adds={0: dh})
        grads["norm_ffn"][l] = dg_ffn[0]
        dmf = flat(dh_mid)
        grads["w_out"][l] = matmul(flat(s["mg"]), dmf, ta=True, name=nm("mmg_out"))
        dmg = unflat(matmul(dmf, p["w_out"][l], tb=True, name=nm("mmb_out")))
        (dga, dgbt, dya, dyb), _ = rowwise_vjp(
            nm("merge_b"), f_merge,
            [Row(proj, LANES, cb(lay.ga)), Row(proj, LANES, cb(lay.gbt)), Row(s["ya"], LANES), Row(s["yb"], LANES)],
            [], [], [dmg], cb(d), tm)
        grads["w_up_b"][l] = matmul(flat(s["obn"]), flat(dyb), ta=True, name=nm("mmg_up_b"))
        dobn = unflat(matmul(flat(dyb), p["w_up_b"][l], tb=True, name=nm("mmb_up_b")))
        (dob, dg_b), _ = rowwise_vjp(nm("ret_out_b"), f_ret_out,
                                     [Row(s["ob"], 4 * LANES), Row(proj, 4 * LANES, cb(lay.gb) // 4)], [], [], [dobn],
                                     h_ret, tmw)
        r1 = ret_bwd(s["qkr"], proj, cb(lay.vb), s["logits"][1], s["st_b"][1], dob, h_ret, 1, None, nm("ret_b1"))
        r0 = ret_bwd(s["qkr"], proj, cb(lay.vb), s["logits"][0], s["st_b"][0], dob, h_ret, 0, r1[:3], nm("ret_b0"))
        dqkr = jnp.concatenate([r0[0], r0[1]], axis=-1)
        (dqk,) = rowwise(nm("rope_b"), rope_b, [Row(dqkr, 2 * LANES)], [Tab(cos), Tab(sin)], [],
                         [(2 * d, 2 * LANES)], 2 * h_ret, tm)
        dv_b = r0[2]
        grads["ret_decay_logit"][l] = jnp.stack([r0[3][0, :h_ret], r1[3][0, :h_ret]])
        grads["w_up_a"][l] = matmul(flat(s["oan"]), flat(dya), ta=True, name=nm("mmg_up_a"))
        doan = unflat(matmul(flat(dya), p["w_up_a"][l], tb=True, name=nm("mmb_up_a")))
        (doa, dz), (dgain_a,) = rowwise_vjp(nm("gdn_out_b"), f_gdn_out,
                                            [Row(s["oa"], LANES), Row(proj, LANES, cb(lay.z))], [], [s["gain_a"]],
                                            [doan], h_gdn, tm)
        grads["gdn_norm"][l] = dgain_a[0]
        a1 = gdn_bwd(s["qa"], s["ka"], s["va"], s["gb"], s["st_a"][1], doa, h_gdn, 1, None, nm("gdn_b1"))
        a0 = gdn_bwd(s["qa"], s["ka"], s["va"], s["gb"], s["st_a"][0], doa, h_gdn, 0, a1, nm("gdn_b0"))
        (dab,), (dalog, ddtb) = rowwise_vjp(nm("gb_b"), f_gb, [Row(proj, LANES, cb(lay.ab))], [], [s["alog"], s["dtb"]],
                                            [a0[3]], 1, tm)
        grads["gdn_a_log"][l] = dalog[0, :2 * h_gdn].reshape(2, h_gdn)
        grads["gdn_dt_bias"][l] = ddtb[0, :2 * h_gdn].reshape(2, h_gdn)
        cw = s["cw"]
        dxq, dwq = conv_bwd(proj, 0, cw[:, :d], a0[0], True, nm("conv_q_b"))
        dxk, dwk = conv_bwd(proj, cb(d), cw[:, d:2 * d], a0[1], True, nm("conv_k_b"))
        dxv, dwv = conv_bwd(proj, cb(2 * d), cw[:, 2 * d:], a0[2], False, nm("conv_v_b"))
        grads["conv_w"][l] = jnp.concatenate([dwq, dwk, dwv], axis=1)
        dproj = jnp.concatenate([dxq, dxk, dxv, dz, dqk, dv_b, dg_b, dga, dgbt, dab,
                                 jnp.zeros((bsz, lp, lay.np - lay.used), F32)], axis=-1)
        grads["w_in"][l] = matmul(flat(s["hn"]), flat(dproj), ta=True, name=nm("mmg_in"))
        dhn = unflat(matmul(flat(dproj), p["w_in"][l], tb=True, name=nm("mmb_in")))
        (dh,), (dg_mix,) = rowwise_vjp(nm("rms_mix_b"), f_rms, [Row(s["h_in"], d)], [], [s["g_mix"]], [dhn], 1, tmw,
                                       adds={0: dh_mid})
        grads["norm_mix"][l] = dg_mix[0]

    out = {k: jnp.stack(v) for k, v in grads.items()}
    out["norm_final"] = d_final[0]
    grad_x = dh[:, CHUNK:]
    grad_meta = jnp.sum(dh[:, N_PAD:CHUNK], axis=0)
    return loss_row, grad_x, grad_meta, out


BIG = ("w_in", "w_up_a", "w_up_b", "w_out", "w_ffn_in", "w_ffn_out")
BIG_AXES = (2, 1, 1, 1, 2, 1)
SMALL_SHARDED = ("meta_tokens", "conv_w")
SMALL_AXES = (1, 2)
REPLICATED = ("norm_mix", "gdn_a_log", "gdn_dt_bias", "gdn_norm", "ret_decay_logit", "norm_ffn", "norm_final")
WEIGHTS = ("meta_tokens", "norm_mix", "w_in", "conv_w", "gdn_a_log", "gdn_dt_bias", "gdn_norm", "ret_decay_logit",
           "w_up_a", "w_up_b", "w_out", "norm_ffn", "w_ffn_in", "w_ffn_out", "norm_final")


def kernel(x, meta_tokens, norm_mix, w_in, conv_w, gdn_a_log, gdn_dt_bias, gdn_norm, ret_decay_logit, w_up_a, w_up_b, w_out, norm_ffn, w_ffn_in, w_ffn_out, norm_final, loss_target, m_meta_tokens, m_norm_mix, m_w_in, m_conv_w, m_gdn_a_log, m_gdn_dt_bias, m_gdn_norm, m_ret_decay_logit, m_w_up_a, m_w_up_b, m_w_out, m_norm_ffn, m_w_ffn_in, m_w_ffn_out, m_norm_final, v_meta_tokens, v_norm_mix, v_w_in, v_conv_w, v_gdn_a_log, v_gdn_dt_bias, v_gdn_norm, v_ret_decay_logit, v_w_up_a, v_w_up_b, v_w_out, v_norm_ffn, v_w_ffn_in, v_w_ffn_out, v_norm_final):
    w = dict(meta_tokens=meta_tokens, norm_mix=norm_mix, w_in=w_in, conv_w=conv_w, gdn_a_log=gdn_a_log,
             gdn_dt_bias=gdn_dt_bias, gdn_norm=gdn_norm, ret_decay_logit=ret_decay_logit, w_up_a=w_up_a,
             w_up_b=w_up_b, w_out=w_out, norm_ffn=norm_ffn, w_ffn_in=w_ffn_in, w_ffn_out=w_ffn_out,
             norm_final=norm_final)
    m = dict(meta_tokens=m_meta_tokens, norm_mix=m_norm_mix, w_in=m_w_in, conv_w=m_conv_w, gdn_a_log=m_gdn_a_log,
             gdn_dt_bias=m_gdn_dt_bias, gdn_norm=m_gdn_norm, ret_decay_logit=m_ret_decay_logit, w_up_a=m_w_up_a,
             w_up_b=m_w_up_b, w_out=m_w_out, norm_ffn=m_norm_ffn, w_ffn_in=m_w_ffn_in, w_ffn_out=m_w_ffn_out,
             norm_final=m_norm_final)
    v = dict(meta_tokens=v_meta_tokens, norm_mix=v_norm_mix, w_in=v_w_in, conv_w=v_conv_w, gdn_a_log=v_gdn_a_log,
             gdn_dt_bias=v_gdn_dt_bias, gdn_norm=v_gdn_norm, ret_decay_logit=v_ret_decay_logit, w_up_a=v_w_up_a,
             w_up_b=v_w_up_b, w_out=v_w_out, norm_ffn=v_norm_ffn, w_ffn_in=v_w_ffn_in, w_ffn_out=v_w_ffn_out,
             norm_final=v_norm_final)
    d = x.shape[-1]
    lay = Layout(d)
    sharded = BIG + SMALL_SHARDED
    sharded_axes = BIG_AXES + SMALL_AXES

    big_shapes = [w[k].shape for k in BIG]
    gathered = all_gather_hbm(_pack([w[k].astype(BF16) for k in BIG], 16), "gather_weights")
    full = dict(zip(BIG, _gathered_full(gathered, big_shapes, BIG_AXES)))
    small_shapes = [w[k].shape for k in SMALL_SHARDED]
    gathered_s = all_gather_small(_pack([w[k] for k in SMALL_SHARDED], 8), "gather_small")
    full.update(zip(SMALL_SHARDED, _gathered_full(gathered_s, small_shapes, SMALL_AXES)))
    p = {k: w[k] for k in REPLICATED}
    p.update({k: full[k] for k in BIG + ("conv_w",)})
    p["w_in"] = jnp.stack([lay.relayout_w_in(full["w_in"][l]) for l in range(full["w_in"].shape[0])])

    loss_row, grad_x, grad_meta, g = local_step(x, loss_target, full["meta_tokens"], p)
    g["meta_tokens"] = grad_meta
    g["w_in"] = jnp.stack([lay.unlayout_w_in(g["w_in"][l]) for l in range(g["w_in"].shape[0])])

    pieces = all_to_all_hbm(_pieces_by_dest([g[k] for k in sharded], sharded_axes, 1024), "exchange_grads")
    local_shapes = [w[k].shape for k in sharded]
    res = adamw(_pack([w[k] for k in sharded], 1024), _pack([m[k] for k in sharded], 1024),
                _pack([v[k] for k in sharded], 1024), pieces, "adamw_sharded")
    outs = {kind: dict(zip(sharded, _unpack(r, local_shapes))) for kind, r in zip(("g", "d", "m", "v"), res)}

    rep_shapes = [w[k].shape for k in REPLICATED]
    part = _pack([g[k] for k in REPLICATED] + [loss_row[0, :1]], 8)
    parts = all_gather_small(part, "gather_replicated")
    pad1 = lambda a: _pack([a[k] for k in REPLICATED] + [jnp.zeros((1,), F32)], 8)
    res_r = adamw(pad1(w), pad1(m), pad1(v), parts, "adamw_replicated")
    for kind, r in zip(("g", "d", "m", "v"), res_r):
        outs[kind].update(zip(REPLICATED, _unpack(r, rep_shapes)))
    loss = res_r[0].reshape(-1)[sum(math.prod(s) for s in rep_shapes)]

    return (loss, grad_x, *[outs["g"][k] for k in WEIGHTS], *[outs["d"][k] for k in WEIGHTS],
            *[outs["m"][k] for k in WEIGHTS], *[outs["v"][k] for k in WEIGHTS])
```

```python
import functools
import math

import jax
import jax.numpy as jnp
from jax import lax
from jax.experimental import pallas as pl
from jax.experimental.pallas import tpu as pltpu

F32 = jnp.float32
BF16 = jnp.bfloat16
HIGHEST = lax.Precision.HIGHEST

LANES = 128
CHUNK = 64
N_META = 16
N_PAD = CHUNK - N_META
CONV_K = 5
EPS = 1e-6
ROPE_BASE = 10000.0
N_DEV = 8
VMEM_LIMIT = 56 * 1024 * 1024

ADAM_LR, ADAM_B1, ADAM_B2, ADAM_EPS, ADAM_WD, ADAM_STEP = 0.001, 0.9, 0.999, 1e-08, 0.01, 10


def _tile(n, cap, mult):
    if n <= cap:
        return n
    best = None
    for t in range(mult, cap + 1, mult):
        if n % t == 0:
            best = t
    assert best is not None, (n, cap, mult)
    return best


def _params(sem):
    return pltpu.CompilerParams(dimension_semantics=sem, vmem_limit_bytes=VMEM_LIMIT)


def _raw_dot(a, b, ca, cb, exact):
    dims = (((ca,), (cb,)), ((), ()))
    a_hi, b_hi = a.astype(BF16), b.astype(BF16)
    out = lax.dot_general(a_hi, b_hi, dims, preferred_element_type=F32)
    if exact:
        a_lo = (a - a_hi.astype(F32)).astype(BF16)
        b_lo = (b - b_hi.astype(F32)).astype(BF16)
        out = out + lax.dot_general(a_hi, b_lo, dims, preferred_element_type=F32)
        out = out + lax.dot_general(a_lo, b_hi, dims, preferred_element_type=F32)
    return out


@functools.partial(jax.custom_vjp, nondiff_argnums=(2, 3, 4))
def _dot(a, b, ca, cb, exact):
    return _raw_dot(a, b, ca, cb, exact)


def _dot_fwd(a, b, ca, cb, exact):
    return _raw_dot(a, b, ca, cb, exact), (a, b)


def _dot_bwd(ca, cb, exact, res, g):
    a, b = res
    if ca == 1:
        da = _raw_dot(g, b, 1, 1 if cb == 0 else 0, exact)
    else:
        da = _raw_dot(b, g, 1 if cb == 0 else 0, 1, exact)
    if cb == 0:
        db = _raw_dot(a, g, 0 if ca == 1 else 1, 0, exact)
    else:
        db = _raw_dot(g, a, 0, 0 if ca == 1 else 1, exact)
    return da, db


_dot.defvjp(_dot_fwd, _dot_bwd)


@functools.partial(jax.custom_vjp, nondiff_argnums=(1, 2))
def _split(x, n, axis):
    return lax.slice_in_dim(x, 0, n, axis=axis), lax.slice_in_dim(x, n, x.shape[axis], axis=axis)


_split.defvjp(lambda x, n, axis: (_split(x, n, axis), None),
              lambda n, axis, _, g: (jnp.concatenate([g[0], g[1]], axis=axis),))


def _bdot(a, b, ca=1, cb=0):
    return _dot(a, b, ca, cb, False)


def _xdot(a, b, ca=1, cb=0):
    return _dot(a, b, ca, cb, True)


def matmul(a, b, *, ta=False, tb=False, add=None, name):
    m, k = (a.shape[1], a.shape[0]) if ta else a.shape
    k2, n = (b.shape[1], b.shape[0]) if tb else b.shape
    assert k == k2, (a.shape, b.shape, ta, tb)
    tm = _tile(m, 1040, 128 if ta else 8)
    tn = _tile(n, 512, 128)
    tk = _tile(k, 1664, 128 if (not ta or tb) else 16)
    nk = k // tk
    ca, cb = (0 if ta else 1), (1 if tb else 0)
    a_spec = pl.BlockSpec((tk, tm), lambda i, j, kk: (kk, i)) if ta else pl.BlockSpec((tm, tk), lambda i, j, kk: (i, kk))
    b_spec = pl.BlockSpec((tn, tk), lambda i, j, kk: (j, kk)) if tb else pl.BlockSpec((tk, tn), lambda i, j, kk: (kk, j))
    o_spec = pl.BlockSpec((tm, tn), lambda i, j, kk: (i, j))
    has_add = add is not None

    def body(*refs):
        a_ref, b_ref = refs[0], refs[1]
        add_ref = refs[2] if has_add else None
        o_ref = refs[3] if has_add else refs[2]
        part = _raw_dot(a_ref[...], b_ref[...], ca, cb, False)
        if nk == 1:
            o_ref[...] = part + add_ref[...] if has_add else part
            return
        acc_ref = refs[-1]
        kk = pl.program_id(2)

        @pl.when(kk == 0)
        def _():
            acc_ref[...] = part

        @pl.when(kk > 0)
        def _():
            acc_ref[...] += part

        @pl.when(kk == nk - 1)
        def _():
            o_ref[...] = acc_ref[...] + add_ref[...] if has_add else acc_ref[...]

    ins = [a, b] + ([add] if has_add else [])
    in_specs = [a_spec, b_spec] + ([o_spec] if has_add else [])
    return pl.pallas_call(
        body, name=name, grid=(m // tm, n // tn, nk), in_specs=in_specs, out_specs=o_spec,
        out_shape=jax.ShapeDtypeStruct((m, n), F32),
        scratch_shapes=[pltpu.VMEM((tm, tn), F32)] if nk > 1 else [],
        compiler_params=_params(("parallel", "parallel", "arbitrary")),
    )(*ins)


class Row:
    def __init__(self, arr, bc, off=0, per_head=True, diff=True):
        self.arr, self.bc, self.off, self.per_head, self.diff = arr, bc, off, per_head, diff


class Tab:
    def __init__(self, arr):
        self.arr = arr


def _row_specs(rows, tabs, pars, tm):
    specs = []
    for r in rows:
        specs.append(pl.BlockSpec((1, tm, r.bc), functools.partial(
            lambda b, i, h, off, ph: (b, i, off + (h if ph else 0)), off=r.off, ph=r.per_head)))
    for t in tabs:
        specs.append(pl.BlockSpec((tm, t.arr.shape[1]), lambda b, i, h: (i, 0)))
    for p in pars:
        specs.append(pl.BlockSpec(p.shape, lambda b, i, h: (0, 0)))
    return specs


def rowwise(name, fn, rows, tabs, pars, outs, nh, tm):
    bsz, lp = rows[0].arr.shape[:2]
    nr, nt, npar = len(rows), len(tabs), len(pars)

    def body(*refs):
        t0 = pl.program_id(1) * tm
        ins = [refs[k][0] for k in range(nr)] + [refs[nr + k][...] for k in range(nt + npar)]
        res = fn(t0, *ins)
        for o_ref, o in zip(refs[nr + nt + npar:], res):
            o_ref[0] = o

    return pl.pallas_call(
        body, name=name, grid=(bsz, lp // tm, nh),
        in_specs=_row_specs(rows, tabs, pars, tm),
        out_specs=[pl.BlockSpec((1, tm, bc), lambda b, i, h: (b, i, h)) for _, bc in outs],
        out_shape=[jax.ShapeDtypeStruct((bsz, lp, c), F32) for c, _ in outs],
        compiler_params=_params(("parallel", "parallel", "parallel")),
    )(*[r.arr for r in rows], *[t.arr for t in tabs], *pars)


def rowwise_vjp(name, fn, rows, tabs, pars, couts, nh, tm, adds=None):
    bsz, lp = rows[0].arr.shape[:2]
    nr, nt, npar, nco = len(rows), len(tabs), len(pars), len(couts)
    adds = adds or {}
    add_keys = sorted(adds)
    diff_idx = [k for k, r in enumerate(rows) if r.diff]
    for k in diff_idx:
        assert rows[k].per_head or nh == 1

    def body(*refs):
        b, i, h = pl.program_id(0), pl.program_id(1), pl.program_id(2)
        t0 = i * tm
        pos = 0
        row_v = [refs[k][0] for k in range(nr)]
        pos += nr
        tab_v = [refs[pos + k][...] for k in range(nt)]
        pos += nt
        par_v = [refs[pos + k][...] for k in range(npar)]
        pos += npar
        co_v = [refs[pos + k][0] for k in range(nco)]
        pos += nco
        add_v = {key: refs[pos + k][0] for k, key in enumerate(add_keys)}
        pos += len(add_keys)
        drow_refs = refs[pos:pos + len(diff_idx)]
        dpar_refs = refs[pos + len(diff_idx):]

        def f(dvals, pvals):
            full = list(row_v)
            for k, v in zip(diff_idx, dvals):
                full[k] = v
            return tuple(fn(t0, *full, *tab_v, *pvals))

        _, pull = jax.vjp(f, [row_v[k] for k in diff_idx], par_v)
        d_rows, d_pars = pull(tuple(co_v))
        for ref, k, d in zip(drow_refs, diff_idx, d_rows):
            ref[0] = d + add_v[k] if k in add_v else d
        first = jnp.logical_and(jnp.logical_and(b == 0, i == 0), h == 0)
        for ref, d in zip(dpar_refs, d_pars):
            @pl.when(first)
            def _(ref=ref, d=d):
                ref[...] = d

            @pl.when(jnp.logical_not(first))
            def _(ref=ref, d=d):
                ref[...] += d

    out_block = lambda bc: pl.BlockSpec((1, tm, bc), lambda b, i, h: (b, i, h))
    in_specs = _row_specs(rows, tabs, pars, tm)
    in_specs += [out_block(c.shape[2] // nh) for c in couts]
    in_specs += [out_block(rows[k].bc) for k in add_keys]
    out_specs = [out_block(rows[k].bc) for k in diff_idx]
    out_specs += [pl.BlockSpec(p.shape, lambda b, i, h: (0, 0)) for p in pars]
    out_shape = [jax.ShapeDtypeStruct((bsz, lp, nh * rows[k].bc), F32) for k in diff_idx]
    out_shape += [jax.ShapeDtypeStruct(p.shape, F32) for p in pars]
    res = pl.pallas_call(
        body, name=name, grid=(bsz, lp // tm, nh), in_specs=in_specs, out_specs=out_specs, out_shape=out_shape,
        compiler_params=_params(("arbitrary", "arbitrary", "arbitrary")),
    )(*[r.arr for r in rows], *[t.arr for t in tabs], *pars, *couts, *[adds[k] for k in add_keys])
    return res[:len(diff_idx)], res[len(diff_idx):]


def _real_rows(t0, tm):
    return (t0 + lax.broadcasted_iota(jnp.int32, (tm, 1), 0)) >= N_PAD


def f_rms(t0, x, gain):
    return (x * lax.rsqrt(jnp.mean(x * x, axis=-1, keepdims=True) + EPS) * gain,)


def make_f_gb(heads):
    def f_gb(t0, ab, alog, dtb):
        lane = lax.broadcasted_iota(jnp.int32, ab.shape, 1)
        g = -jnp.exp(alog) * jax.nn.softplus(ab + dtb)
        beta = jax.nn.sigmoid(ab)
        out = jnp.where(lane < 2 * heads, g, jnp.where(lane < 4 * heads, beta, 0.0))
        return (jnp.where(_real_rows(t0, ab.shape[0]), out, 0.0),)
    return f_gb


def f_gdn_out(t0, o, z, gain):
    on = o * lax.rsqrt(jnp.mean(o * o, axis=-1, keepdims=True) + EPS)
    return (on * gain * jax.nn.silu(z),)


def f_ret_out(t0, o, g):
    on = o * lax.rsqrt(jnp.mean(o * o, axis=-1, keepdims=True) + EPS)
    return (on * jax.nn.silu(g),)


def f_merge(t0, ga, gb, ya, yb):
    return (jax.nn.sigmoid(ga) * ya + jax.nn.sigmoid(gb) * yb,)


def f_swiglu(t0, gate, up):
    return (jax.nn.silu(gate) * up,)


def make_f_rope(sign):
    def f_rope(t0, x, cos, sin):
        half = x.shape[1] // 2
        x1, x2 = x[:, :half], x[:, half:]
        s = sin * sign
        return (jnp.concatenate([x1 * cos - x2 * s, x1 * s + x2 * cos], axis=1),)
    return f_rope


def loss_head(h, gain, target, tm):
    bsz, lp, d = h.shape

    def body(h_ref, g_ref, t_ref, dh_ref, dg_ref, loss_ref):
        b, i = pl.program_id(0), pl.program_id(1)
        rows = (i * tm + lax.broadcasted_iota(jnp.int32, (tm, 1), 0)) >= CHUNK
        tgt = t_ref[0]

        def f(x, gain_v):
            y = f_rms(0, x, gain_v)[0]
            err = jnp.where(rows, y - tgt, 0.0)
            return 0.5 * jnp.sum(jnp.mean(err * err, axis=-1, keepdims=True), keepdims=True)

        val, pull = jax.vjp(f, h_ref[0], g_ref[...])
        dh, dg = pull(jnp.ones((1, 1), F32))
        dh_ref[0] = dh
        first = jnp.logical_and(b == 0, i == 0)
        val_row = jnp.broadcast_to(val, (1, LANES))

        @pl.when(first)
        def _():
            dg_ref[...] = dg
            loss_ref[...] = val_row

        @pl.when(jnp.logical_not(first))
        def _():
            dg_ref[...] += dg
            loss_ref[...] += val_row

    blk = pl.BlockSpec((1, tm, d), lambda b, i: (b, i, 0))
    return pl.pallas_call(
        body, name="loss_head", grid=(bsz, lp // tm),
        in_specs=[blk, pl.BlockSpec((1, d), lambda b, i: (0, 0)), blk],
        out_specs=[blk, pl.BlockSpec((1, d), lambda b, i: (0, 0)), pl.BlockSpec((1, LANES), lambda b, i: (0, 0))],
        out_shape=[jax.ShapeDtypeStruct((bsz, lp, d), F32), jax.ShapeDtypeStruct((1, d), F32),
                   jax.ShapeDtypeStruct((1, LANES), F32)],
        compiler_params=_params(("arbitrary", "arbitrary")),
    )(h, gain, target)


def _conv_pre(x, w_ref):
    lp = x.shape[0]
    acc = w_ref[2:3, :] * x
    for k in (0, 1, 3, 4):
        acc = acc + w_ref[k:k + 1, :] * pltpu.roll(x, (2 - k) % lp, 0)
    return acc


def conv_fwd(proj, off, w, l2, name):
    bsz, lp, _ = proj.shape
    d = w.shape[1]

    def body(x_ref, w_ref, o_ref):
        x = x_ref[0]
        s = jnp.where(_real_rows(0, lp), jax.nn.silu(_conv_pre(x, w_ref)), 0.0)
        if l2:
            s = s * lax.rsqrt(jnp.sum(s * s, axis=-1, keepdims=True) + EPS)
        o_ref[0] = s

    return pl.pallas_call(
        body, name=name, grid=(d // LANES, bsz),
        in_specs=[pl.BlockSpec((1, lp, LANES), lambda j, b: (b, 0, off + j)),
                  pl.BlockSpec((CONV_K, LANES), lambda j, b: (0, j))],
        out_specs=pl.BlockSpec((1, lp, LANES), lambda j, b: (b, 0, j)),
        out_shape=jax.ShapeDtypeStruct((bsz, lp, d), F32),
        compiler_params=_params(("parallel", "parallel")),
    )(proj, w)


def conv_bwd(proj, off, w, dy, l2, name):
    bsz, lp, _ = proj.shape
    d = w.shape[1]

    def body(x_ref, w_ref, dy_ref, dx_ref, dw_ref):
        b = pl.program_id(1)
        x, g = x_ref[0], dy_ref[0]
        real = _real_rows(0, lp)
        c = _conv_pre(x, w_ref)
        sg = jax.nn.sigmoid(c)
        s = jnp.where(real, c * sg, 0.0)
        if l2:
            r = lax.rsqrt(jnp.sum(s * s, axis=-1, keepdims=True) + EPS)
            g = r * g - s * (r * r * r) * jnp.sum(g * s, axis=-1, keepdims=True)
        dc = jnp.where(real, g * (sg * (1.0 + c * (1.0 - sg))), 0.0)
        dx = w_ref[2:3, :] * dc
        for k in (0, 1, 3, 4):
            dx = dx + w_ref[k:k + 1, :] * pltpu.roll(dc, (k - 2) % lp, 0)
        dx_ref[0] = jnp.where(real, dx, 0.0)
        tap_row = lax.broadcasted_iota(jnp.int32, (CONV_K, LANES), 0)
        dw = jnp.zeros((CONV_K, LANES), F32)
        for k in range(CONV_K):
            xs = x if k == 2 else pltpu.roll(x, (2 - k) % lp, 0)
            dw = dw + jnp.where(tap_row == k, jnp.sum(dc * xs, axis=0, keepdims=True), 0.0)

        @pl.when(b == 0)
        def _():
            dw_ref[...] = dw

        @pl.when(b > 0)
        def _():
            dw_ref[...] += dw

    blk = pl.BlockSpec((1, lp, LANES), lambda j, b: (b, 0, j))
    return pl.pallas_call(
        body, name=name, grid=(d // LANES, bsz),
        in_specs=[pl.BlockSpec((1, lp, LANES), lambda j, b: (b, 0, off + j)),
                  pl.BlockSpec((CONV_K, LANES), lambda j, b: (0, j)), blk],
        out_specs=[blk, pl.BlockSpec((CONV_K, LANES), lambda j, b: (0, j))],
        out_shape=[jax.ShapeDtypeStruct((bsz, lp, d), F32), jax.ShapeDtypeStruct((CONV_K, d), F32)],
        compiler_params=_params(("parallel", "arbitrary")),
    )(proj, w, dy)


def _tri_masks(rev):
    ii = lax.broadcasted_iota(jnp.int32, (CHUNK, CHUNK), 0)
    jj = lax.broadcasted_iota(jnp.int32, (CHUNK, CHUNK), 1)
    return ((ii <= jj), (ii < jj)) if rev else ((ii >= jj), (ii > jj))


def _lane_pick(block, lane):
    sel = lax.broadcasted_iota(jnp.int32, block.shape, 1) == lane
    return jnp.sum(jnp.where(sel, block, 0.0), axis=1, keepdims=True)


def _cumsum_impl(x, rev):
    n = x.shape[0]
    row = lax.broadcasted_iota(jnp.int32, x.shape, 0)
    step = 1
    while step < n:
        if rev:
            x = x + jnp.where(row < n - step, pltpu.roll(x, n - step, 0), 0.0)
        else:
            x = x + jnp.where(row >= step, pltpu.roll(x, step, 0), 0.0)
        step *= 2
    return x


@functools.partial(jax.custom_vjp, nondiff_argnums=(1,))
def _cumsum_rows(x, rev):
    return _cumsum_impl(x, rev)


_cumsum_rows.defvjp(lambda x, rev: (_cumsum_impl(x, rev), None),
                    lambda rev, _, g: (_cumsum_impl(g, not rev),))


def _unit_inv_impl(m):
    n = m.shape[0]
    eye = (lax.broadcasted_iota(jnp.int32, (n, n), 0) == lax.broadcasted_iota(jnp.int32, (n, n), 1)).astype(F32)
    p = -m
    inv = eye + p
    step = 2
    while step < n:
        p = _raw_dot(p, p, 1, 0, True)
        inv = inv + _raw_dot(inv, p, 1, 0, True)
        step *= 2
    return inv


@jax.custom_vjp
def _unit_inv(m):
    return _unit_inv_impl(m)


def _unit_inv_fwd(m):
    inv = _unit_inv_impl(m)
    return inv, inv


def _unit_inv_bwd(inv, g):
    return (-_raw_dot(_raw_dot(inv, g, 0, 0, True), inv, 1, 1, True),)


_unit_inv.defvjp(_unit_inv_fwd, _unit_inv_bwd)


def _gdn_chunk(qs, ks, vs, gb, ss, g_lanes, b_lanes, rev):
    nh = len(qs)
    dk = qs[0].shape[1]
    incl, strict = _tri_masks(rev)
    hs = range(nh)
    g = [_lane_pick(gb, l) for l in g_lanes]
    beta = [_lane_pick(gb, l) for l in b_lanes]
    qs = [q * (dk ** -0.5) for q in qs]
    gc_sq = [_cumsum_rows(jnp.broadcast_to(g[h], (CHUNK, CHUNK)), rev) for h in hs]
    gc = [_cumsum_rows(jnp.broadcast_to(g[h], (CHUNK, dk)), rev) for h in hs]
    g_last = [jnp.sum(g[h], axis=0, keepdims=True) for h in hs]
    decay = [jnp.where(incl, jnp.exp(jnp.where(incl, gc_sq[h] - gc_sq[h].T, 0.0)), 0.0) for h in hs]
    kb = [ks[h] * beta[h] for h in hs]
    kk = [_split(_bdot(jnp.concatenate([kb[h], qs[h]], axis=0), ks[h], 1, 1), CHUNK, 0) for h in hs]
    m = [jnp.where(strict, kk[h][0] * decay[h], 0.0) for h in hs]
    qk = [kk[h][1] * decay[h] for h in hs]
    inv = [_unit_inv(m[h]) for h in hs]
    e_gc = [jnp.exp(gc[h]) for h in hs]
    uw = [_split(_bdot(inv[h], jnp.concatenate([vs[h] * beta[h], kb[h] * e_gc[h]], axis=1)), vs[h].shape[1], 1)
          for h in hs]
    u = [uw[h][0] for h in hs]
    w = [uw[h][1] for h in hs]
    q_dec = [qs[h] * e_gc[h] for h in hs]
    k_dec = [ks[h] * jnp.exp(g_last[h] - gc[h]) for h in hs]
    ws = [_split(_bdot(jnp.concatenate([w[h], q_dec[h]], axis=0), ss[h]), CHUNK, 0) for h in hs]
    v_new = [u[h] - ws[h][0] for h in hs]
    o = [ws[h][1] + _bdot(qk[h], v_new[h]) for h in hs]
    s_new = [ss[h] * jnp.exp(g_last[h]) + _bdot(k_dec[h], v_new[h], 0, 0) for h in hs]
    return o, s_new


def gdn_fwd(q, k, v, gb, heads, direction, o_prev, name):
    bsz, lp, d = q.shape
    nc = lp // CHUNK
    rev = direction == 1
    cm = (lambda n: nc - 1 - n) if rev else (lambda n: n)
    has_prev = o_prev is not None

    def body(*refs):
        q_ref, k_ref, v_ref, gb_ref = refs[:4]
        prev_ref = refs[4] if has_prev else None
        o_ref, st_ref, s_ref = refs[-3], refs[-2], refs[-1]
        n = pl.program_id(1)

        @pl.when(n == 0)
        def _():
            s_ref[...] = jnp.zeros_like(s_ref)

        sls = [slice(h * LANES, (h + 1) * LANES) for h in range(heads)]
        ss = [s_ref[h] for h in range(heads)]
        for h in range(heads):
            st_ref[0, 0, h] = ss[h]
        os_, s_new = _gdn_chunk([q_ref[0, :, sl] for sl in sls], [k_ref[0, :, sl] for sl in sls],
                                [v_ref[0, :, sl] for sl in sls], gb_ref[0], ss,
                                [direction * heads + h for h in range(heads)],
                                [2 * heads + direction * heads + h for h in range(heads)], rev)
        for h, sl in enumerate(sls):
            s_ref[h] = s_new[h]
            o_ref[0, :, sl] = os_[h] + prev_ref[0, :, sl] if has_prev else os_[h]

    blk = pl.BlockSpec((1, CHUNK, d), lambda b, n: (b, cm(n), 0))
    gblk = pl.BlockSpec((1, CHUNK, LANES), lambda b, n: (b, cm(n), 0))
    st_blk = pl.BlockSpec((1, 1, heads, LANES, LANES), lambda b, n: (b, cm(n), 0, 0, 0))
    return pl.pallas_call(
        body, name=name, grid=(bsz, nc),
        in_specs=[blk, blk, blk, gblk] + ([blk] if has_prev else []),
        out_specs=[blk, st_blk],
        out_shape=[jax.ShapeDtypeStruct((bsz, lp, d), F32),
                   jax.ShapeDtypeStruct((bsz, nc, heads, LANES, LANES), F32)],
        scratch_shapes=[pltpu.VMEM((heads, LANES, LANES), F32)],
        compiler_params=_params(("parallel", "arbitrary")),
    )(q, k, v, gb, *([o_prev] if has_prev else []))


def gdn_bwd(q, k, v, gb, states, do, heads, direction, prev, name):
    bsz, lp, d = q.shape
    nc = lp // CHUNK
    rev = direction == 1
    cm = (lambda n: n) if rev else (lambda n: nc - 1 - n)
    has_prev = prev is not None

    def body(*refs):
        q_ref, k_ref, v_ref, gb_ref, st_ref, do_ref = refs[:6]
        prev_refs = refs[6:10] if has_prev else None
        dq_ref, dk_ref, dv_ref, dgb_ref, ds_ref = refs[-5:]
        n = pl.program_id(1)

        @pl.when(n == 0)
        def _():
            ds_ref[...] = jnp.zeros_like(ds_ref)

        sls = [slice(h * LANES, (h + 1) * LANES) for h in range(heads)]
        f = functools.partial(_gdn_chunk, g_lanes=[direction * heads + h for h in range(heads)],
                              b_lanes=[2 * heads + direction * heads + h for h in range(heads)], rev=rev)
        _, pull = jax.vjp(f, [q_ref[0, :, sl] for sl in sls], [k_ref[0, :, sl] for sl in sls],
                          [v_ref[0, :, sl] for sl in sls], gb_ref[0], [st_ref[0, 0, h] for h in range(heads)])
        dq, dk, dv, dgb, ds = pull(([do_ref[0, :, sl] for sl in sls], [ds_ref[h] for h in range(heads)]))
        for h, sl in enumerate(sls):
            ds_ref[h] = ds[h]
            if has_prev:
                dq[h], dk[h], dv[h] = (dq[h] + prev_refs[0][0, :, sl], dk[h] + prev_refs[1][0, :, sl],
                                       dv[h] + prev_refs[2][0, :, sl])
            dq_ref[0, :, sl] = dq[h]
            dk_ref[0, :, sl] = dk[h]
            dv_ref[0, :, sl] = dv[h]
        dgb_ref[0] = dgb + prev_refs[3][0] if has_prev else dgb

    blk = pl.BlockSpec((1, CHUNK, d), lambda b, n: (b, cm(n), 0))
    gblk = pl.BlockSpec((1, CHUNK, LANES), lambda b, n: (b, cm(n), 0))
    st_blk = pl.BlockSpec((1, 1, heads, LANES, LANES), lambda b, n: (b, cm(n), 0, 0, 0))
    big = jax.ShapeDtypeStruct((bsz, lp, d), F32)
    return pl.pallas_call(
        body, name=name, grid=(bsz, nc),
        in_specs=[blk, blk, blk, gblk, st_blk, blk] + ([blk, blk, blk, gblk] if has_prev else []),
        out_specs=[blk, blk, blk, gblk],
        out_shape=[big, big, big, jax.ShapeDtypeStruct((bsz, lp, LANES), F32)],
        scratch_shapes=[pltpu.VMEM((heads, LANES, LANES), F32)],
        compiler_params=_params(("parallel", "arbitrary")),
    )(q, k, v, gb, states, do, *(list(prev) if has_prev else []))


def _ret_chunk(q, k, v, r, logit, lane, rev):
    dk = q.shape[1]
    lg = jax.nn.log_sigmoid(_lane_pick(logit, lane))
    k = k * (dk ** -0.5)
    ii = lax.broadcasted_iota(jnp.int32, (CHUNK, CHUNK), 0)
    jj = lax.broadcasted_iota(jnp.int32, (CHUNK, CHUNK), 1)
    pos = lax.broadcasted_iota(jnp.int32, (CHUNK, 1), 0)
    if rev:
        incl, rel = ii <= jj, (jj - ii)
        seen = (CHUNK - 1 - pos)
    else:
        incl, rel = ii >= jj, (ii - jj)
        seen = pos
    relf = jnp.where(incl, rel, 0).astype(F32)
    seenf = seen.astype(F32)
    intra = jnp.where(incl, jnp.exp(relf * lg), 0.0)
    qk = _bdot(q, k, 1, 1) * intra
    q_dec = q * jnp.exp(lg * (seenf + 1.0))
    k_dec = k * jnp.exp(lg * (CHUNK - 1.0 - seenf))
    o = _bdot(q_dec, r) + _bdot(qk, v)
    r_new = r * jnp.exp(lg * CHUNK) + _bdot(k_dec, v, 0, 0)
    return o, r_new


def ret_fwd(qk, v_arr, v_off, logit, heads, direction, o_prev, name):
    bsz, lp, d2 = qk.shape
    d = d2 // 2
    dkh, dvh = d // heads, 2 * d // heads
    nc = lp // CHUNK
    rev = direction == 1
    cm = (lambda n: nc - 1 - n) if rev else (lambda n: n)
    has_prev = o_prev is not None
    v_cb = v_off * LANES // (2 * d)
    assert v_cb * 2 * d == v_off * LANES

    def body(*refs):
        q_ref, k_ref, v_ref, lg_ref = refs[:4]
        prev_ref = refs[4] if has_prev else None
        o_ref, st_ref, r_ref = refs[-3], refs[-2], refs[-1]
        n = pl.program_id(1)

        @pl.when(n == 0)
        def _():
            r_ref[...] = jnp.zeros_like(r_ref)

        lgv = lg_ref[...]
        for h in range(heads):
            ks, vs = slice(h * dkh, (h + 1) * dkh), slice(h * dvh, (h + 1) * dvh)
            r = r_ref[h]
            st_ref[0, 0, h] = r
            o, r_new = _ret_chunk(q_ref[0, :, ks], k_ref[0, :, ks], v_ref[0, :, vs], r, lgv, h, rev)
            r_ref[h] = r_new
            o_ref[0, :, vs] = o + prev_ref[0, :, vs] if has_prev else o

    qblk = pl.BlockSpec((1, CHUNK, d), lambda b, n: (b, cm(n), 0))
    kblk = pl.BlockSpec((1, CHUNK, d), lambda b, n: (b, cm(n), 1))
    vblk = pl.BlockSpec((1, CHUNK, 2 * d), lambda b, n: (b, cm(n), v_cb))
    oblk = pl.BlockSpec((1, CHUNK, 2 * d), lambda b, n: (b, cm(n), 0))
    st_blk = pl.BlockSpec((1, 1, heads, dkh, dvh), lambda b, n: (b, cm(n), 0, 0, 0))
    return pl.pallas_call(
        body, name=name, grid=(bsz, nc),
        in_specs=[qblk, kblk, vblk, pl.BlockSpec((1, LANES), lambda b, n: (0, 0))] + ([oblk] if has_prev else []),
        out_specs=[oblk, st_blk],
        out_shape=[jax.ShapeDtypeStruct((bsz, lp, 2 * d), F32),
                   jax.ShapeDtypeStruct((bsz, nc, heads, dkh, dvh), F32)],
        scratch_shapes=[pltpu.VMEM((heads, dkh, dvh), F32)],
        compiler_params=_params(("parallel", "arbitrary")),
    )(qk, qk, v_arr, logit, *([o_prev] if has_prev else []))


def ret_bwd(qk, v_arr, v_off, logit, states, do, heads, direction, prev, name):
    bsz, lp, d2 = qk.shape
    d = d2 // 2
    dkh, dvh = d // heads, 2 * d // heads
    nc = lp // CHUNK
    rev = direction == 1
    cm = (lambda n: n) if rev else (lambda n: nc - 1 - n)
    has_prev = prev is not None
    v_cb = v_off * LANES // (2 * d)

    def body(*refs):
        q_ref, k_ref, v_ref, lg_ref, st_ref, do_ref = refs[:6]
        prev_refs = refs[6:9] if has_prev else None
        dq_ref, dk_ref, dv_ref, dlg_ref, dr_ref = refs[-5:]
        b, n = pl.program_id(0), pl.program_id(1)

        @pl.when(n == 0)
        def _():
            dr_ref[...] = jnp.zeros_like(dr_ref)

        lgv = lg_ref[...]
        dlg = jnp.zeros((1, LANES), F32)
        for h in range(heads):
            ks, vs = slice(h * dkh, (h + 1) * dkh), slice(h * dvh, (h + 1) * dvh)
            f = functools.partial(_ret_chunk, lane=h, rev=rev)
            _, pull = jax.vjp(f, q_ref[0, :, ks], k_ref[0, :, ks], v_ref[0, :, vs], st_ref[0, 0, h], lgv)
            dq, dk, dv, dr, dlg_h = pull((do_ref[0, :, vs], dr_ref[h]))
            dr_ref[h] = dr
            dlg = dlg + dlg_h
            if has_prev:
                dq, dk, dv = dq + prev_refs[0][0, :, ks], dk + prev_refs[1][0, :, ks], dv + prev_refs[2][0, :, vs]
            dq_ref[0, :, ks] = dq
            dk_ref[0, :, ks] = dk
            dv_ref[0, :, vs] = dv
        first = jnp.logical_and(b == 0, n == 0)

        @pl.when(first)
        def _():
            dlg_ref[...] = dlg

        @pl.when(jnp.logical_not(first))
        def _():
            dlg_ref[...] += dlg

    qblk = pl.BlockSpec((1, CHUNK, d), lambda b, n: (b, cm(n), 0))
    kblk = pl.BlockSpec((1, CHUNK, d), lambda b, n: (b, cm(n), 1))
    vblk = pl.BlockSpec((1, CHUNK, 2 * d), lambda b, n: (b, cm(n), v_cb))
    oblk = pl.BlockSpec((1, CHUNK, 2 * d), lambda b, n: (b, cm(n), 0))
    lblk = pl.BlockSpec((1, LANES), lambda b, n: (0, 0))
    st_blk = pl.BlockSpec((1, 1, heads, dkh, dvh), lambda b, n: (b, cm(n), 0, 0, 0))
    return pl.pallas_call(
        body, name=name, grid=(bsz, nc),
        in_specs=[qblk, kblk, vblk, lblk, st_blk, oblk] + ([qblk, qblk, oblk] if has_prev else []),
        out_specs=[qblk, qblk, oblk, lblk],
        out_shape=[jax.ShapeDtypeStruct((bsz, lp, d), F32), jax.ShapeDtypeStruct((bsz, lp, d), F32),
                   jax.ShapeDtypeStruct((bsz, lp, 2 * d), F32), jax.ShapeDtypeStruct((1, LANES), F32)],
        scratch_shapes=[pltpu.VMEM((heads, dkh, dvh), F32)],
        compiler_params=_params(("arbitrary", "arbitrary")),
    )(qk, qk, v_arr, logit, states, do, *(list(prev) if has_prev else []))


def _flip(v, bit):
    return 1 - v if bit else v


def _peer(x, y, c, off):
    return (_flip(x, off & 4), _flip(y, off & 2), _flip(c, off & 1))


def all_gather_hbm(block, name):
    r, lanes = block.shape

    def body(x_ref, out_ref, send_sems, recv_sems, local_sem):
        x, y, c = lax.axis_index("x"), lax.axis_index("y"), lax.axis_index("c")
        me, sibling = (x, y, c), (x, y, 1 - c)
        chips = [(1 - x, y), (x, 1 - y), (1 - x, 1 - y)]

        def slot(px, py, pc):
            return out_ref.at[4 * px + 2 * py + pc]

        def copy(k, block_of, to, src=None):
            return pltpu.make_async_remote_copy(
                src_ref=slot(*block_of) if src is None else src, dst_ref=slot(*block_of),
                send_sem=send_sems.at[k], recv_sem=recv_sems.at[k], device_id=to, device_id_type=pl.DeviceIdType.MESH)

        mine = pltpu.make_async_copy(x_ref, slot(*me), local_sem)
        mine.start()
        first = [copy(0, me, sibling, src=x_ref)]
        first += [copy(1 + j, me, (*chip, c), src=x_ref) for j, chip in enumerate(chips)]
        for cp in first:
            cp.start()
        passed = [copy(4 + j, (*chip, c), sibling) for j, chip in enumerate(chips)]
        for j, chip in enumerate(chips):
            copy(1 + j, (*chip, c), me).wait_recv()
            passed[j].start()
        copy(0, sibling, me).wait_recv()
        for j, chip in enumerate(chips):
            copy(4 + j, (*chip, 1 - c), me).wait_recv()
        for cp in first + passed:
            cp.wait_send()
        mine.wait()

    return pl.pallas_call(
        body, name=name, out_shape=jax.ShapeDtypeStruct((N_DEV, r, lanes), block.dtype),
        in_specs=[pl.BlockSpec(memory_space=pl.ANY)], out_specs=pl.BlockSpec(memory_space=pl.ANY),
        scratch_shapes=[pltpu.SemaphoreType.DMA((7,)), pltpu.SemaphoreType.DMA((7,)), pltpu.SemaphoreType.DMA],
    )(block)


def all_gather_small(block, name):
    r, lanes = block.shape

    def body(x_ref, out_ref, send_sems, recv_sems):
        x, y, c = lax.axis_index("x"), lax.axis_index("y"), lax.axis_index("c")
        me = 4 * x + 2 * y + c
        out_ref[me] = x_ref[...]
        copies = []
        for off in range(1, N_DEV):
            copies.append(pltpu.make_async_remote_copy(
                src_ref=x_ref, dst_ref=out_ref.at[me], send_sem=send_sems.at[off - 1], recv_sem=recv_sems.at[off - 1],
                device_id=_peer(x, y, c, off), device_id_type=pl.DeviceIdType.MESH))
        for cp in copies:
            cp.start()
        for cp in copies:
            cp.wait()

    return pl.pallas_call(
        body, name=name, out_shape=jax.ShapeDtypeStruct((N_DEV, r, lanes), block.dtype),
        in_specs=[pl.BlockSpec(memory_space=pltpu.VMEM)], out_specs=pl.BlockSpec(memory_space=pltpu.VMEM),
        scratch_shapes=[pltpu.SemaphoreType.DMA((7,)), pltpu.SemaphoreType.DMA((7,))],
    )(block)


def all_to_all_hbm(pieces, name):
    def body(x_ref, out_ref, send_sems, recv_sems, local_sem):
        x, y, c = lax.axis_index("x"), lax.axis_index("y"), lax.axis_index("c")
        me = 4 * x + 2 * y + c
        mine = pltpu.make_async_copy(x_ref.at[me], out_ref.at[me], local_sem)
        mine.start()
        copies = []
        for off in range(1, N_DEV):
            px, py, pc = _peer(x, y, c, off)
            copies.append(pltpu.make_async_remote_copy(
                src_ref=x_ref.at[4 * px + 2 * py + pc], dst_ref=out_ref.at[me],
                send_sem=send_sems.at[off - 1], recv_sem=recv_sems.at[off - 1],
                device_id=(px, py, pc), device_id_type=pl.DeviceIdType.MESH))
        for cp in copies:
            cp.start()
        for cp in copies:
            cp.wait()
        mine.wait()

    return pl.pallas_call(
        body, name=name, out_shape=jax.ShapeDtypeStruct(pieces.shape, pieces.dtype),
        in_specs=[pl.BlockSpec(memory_space=pl.ANY)], out_specs=pl.BlockSpec(memory_space=pl.ANY),
        scratch_shapes=[pltpu.SemaphoreType.DMA((7,)), pltpu.SemaphoreType.DMA((7,)), pltpu.SemaphoreType.DMA],
    )(pieces)


def adamw(w, m, v, g8, name):
    r = w.shape[0]
    tm = _tile(r, 1024, 8)

    def body(w_ref, m_ref, v_ref, g_ref, g_out, d_out, m_out, v_out):
        g = g_ref[0].astype(F32)
        for s in range(1, N_DEV):
            g = g + g_ref[s].astype(F32)
        mn = ADAM_B1 * m_ref[...] + (1.0 - ADAM_B1) * g
        vn = ADAM_B2 * v_ref[...] + (1.0 - ADAM_B2) * (g * g)
        m_hat = mn / (1.0 - ADAM_B1 ** ADAM_STEP)
        v_hat = vn / (1.0 - ADAM_B2 ** ADAM_STEP)
        g_out[...] = g
        d_out[...] = -ADAM_LR * (m_hat / (jnp.sqrt(v_hat) + ADAM_EPS) + ADAM_WD * w_ref[...])
        m_out[...] = mn
        v_out[...] = vn

    blk = pl.BlockSpec((tm, LANES), lambda i: (i, 0))
    shp = jax.ShapeDtypeStruct((r, LANES), F32)
    return pl.pallas_call(
        body, name=name, grid=(r // tm,),
        in_specs=[blk, blk, blk, pl.BlockSpec((N_DEV, tm, LANES), lambda i: (0, i, 0))],
        out_specs=[blk, blk, blk, blk], out_shape=[shp, shp, shp, shp],
        compiler_params=_params(("parallel",)),
    )(w, m, v, g8)


def _pack(blocks, rows_mult):
    flat = jnp.concatenate([b.reshape(-1) for b in blocks])
    unit = rows_mult * LANES
    total = -(-flat.shape[0] // unit) * unit
    return jnp.pad(flat, (0, total - flat.shape[0])).reshape(-1, LANES)


def _unpack(packed, shapes):
    flat = packed.reshape(-1)
    out, pos = [], 0
    for s in shapes:
        n = math.prod(s)
        out.append(flat[pos:pos + n].reshape(s))
        pos += n
    return out


def _gathered_full(gathered, shapes, axes):
    per_dev = [_unpack(gathered[d], shapes) for d in range(N_DEV)]
    return [jnp.concatenate([per_dev[d][k] for d in range(N_DEV)], axis=axes[k]) for k in range(len(shapes))]


def _pieces_by_dest(fulls, axes, rows_mult):
    packs = []
    for d in range(N_DEV):
        blocks = []
        for f, ax in zip(fulls, axes):
            n = f.shape[ax] // N_DEV
            blocks.append(lax.slice_in_dim(f, d * n, (d + 1) * n, axis=ax))
        packs.append(_pack(blocks, rows_mult))
    return jnp.stack(packs)


class Layout:
    def __init__(self, d):
        self.d = d
        self.h = d // 128
        self.hr = d // 256
        self.z = 3 * d
        self.qb = 4 * d
        self.vb = 6 * d
        self.gb = 8 * d
        self.ga = 10 * d
        self.gbt = 11 * d
        self.ab = 12 * d
        self.used = 12 * d + LANES
        self.np = -(-self.used // 512) * 512

    def relayout_w_in(self, w):
        d, h4 = self.d, 4 * self.h
        return jnp.concatenate([w[:, :4 * d], w[:, 4 * d + h4:], w[:, 4 * d:4 * d + h4],
                                jnp.zeros((d, self.np - 12 * d - h4), w.dtype)], axis=1)

    def unlayout_w_in(self, w):
        d, h4 = self.d, 4 * self.h
        return jnp.concatenate([w[:, :4 * d], w[:, 12 * d:12 * d + h4], w[:, 4 * d:12 * d]], axis=1)


def _lane_row(vec):
    return jnp.pad(vec.reshape(-1), (0, LANES - vec.size)).reshape(1, LANES)


def _rope_tables(lp, half):
    inv = ROPE_BASE ** (-jnp.arange(half, dtype=F32) / half)
    pos = (jnp.arange(lp) - N_PAD).astype(F32)
    ang = pos[:, None] * inv[None, :]
    return jnp.cos(ang), jnp.sin(ang)


def local_step(x, target, meta, p):
    bsz, seq, d = x.shape
    lay = Layout(d)
    h_gdn, h_ret = lay.h, lay.hr
    lp = seq + CHUNK
    t_all = bsz * lp
    depth = p["w_up_a"].shape[0]
    ff = p["w_ffn_out"].shape[1]
    tm = _tile(lp, 512, 8)
    tmw = _tile(lp, 256, 8)
    cb = lambda cols: cols // LANES
    flat = lambda a: a.reshape(t_all, a.shape[-1])
    unflat = lambda a: a.reshape(bsz, lp, a.shape[-1])
    cos, sin = _rope_tables(lp, LANES)
    f_gb = make_f_gb(h_gdn)
    rope_f, rope_b = make_f_rope(1.0), make_f_rope(-1.0)

    head = jnp.concatenate([jnp.zeros((N_PAD, d), F32), meta], axis=0)
    h = jnp.concatenate([jnp.broadcast_to(head[None], (bsz, CHUNK, d)), x], axis=1)
    tgt = jnp.pad(target, ((0, 0), (CHUNK, 0), (0, 0)))

    saved = []
    for l in range(depth):
        s = {"h_in": h}
        nm = lambda k: f"l{l}_{k}"
        g_mix, g_ffn = p["norm_mix"][l][None], p["norm_ffn"][l][None]
        alog, dtb = _lane_row(p["gdn_a_log"][l]), _lane_row(p["gdn_dt_bias"][l])
        gain_a = p["gdn_norm"][l][None]
        logits = [_lane_row(p["ret_decay_logit"][l][0]), _lane_row(p["ret_decay_logit"][l][1])]
        cw = p["conv_w"][l]
        (hn,) = rowwise(nm("rms_mix"), f_rms, [Row(h, d)], [], [g_mix], [(d, d)], 1, tm)
        proj = unflat(matmul(flat(hn), p["w_in"][l], name=nm("mm_in")))
        qa = conv_fwd(proj, 0, cw[:, :d], True, nm("conv_q"))
        ka = conv_fwd(proj, cb(d), cw[:, d:2 * d], True, nm("conv_k"))
        va = conv_fwd(proj, cb(2 * d), cw[:, 2 * d:], False, nm("conv_v"))
        (gb,) = rowwise(nm("gb"), f_gb, [Row(proj, LANES, cb(lay.ab))], [], [alog, dtb], [(LANES, LANES)], 1, tm)
        o0, st_a0 = gdn_fwd(qa, ka, va, gb, h_gdn, 0, None, nm("gdn_f0"))
        oa, st_a1 = gdn_fwd(qa, ka, va, gb, h_gdn, 1, o0, nm("gdn_f1"))
        (oan,) = rowwise(nm("gdn_out"), f_gdn_out, [Row(oa, LANES), Row(proj, LANES, cb(lay.z))], [], [gain_a],
                         [(d, LANES)], h_gdn, tm)
        ya = unflat(matmul(flat(oan), p["w_up_a"][l], name=nm("mm_up_a")))
        (qkr,) = rowwise(nm("rope"), rope_f, [Row(proj, 2 * LANES, cb(lay.qb) // 2)], [Tab(cos), Tab(sin)], [],
                         [(2 * d, 2 * LANES)], 2 * h_ret, tm)
        r0, st_b0 = ret_fwd(qkr, proj, cb(lay.vb), logits[0], h_ret, 0, None, nm("ret_f0"))
        ob, st_b1 = ret_fwd(qkr, proj, cb(lay.vb), logits[1], h_ret, 1, r0, nm("ret_f1"))
        (obn,) = rowwise(nm("ret_out"), f_ret_out, [Row(ob, 4 * LANES), Row(proj, 4 * LANES, cb(lay.gb) // 4)], [], [],
                         [(2 * d, 4 * LANES)], h_ret, tmw)
        yb = unflat(matmul(flat(obn), p["w_up_b"][l], name=nm("mm_up_b")))
        (mg,) = rowwise(nm("merge"), f_merge,
                        [Row(proj, LANES, cb(lay.ga)), Row(proj, LANES, cb(lay.gbt)), Row(ya, LANES), Row(yb, LANES)],
                        [], [], [(d, LANES)], cb(d), tm)
        h_mid = unflat(matmul(flat(mg), p["w_out"][l], add=flat(h), name=nm("mm_out")))
        (hn2,) = rowwise(nm("rms_ffn"), f_rms, [Row(h_mid, d)], [], [g_ffn], [(d, d)], 1, tm)
        ffp = unflat(matmul(flat(hn2), p["w_ffn_in"][l], name=nm("mm_ffn_in")))
        fbc = _tile(ff, 1536, LANES)
        nfb = ff // fbc
        (act,) = rowwise(nm("swiglu"), f_swiglu, [Row(ffp, fbc), Row(ffp, fbc, nfb)], [], [], [(ff, fbc)], nfb, tmw)
        h = unflat(matmul(flat(act), p["w_ffn_out"][l], add=flat(h_mid), name=nm("mm_ffn_out")))
        s.update(hn=hn, proj=proj, qa=qa, ka=ka, va=va, gb=gb, st_a=(st_a0, st_a1), oa=oa, oan=oan, ya=ya, qkr=qkr,
                 st_b=(st_b0, st_b1), ob=ob, obn=obn, yb=yb, mg=mg, h_mid=h_mid, hn2=hn2, ffp=ffp, act=act,
                 fbc=fbc, nfb=nfb, logits=logits, alog=alog, dtb=dtb, gain_a=gain_a, cw=cw, g_mix=g_mix, g_ffn=g_ffn)
        saved.append(s)

    dh, d_final, loss_row = loss_head(h, p["norm_final"][None], tgt, tm)

    grads = {k: [None] * depth for k in ("norm_mix", "w_in", "conv_w", "gdn_a_log", "gdn_dt_bias", "gdn_norm",
                                          "ret_decay_logit", "w_up_a", "w_up_b", "w_out", "norm_ffn", "w_ffn_in",
                                          "w_ffn_out")}
    for l in reversed(range(depth)):
        s = saved[l]
        nm = lambda k: f"l{l}_{k}"
        proj = s["proj"]
        dhf = flat(dh)
        grads["w_ffn_out"][l] = matmul(flat(s["act"]), dhf, ta=True, name=nm("mmg_ffn_out"))
        dact = unflat(matmul(dhf, p["w_ffn_out"][l], tb=True, name=nm("mmb_ffn_out")))
        fbc, nfb = s["fbc"], s["nfb"]
        (dgate, dup), _ = rowwise_vjp(nm("swiglu_b"), f_swiglu, [Row(s["ffp"], fbc), Row(s["ffp"], fbc, nfb)], [], [],
                                      [dact], nfb, tmw)
        dffp = jnp.concatenate([dgate, dup], axis=-1)
        grads["w_ffn_in"][l] = matmul(flat(s["hn2"]), flat(dffp), ta=True, name=nm("mmg_ffn_in"))
        dhn2 = unflat(matmul(flat(dffp), p["w_ffn_in"][l], tb=True, name=nm("mmb_ffn_in")))
        (dh_mid,), (dg_ffn,) = rowwise_vjp(nm("rms_ffn_b"), f_rms, [Row(s["h_mid"], d)], [], [s["g_ffn"]], [dhn2], 1, tmw,
                                           adds={0: dh})
        grads["norm_ffn"][l] = dg_ffn[0]
        dmf = flat(dh_mid)
        grads["w_out"][l] = matmul(flat(s["mg"]), dmf, ta=True, name=nm("mmg_out"))
        dmg = unflat(matmul(dmf, p["w_out"][l], tb=True, name=nm("mmb_out")))
        (dga, dgbt, dya, dyb), _ = rowwise_vjp(
            nm("merge_b"), f_merge,
            [Row(proj, LANES, cb(lay.ga)), Row(proj, LANES, cb(lay.gbt)), Row(s["ya"], LANES), Row(s["yb"], LANES)],
            [], [], [dmg], cb(d), tm)
        grads["w_up_b"][l] = matmul(flat(s["obn"]), flat(dyb), ta=True, name=nm("mmg_up_b"))
        dobn = unflat(matmul(flat(dyb), p["w_up_b"][l], tb=True, name=nm("mmb_up_b")))
        (dob, dg_b), _ = rowwise_vjp(nm("ret_out_b"), f_ret_out,
                                     [Row(s["ob"], 4 * LANES), Row(proj, 4 * LANES, cb(lay.gb) // 4)], [], [], [dobn],
                                     h_ret, tmw)
        r1 = ret_bwd(s["qkr"], proj, cb(lay.vb), s["logits"][1], s["st_b"][1], dob, h_ret, 1, None, nm("ret_b1"))
        r0 = ret_bwd(s["qkr"], proj, cb(lay.vb), s["logits"][0], s["st_b"][0], dob, h_ret, 0, r1[:3], nm("ret_b0"))
        dqkr = jnp.concatenate([r0[0], r0[1]], axis=-1)
        (dqk,) = rowwise(nm("rope_b"), rope_b, [Row(dqkr, 2 * LANES)], [Tab(cos), Tab(sin)], [],
                         [(2 * d, 2 * LANES)], 2 * h_ret, tm)
        dv_b = r0[2]
        grads["ret_decay_logit"][l] = jnp.stack([r0[3][0, :h_ret], r1[3][0, :h_ret]])
        grads["w_up_a"][l] = matmul(flat(s["oan"]), flat(dya), ta=True, name=nm("mmg_up_a"))
        doan = unflat(matmul(flat(dya), p["w_up_a"][l], tb=True, name=nm("mmb_up_a")))
        (doa, dz), (dgain_a,) = rowwise_vjp(nm("gdn_out_b"), f_gdn_out,
                                            [Row(s["oa"], LANES), Row(proj, LANES, cb(lay.z))], [], [s["gain_a"]],
                                            [doan], h_gdn, tm)
        grads["gdn_norm"][l] = dgain_a[0]
        a1 = gdn_bwd(s["qa"], s["ka"], s["va"], s["gb"], s["st_a"][1], doa, h_gdn, 1, None, nm("gdn_b1"))
        a0 = gdn_bwd(s["qa"], s["ka"], s["va"], s["gb"], s["st_a"][0], doa, h_gdn, 0, a1, nm("gdn_b0"))
        (dab,), (dalog, ddtb) = rowwise_vjp(nm("gb_b"), f_gb, [Row(proj, LANES, cb(lay.ab))], [], [s["alog"], s["dtb"]],
                                            [a0[3]], 1, tm)
        grads["gdn_a_log"][l] = dalog[0, :2 * h_gdn].reshape(2, h_gdn)
        grads["gdn_dt_bias"][l] = ddtb[0, :2 * h_gdn].reshape(2, h_gdn)
        cw = s["cw"]
        dxq, dwq = conv_bwd(proj, 0, cw[:, :d], a0[0], True, nm("conv_q_b"))
        dxk, dwk = conv_bwd(proj, cb(d), cw[:, d:2 * d], a0[1], True, nm("conv_k_b"))
        dxv, dwv = conv_bwd(proj, cb(2 * d), cw[:, 2 * d:], a0[2], False, nm("conv_v_b"))
        grads["conv_w"][l] = jnp.concatenate([dwq, dwk, dwv], axis=1)
        dproj = jnp.concatenate([dxq, dxk, dxv, dz, dqk, dv_b, dg_b, dga, dgbt, dab,
                                 jnp.zeros((bsz, lp, lay.np - lay.used), F32)], axis=-1)
        grads["w_in"][l] = matmul(flat(s["hn"]), flat(dproj), ta=True, name=nm("mmg_in"))
        dhn = unflat(matmul(flat(dproj), p["w_in"][l], tb=True, name=nm("mmb_in")))
        (dh,), (dg_mix,) = rowwise_vjp(nm("rms_mix_b"), f_rms, [Row(s["h_in"], d)], [], [s["g_mix"]], [dhn], 1, tmw,
                                       adds={0: dh_mid})
        grads["norm_mix"][l] = dg_mix[0]

    out = {k: jnp.stack(v) for k, v in grads.items()}
    out["norm_final"] = d_final[0]
    grad_x = dh[:, CHUNK:]
    grad_meta = jnp.sum(dh[:, N_PAD:CHUNK], axis=0)
    return loss_row, grad_x, grad_meta, out


BIG = ("w_in", "w_up_a", "w_up_b", "w_out", "w_ffn_in", "w_ffn_out")
BIG_AXES = (2, 1, 1, 1, 2, 1)
SMALL_SHARDED = ("meta_tokens", "conv_w")
SMALL_AXES = (1, 2)
REPLICATED = ("norm_mix", "gdn_a_log", "gdn_dt_bias", "gdn_norm", "ret_decay_logit", "norm_ffn", "norm_final")
WEIGHTS = ("meta_tokens", "norm_mix", "w_in", "conv_w", "gdn_a_log", "gdn_dt_bias", "gdn_norm", "ret_decay_logit",
           "w_up_a", "w_up_b", "w_out", "norm_ffn", "w_ffn_in", "w_ffn_out", "norm_final")


def kernel(x, meta_tokens, norm_mix, w_in, conv_w, gdn_a_log, gdn_dt_bias, gdn_norm, ret_decay_logit, w_up_a, w_up_b, w_out, norm_ffn, w_ffn_in, w_ffn_out, norm_final, loss_target, m_meta_tokens, m_norm_mix, m_w_in, m_conv_w, m_gdn_a_log, m_gdn_dt_bias, m_gdn_norm, m_ret_decay_logit, m_w_up_a, m_w_up_b, m_w_out, m_norm_ffn, m_w_ffn_in, m_w_ffn_out, m_norm_final, v_meta_tokens, v_norm_mix, v_w_in, v_conv_w, v_gdn_a_log, v_gdn_dt_bias, v_gdn_norm, v_ret_decay_logit, v_w_up_a, v_w_up_b, v_w_out, v_norm_ffn, v_w_ffn_in, v_w_ffn_out, v_norm_final):
    w = dict(meta_tokens=meta_tokens, norm_mix=norm_mix, w_in=w_in, conv_w=conv_w, gdn_a_log=gdn_a_log,
             gdn_dt_bias=gdn_dt_bias, gdn_norm=gdn_norm, ret_decay_logit=ret_decay_logit, w_up_a=w_up_a,
             w_up_b=w_up_b, w_out=w_out, norm_ffn=norm_ffn, w_ffn_in=w_ffn_in, w_ffn_out=w_ffn_out,
             norm_final=norm_final)
    m = dict(meta_tokens=m_meta_tokens, norm_mix=m_norm_mix, w_in=m_w_in, conv_w=m_conv_w, gdn_a_log=m_gdn_a_log,
             gdn_dt_bias=m_gdn_dt_bias, gdn_norm=m_gdn_norm, ret_decay_logit=m_ret_decay_logit, w_up_a=m_w_up_a,
             w_up_b=m_w_up_b, w_out=m_w_out, norm_ffn=m_norm_ffn, w_ffn_in=m_w_ffn_in, w_ffn_out=m_w_ffn_out,
             norm_final=m_norm_final)
    v = dict(meta_tokens=v_meta_tokens, norm_mix=v_norm_mix, w_in=v_w_in, conv_w=v_conv_w, gdn_a_log=v_gdn_a_log,
             gdn_dt_bias=v_gdn_dt_bias, gdn_norm=v_gdn_norm, ret_decay_logit=v_ret_decay_logit, w_up_a=v_w_up_a,
             w_up_b=v_w_up_b, w_out=v_w_out, norm_ffn=v_norm_ffn, w_ffn_in=v_w_ffn_in, w_ffn_out=v_w_ffn_out,
             norm_final=v_norm_final)
    d = x.shape[-1]
    lay = Layout(d)
    sharded = BIG + SMALL_SHARDED
    sharded_axes = BIG_AXES + SMALL_AXES

    big_shapes = [w[k].shape for k in BIG]
    gathered = all_gather_hbm(_pack([w[k].astype(BF16) for k in BIG], 16), "gather_weights")
    full = dict(zip(BIG, _gathered_full(gathered, big_shapes, BIG_AXES)))
    small_shapes = [w[k].shape for k in SMALL_SHARDED]
    gathered_s = all_gather_small(_pack([w[k] for k in SMALL_SHARDED], 8), "gather_small")
    full.update(zip(SMALL_SHARDED, _gathered_full(gathered_s, small_shapes, SMALL_AXES)))
    p = {k: w[k] for k in REPLICATED}
    p.update({k: full[k] for k in BIG + ("conv_w",)})
    p["w_in"] = jnp.stack([lay.relayout_w_in(full["w_in"][l]) for l in range(full["w_in"].shape[0])])

    loss_row, grad_x, grad_meta, g = local_step(x, loss_target, full["meta_tokens"], p)
    g["meta_tokens"] = grad_meta
    g["w_in"] = jnp.stack([lay.unlayout_w_in(g["w_in"][l]) for l in range(g["w_in"].shape[0])])

    pieces = all_to_all_hbm(_pieces_by_dest([g[k].astype(BF16) for k in sharded], sharded_axes, 1024),
                            "exchange_grads")
    local_shapes = [w[k].shape for k in sharded]
    res = adamw(_pack([w[k] for k in sharded], 1024), _pack([m[k] for k in sharded], 1024),
                _pack([v[k] for k in sharded], 1024), pieces, "adamw_sharded")
    outs = {kind: dict(zip(sharded, _unpack(r, local_shapes))) for kind, r in zip(("g", "d", "m", "v"), res)}

    rep_shapes = [w[k].shape for k in REPLICATED]
    part = _pack([g[k] for k in REPLICATED] + [loss_row[0, :1]], 8)
    parts = all_gather_small(part, "gather_replicated")
    pad1 = lambda a: _pack([a[k] for k in REPLICATED] + [jnp.zeros((1,), F32)], 8)
    res_r = adamw(pad1(w), pad1(m), pad1(v), parts, "adamw_replicated")
    for kind, r in zip(("g", "d", "m", "v"), res_r):
        outs[kind].update(zip(REPLICATED, _unpack(r, rep_shapes)))
    loss = res_r[0].reshape(-1)[sum(math.prod(s) for s in rep_shapes)]

    return (loss, grad_x, *[outs["g"][k] for k in WEIGHTS], *[outs["d"][k] for k in WEIGHTS],
            *[outs["m"][k] for k in WEIGHTS], *[outs["v"][k] for k in WEIGHTS])
```

```python
import functools
import math

import jax
import jax.numpy as jnp
from jax import lax
from jax.experimental import pallas as pl
from jax.experimental.pallas import tpu as pltpu

F32 = jnp.float32
BF16 = jnp.bfloat16
HIGHEST = lax.Precision.HIGHEST

LANES = 128
CHUNK = 64
N_META = 16
N_PAD = CHUNK - N_META
CONV_K = 5
EPS = 1e-6
ROPE_BASE = 10000.0
N_DEV = 8
VMEM_LIMIT = 56 * 1024 * 1024

ADAM_LR, ADAM_B1, ADAM_B2, ADAM_EPS, ADAM_WD, ADAM_STEP = 0.001, 0.9, 0.999, 1e-08, 0.01, 10


def _tile(n, cap, mult):
    if n <= cap:
        return n
    best = None
    for t in range(mult, cap + 1, mult):
        if n % t == 0:
            best = t
    assert best is not None, (n, cap, mult)
    return best


def _params(sem):
    return pltpu.CompilerParams(dimension_semantics=sem, vmem_limit_bytes=VMEM_LIMIT)


def _raw_dot(a, b, ca, cb, exact):
    dims = (((ca,), (cb,)), ((), ()))
    a_hi, b_hi = a.astype(BF16), b.astype(BF16)
    out = lax.dot_general(a_hi, b_hi, dims, preferred_element_type=F32)
    if exact:
        a_lo = (a - a_hi.astype(F32)).astype(BF16)
        b_lo = (b - b_hi.astype(F32)).astype(BF16)
        out = out + lax.dot_general(a_hi, b_lo, dims, preferred_element_type=F32)
        out = out + lax.dot_general(a_lo, b_hi, dims, preferred_element_type=F32)
    return out


@functools.partial(jax.custom_vjp, nondiff_argnums=(2, 3, 4))
def _dot(a, b, ca, cb, exact):
    return _raw_dot(a, b, ca, cb, exact)


def _dot_fwd(a, b, ca, cb, exact):
    return _raw_dot(a, b, ca, cb, exact), (a, b)


def _dot_bwd(ca, cb, exact, res, g):
    a, b = res
    if ca == 1:
        da = _raw_dot(g, b, 1, 1 if cb == 0 else 0, exact)
    else:
        da = _raw_dot(b, g, 1 if cb == 0 else 0, 1, exact)
    if cb == 0:
        db = _raw_dot(a, g, 0 if ca == 1 else 1, 0, exact)
    else:
        db = _raw_dot(g, a, 0, 0 if ca == 1 else 1, exact)
    return da, db


_dot.defvjp(_dot_fwd, _dot_bwd)


@functools.partial(jax.custom_vjp, nondiff_argnums=(1, 2))
def _split(x, n, axis):
    return lax.slice_in_dim(x, 0, n, axis=axis), lax.slice_in_dim(x, n, x.shape[axis], axis=axis)


_split.defvjp(lambda x, n, axis: (_split(x, n, axis), None),
              lambda n, axis, _, g: (jnp.concatenate([g[0], g[1]], axis=axis),))


def _bdot(a, b, ca=1, cb=0):
    return _dot(a, b, ca, cb, False)


def _xdot(a, b, ca=1, cb=0):
    return _dot(a, b, ca, cb, True)


def matmul(a, b, *, ta=False, tb=False, add=None, name):
    m, k = (a.shape[1], a.shape[0]) if ta else a.shape
    k2, n = (b.shape[1], b.shape[0]) if tb else b.shape
    assert k == k2, (a.shape, b.shape, ta, tb)
    tm = _tile(m, 1040, 128 if ta else 8)
    tn = _tile(n, 512, 128)
    tk = _tile(k, 1664, 128 if (not ta or tb) else 16)
    nk = k // tk
    ca, cb = (0 if ta else 1), (1 if tb else 0)
    a_spec = pl.BlockSpec((tk, tm), lambda i, j, kk: (kk, i)) if ta else pl.BlockSpec((tm, tk), lambda i, j, kk: (i, kk))
    b_spec = pl.BlockSpec((tn, tk), lambda i, j, kk: (j, kk)) if tb else pl.BlockSpec((tk, tn), lambda i, j, kk: (kk, j))
    o_spec = pl.BlockSpec((tm, tn), lambda i, j, kk: (i, j))
    has_add = add is not None

    def body(*refs):
        a_ref, b_ref = refs[0], refs[1]
        add_ref = refs[2] if has_add else None
        o_ref = refs[3] if has_add else refs[2]
        part = _raw_dot(a_ref[...], b_ref[...], ca, cb, False)
        if nk == 1:
            o_ref[...] = part + add_ref[...] if has_add else part
            return
        acc_ref = refs[-1]
        kk = pl.program_id(2)

        @pl.when(kk == 0)
        def _():
            acc_ref[...] = part

        @pl.when(kk > 0)
        def _():
            acc_ref[...] += part

        @pl.when(kk == nk - 1)
        def _():
            o_ref[...] = acc_ref[...] + add_ref[...] if has_add else acc_ref[...]

    ins = [a, b] + ([add] if has_add else [])
    in_specs = [a_spec, b_spec] + ([o_spec] if has_add else [])
    return pl.pallas_call(
        body, name=name, grid=(m // tm, n // tn, nk), in_specs=in_specs, out_specs=o_spec,
        out_shape=jax.ShapeDtypeStruct((m, n), F32),
        scratch_shapes=[pltpu.VMEM((tm, tn), F32)] if nk > 1 else [],
        compiler_params=_params(("parallel", "parallel", "arbitrary")),
    )(*ins)


class Row:
    def __init__(self, arr, bc, off=0, per_head=True, diff=True):
        self.arr, self.bc, self.off, self.per_head, self.diff = arr, bc, off, per_head, diff


class Tab:
    def __init__(self, arr):
        self.arr = arr


def _row_specs(rows, tabs, pars, tm):
    specs = []
    for r in rows:
        specs.append(pl.BlockSpec((1, tm, r.bc), functools.partial(
            lambda b, i, h, off, ph: (b, i, off + (h if ph else 0)), off=r.off, ph=r.per_head)))
    for t in tabs:
        specs.append(pl.BlockSpec((tm, t.arr.shape[1]), lambda b, i, h: (i, 0)))
    for p in pars:
        specs.append(pl.BlockSpec(p.shape, lambda b, i, h: (0, 0)))
    return specs


def rowwise(name, fn, rows, tabs, pars, outs, nh, tm, out_dtype=F32):
    bsz, lp = rows[0].arr.shape[:2]
    nr, nt, npar = len(rows), len(tabs), len(pars)

    def body(*refs):
        t0 = pl.program_id(1) * tm
        ins = [refs[k][0] for k in range(nr)] + [refs[nr + k][...] for k in range(nt + npar)]
        res = fn(t0, *ins)
        for o_ref, o in zip(refs[nr + nt + npar:], res):
            o_ref[0] = o.astype(o_ref.dtype)

    return pl.pallas_call(
        body, name=name, grid=(bsz, lp // tm, nh),
        in_specs=_row_specs(rows, tabs, pars, tm),
        out_specs=[pl.BlockSpec((1, tm, bc), lambda b, i, h: (b, i, h)) for _, bc in outs],
        out_shape=[jax.ShapeDtypeStruct((bsz, lp, c), out_dtype) for c, _ in outs],
        compiler_params=_params(("parallel", "parallel", "parallel")),
    )(*[r.arr for r in rows], *[t.arr for t in tabs], *pars)


def rowwise_vjp(name, fn, rows, tabs, pars, couts, nh, tm, adds=None, narrow=()):
    bsz, lp = rows[0].arr.shape[:2]
    nr, nt, npar, nco = len(rows), len(tabs), len(pars), len(couts)
    adds = adds or {}
    add_keys = sorted(adds)
    diff_idx = [k for k, r in enumerate(rows) if r.diff]
    for k in diff_idx:
        assert rows[k].per_head or nh == 1

    def body(*refs):
        b, i, h = pl.program_id(0), pl.program_id(1), pl.program_id(2)
        t0 = i * tm
        pos = 0
        row_v = [refs[k][0] for k in range(nr)]
        pos += nr
        tab_v = [refs[pos + k][...] for k in range(nt)]
        pos += nt
        par_v = [refs[pos + k][...] for k in range(npar)]
        pos += npar
        co_v = [refs[pos + k][0] for k in range(nco)]
        pos += nco
        add_v = {key: refs[pos + k][0] for k, key in enumerate(add_keys)}
        pos += len(add_keys)
        drow_refs = refs[pos:pos + len(diff_idx)]
        dpar_refs = refs[pos + len(diff_idx):]

        def f(dvals, pvals):
            full = list(row_v)
            for k, v in zip(diff_idx, dvals):
                full[k] = v
            return tuple(fn(t0, *full, *tab_v, *pvals))

        _, pull = jax.vjp(f, [row_v[k] for k in diff_idx], par_v)
        d_rows, d_pars = pull(tuple(co_v))
        for ref, k, d in zip(drow_refs, diff_idx, d_rows):
            ref[0] = (d + add_v[k] if k in add_v else d).astype(ref.dtype)
        first = jnp.logical_and(jnp.logical_and(b == 0, i == 0), h == 0)
        for ref, d in zip(dpar_refs, d_pars):
            @pl.when(first)
            def _(ref=ref, d=d):
                ref[...] = d

            @pl.when(jnp.logical_not(first))
            def _(ref=ref, d=d):
                ref[...] += d

    out_block = lambda bc: pl.BlockSpec((1, tm, bc), lambda b, i, h: (b, i, h))
    in_specs = _row_specs(rows, tabs, pars, tm)
    in_specs += [out_block(c.shape[2] // nh) for c in couts]
    in_specs += [out_block(rows[k].bc) for k in add_keys]
    out_specs = [out_block(rows[k].bc) for k in diff_idx]
    out_specs += [pl.BlockSpec(p.shape, lambda b, i, h: (0, 0)) for p in pars]
    out_shape = [jax.ShapeDtypeStruct((bsz, lp, nh * rows[k].bc), BF16 if k in narrow else F32) for k in diff_idx]
    out_shape += [jax.ShapeDtypeStruct(p.shape, F32) for p in pars]
    res = pl.pallas_call(
        body, name=name, grid=(bsz, lp // tm, nh), in_specs=in_specs, out_specs=out_specs, out_shape=out_shape,
        compiler_params=_params(("arbitrary", "arbitrary", "arbitrary")),
    )(*[r.arr for r in rows], *[t.arr for t in tabs], *pars, *couts, *[adds[k] for k in add_keys])
    return res[:len(diff_idx)], res[len(diff_idx):]


def _real_rows(t0, tm):
    return (t0 + lax.broadcasted_iota(jnp.int32, (tm, 1), 0)) >= N_PAD


def f_rms(t0, x, gain):
    return (x * lax.rsqrt(jnp.mean(x * x, axis=-1, keepdims=True) + EPS) * gain,)


def make_f_gb(heads):
    def f_gb(t0, ab, alog, dtb):
        lane = lax.broadcasted_iota(jnp.int32, ab.shape, 1)
        g = -jnp.exp(alog) * jax.nn.softplus(ab + dtb)
        beta = jax.nn.sigmoid(ab)
        out = jnp.where(lane < 2 * heads, g, jnp.where(lane < 4 * heads, beta, 0.0))
        return (jnp.where(_real_rows(t0, ab.shape[0]), out, 0.0),)
    return f_gb


def f_gdn_out(t0, o, z, gain):
    on = o * lax.rsqrt(jnp.mean(o * o, axis=-1, keepdims=True) + EPS)
    return (on * gain * jax.nn.silu(z),)


def f_ret_out(t0, o, g):
    on = o * lax.rsqrt(jnp.mean(o * o, axis=-1, keepdims=True) + EPS)
    return (on * jax.nn.silu(g),)


def f_merge(t0, ga, gb, ya, yb):
    return (jax.nn.sigmoid(ga) * ya + jax.nn.sigmoid(gb) * yb,)


def f_swiglu(t0, x):
    gate, up = _split(x, x.shape[1] // 2, 1)
    return (jax.nn.silu(gate) * up,)


def make_f_rope(sign):
    def f_rope(t0, x, cos, sin):
        half = x.shape[1] // 2
        x1, x2 = x[:, :half], x[:, half:]
        s = sin * sign
        return (jnp.concatenate([x1 * cos - x2 * s, x1 * s + x2 * cos], axis=1),)
    return f_rope


def loss_head(h, gain, target, tm):
    bsz, lp, d = h.shape

    def body(h_ref, g_ref, t_ref, dh_ref, dg_ref, loss_ref):
        b, i = pl.program_id(0), pl.program_id(1)
        rows = (i * tm + lax.broadcasted_iota(jnp.int32, (tm, 1), 0)) >= CHUNK
        tgt = t_ref[0]

        def f(x, gain_v):
            y = f_rms(0, x, gain_v)[0]
            err = jnp.where(rows, y - tgt, 0.0)
            return 0.5 * jnp.sum(jnp.mean(err * err, axis=-1, keepdims=True), keepdims=True)

        val, pull = jax.vjp(f, h_ref[0], g_ref[...])
        dh, dg = pull(jnp.ones((1, 1), F32))
        dh_ref[0] = dh
        first = jnp.logical_and(b == 0, i == 0)
        val_row = jnp.broadcast_to(val, (1, LANES))

        @pl.when(first)
        def _():
            dg_ref[...] = dg
            loss_ref[...] = val_row

        @pl.when(jnp.logical_not(first))
        def _():
            dg_ref[...] += dg
            loss_ref[...] += val_row

    blk = pl.BlockSpec((1, tm, d), lambda b, i: (b, i, 0))
    return pl.pallas_call(
        body, name="loss_head", grid=(bsz, lp // tm),
        in_specs=[blk, pl.BlockSpec((1, d), lambda b, i: (0, 0)), blk],
        out_specs=[blk, pl.BlockSpec((1, d), lambda b, i: (0, 0)), pl.BlockSpec((1, LANES), lambda b, i: (0, 0))],
        out_shape=[jax.ShapeDtypeStruct((bsz, lp, d), F32), jax.ShapeDtypeStruct((1, d), F32),
                   jax.ShapeDtypeStruct((1, LANES), F32)],
        compiler_params=_params(("arbitrary", "arbitrary")),
    )(h, gain, target)


def _conv_pre(x, w_ref):
    lp = x.shape[0]
    acc = w_ref[2:3, :] * x
    for k in (0, 1, 3, 4):
        acc = acc + w_ref[k:k + 1, :] * pltpu.roll(x, (2 - k) % lp, 0)
    return acc


def conv_fwd(proj, off, w, l2, name):
    bsz, lp, _ = proj.shape
    d = w.shape[1]

    def body(x_ref, w_ref, o_ref):
        x = x_ref[0]
        s = jnp.where(_real_rows(0, lp), jax.nn.silu(_conv_pre(x, w_ref)), 0.0)
        if l2:
            s = s * lax.rsqrt(jnp.sum(s * s, axis=-1, keepdims=True) + EPS)
        o_ref[0] = s

    return pl.pallas_call(
        body, name=name, grid=(d // LANES, bsz),
        in_specs=[pl.BlockSpec((1, lp, LANES), lambda j, b: (b, 0, off + j)),
                  pl.BlockSpec((CONV_K, LANES), lambda j, b: (0, j))],
        out_specs=pl.BlockSpec((1, lp, LANES), lambda j, b: (b, 0, j)),
        out_shape=jax.ShapeDtypeStruct((bsz, lp, d), F32),
        compiler_params=_params(("parallel", "parallel")),
    )(proj, w)


def conv_bwd(proj, off, w, dy, l2, name):
    bsz, lp, _ = proj.shape
    d = w.shape[1]

    def body(x_ref, w_ref, dy_ref, dx_ref, dw_ref):
        b = pl.program_id(1)
        x, g = x_ref[0], dy_ref[0]
        real = _real_rows(0, lp)
        c = _conv_pre(x, w_ref)
        sg = jax.nn.sigmoid(c)
        s = jnp.where(real, c * sg, 0.0)
        if l2:
            r = lax.rsqrt(jnp.sum(s * s, axis=-1, keepdims=True) + EPS)
            g = r * g - s * (r * r * r) * jnp.sum(g * s, axis=-1, keepdims=True)
        dc = jnp.where(real, g * (sg * (1.0 + c * (1.0 - sg))), 0.0)
        dx = w_ref[2:3, :] * dc
        for k in (0, 1, 3, 4):
            dx = dx + w_ref[k:k + 1, :] * pltpu.roll(dc, (k - 2) % lp, 0)
        dx_ref[0] = jnp.where(real, dx, 0.0).astype(dx_ref.dtype)
        tap_row = lax.broadcasted_iota(jnp.int32, (CONV_K, LANES), 0)
        dw = jnp.zeros((CONV_K, LANES), F32)
        for k in range(CONV_K):
            xs = x if k == 2 else pltpu.roll(x, (2 - k) % lp, 0)
            dw = dw + jnp.where(tap_row == k, jnp.sum(dc * xs, axis=0, keepdims=True), 0.0)

        @pl.when(b == 0)
        def _():
            dw_ref[...] = dw

        @pl.when(b > 0)
        def _():
            dw_ref[...] += dw

    blk = pl.BlockSpec((1, lp, LANES), lambda j, b: (b, 0, j))
    return pl.pallas_call(
        body, name=name, grid=(d // LANES, bsz),
        in_specs=[pl.BlockSpec((1, lp, LANES), lambda j, b: (b, 0, off + j)),
                  pl.BlockSpec((CONV_K, LANES), lambda j, b: (0, j)), blk],
        out_specs=[blk, pl.BlockSpec((CONV_K, LANES), lambda j, b: (0, j))],
        out_shape=[jax.ShapeDtypeStruct((bsz, lp, d), BF16), jax.ShapeDtypeStruct((CONV_K, d), F32)],
        compiler_params=_params(("parallel", "arbitrary")),
    )(proj, w, dy)


def _tri_masks(rev):
    ii = lax.broadcasted_iota(jnp.int32, (CHUNK, CHUNK), 0)
    jj = lax.broadcasted_iota(jnp.int32, (CHUNK, CHUNK), 1)
    return ((ii <= jj), (ii < jj)) if rev else ((ii >= jj), (ii > jj))


def _lane_pick(block, lane):
    sel = lax.broadcasted_iota(jnp.int32, block.shape, 1) == lane
    return jnp.sum(jnp.where(sel, block, 0.0), axis=1, keepdims=True)


def _cumsum_impl(x, rev):
    n = x.shape[0]
    row = lax.broadcasted_iota(jnp.int32, x.shape, 0)
    step = 1
    while step < n:
        if rev:
            x = x + jnp.where(row < n - step, pltpu.roll(x, n - step, 0), 0.0)
        else:
            x = x + jnp.where(row >= step, pltpu.roll(x, step, 0), 0.0)
        step *= 2
    return x


@functools.partial(jax.custom_vjp, nondiff_argnums=(1,))
def _cumsum_rows(x, rev):
    return _cumsum_impl(x, rev)


_cumsum_rows.defvjp(lambda x, rev: (_cumsum_impl(x, rev), None),
                    lambda rev, _, g: (_cumsum_impl(g, not rev),))


def _unit_inv_impl(m):
    n = m.shape[0]
    eye = (lax.broadcasted_iota(jnp.int32, (n, n), 0) == lax.broadcasted_iota(jnp.int32, (n, n), 1)).astype(F32)
    p = -m
    inv = eye + p
    step = 2
    while step < n:
        p = _raw_dot(p, p, 1, 0, True)
        inv = inv + _raw_dot(inv, p, 1, 0, True)
        step *= 2
    return inv


@jax.custom_vjp
def _unit_inv(m):
    return _unit_inv_impl(m)


def _unit_inv_fwd(m):
    inv = _unit_inv_impl(m)
    return inv, inv


def _unit_inv_bwd(inv, g):
    return (-_raw_dot(_raw_dot(inv, g, 0, 0, True), inv, 1, 1, True),)


_unit_inv.defvjp(_unit_inv_fwd, _unit_inv_bwd)


GROUP = 2


def _unit_inv_all_impl(ms):
    n = ms[0].shape[0]
    width = GROUP * n
    dims = (((1,), (0,)), ((), ()))
    lane = lax.broadcasted_iota(jnp.int32, (n, width), 1)
    row = lax.broadcasted_iota(jnp.int32, (n, width), 0)
    blocks = [jnp.logical_and(lane >= t * n, lane < (t + 1) * n) for t in range(GROUP)]

    def halves(x):
        hi = x.astype(BF16)
        return hi, (x - hi.astype(F32)).astype(BF16)

    def xdot(a, b):
        diag = jnp.concatenate([jnp.where(blk, b, 0.0) for blk in blocks], axis=0)
        a_hi, a_lo = halves(a)
        d_hi, d_lo = halves(diag)
        out = lax.dot_general(a_hi, d_hi, dims, preferred_element_type=F32)
        out = out + lax.dot_general(a_hi, d_lo, dims, preferred_element_type=F32)
        return out + lax.dot_general(a_lo, d_hi, dims, preferred_element_type=F32)

    eye = (row == lane % n).astype(F32)
    groups = range(len(ms) // GROUP)
    p = [-jnp.concatenate(ms[t * GROUP:(t + 1) * GROUP], axis=1) for t in groups]
    inv = [eye + p[t] for t in groups]
    step = 2
    while step < n:
        p = [xdot(p[t], p[t]) for t in groups]
        inv = [inv[t] + xdot(inv[t], p[t]) for t in groups]
        step *= 2
    return [inv[t][:, u * n:(u + 1) * n] for t in groups for u in range(GROUP)]


@jax.custom_vjp
def _unit_inv_all(ms):
    return _unit_inv_all_impl(ms)


def _unit_inv_all_fwd(ms):
    out = _unit_inv_all_impl(ms)
    return out, out


_unit_inv_all.defvjp(_unit_inv_all_fwd, lambda invs, g: ([_unit_inv_bwd(i, gi)[0] for i, gi in zip(invs, g)],))


@jax.custom_vjp
def _unit_inv_known(m, inv):
    return inv


_unit_inv_known.defvjp(lambda m, inv: (inv, inv),
                       lambda inv, g: (_unit_inv_bwd(inv, g)[0], jnp.zeros_like(inv)))


def _gdn_chunk(qs, ks, vs, gb, ss, g_lanes, b_lanes, rev, invs=None, want_inv=False):
    nh = len(qs)
    dk = qs[0].shape[1]
    incl, strict = _tri_masks(rev)
    hs = range(nh)
    g = [_lane_pick(gb, l) for l in g_lanes]
    beta = [_lane_pick(gb, l) for l in b_lanes]
    qs = [q * (dk ** -0.5) for q in qs]
    gc_sq = [_cumsum_rows(jnp.broadcast_to(g[h], (CHUNK, CHUNK)), rev) for h in hs]
    gc = [_cumsum_rows(jnp.broadcast_to(g[h], (CHUNK, dk)), rev) for h in hs]
    g_last = [jnp.sum(g[h], axis=0, keepdims=True) for h in hs]
    decay = [jnp.where(incl, jnp.exp(jnp.where(incl, gc_sq[h] - gc_sq[h].T, 0.0)), 0.0) for h in hs]
    kb = [ks[h] * beta[h] for h in hs]
    kk = [_split(_bdot(jnp.concatenate([kb[h], qs[h]], axis=0), ks[h], 1, 1), CHUNK, 0) for h in hs]
    m = [jnp.where(strict, kk[h][0] * decay[h], 0.0) for h in hs]
    qk = [kk[h][1] * decay[h] for h in hs]
    if invs is not None:
        inv = [_unit_inv_known(m[h], invs[h]) for h in hs]
    elif nh % GROUP == 0:
        inv = _unit_inv_all(m)
    else:
        inv = [_unit_inv(m[h]) for h in hs]
    e_gc = [jnp.exp(gc[h]) for h in hs]
    uw = [_split(_bdot(inv[h], jnp.concatenate([vs[h] * beta[h], kb[h] * e_gc[h]], axis=1)), vs[h].shape[1], 1)
          for h in hs]
    u = [uw[h][0] for h in hs]
    w = [uw[h][1] for h in hs]
    q_dec = [qs[h] * e_gc[h] for h in hs]
    k_dec = [ks[h] * jnp.exp(g_last[h] - gc[h]) for h in hs]
    ws = [_split(_bdot(jnp.concatenate([w[h], q_dec[h]], axis=0), ss[h]), CHUNK, 0) for h in hs]
    v_new = [u[h] - ws[h][0] for h in hs]
    o = [ws[h][1] + _bdot(qk[h], v_new[h]) for h in hs]
    s_new = [ss[h] * jnp.exp(g_last[h]) + _bdot(k_dec[h], v_new[h], 0, 0) for h in hs]
    return (o, s_new, inv) if want_inv else (o, s_new)


def gdn_fwd(q, k, v, gb, heads, direction, o_prev, name):
    bsz, lp, d = q.shape
    nc = lp // CHUNK
    rev = direction == 1
    cm = (lambda n: nc - 1 - n) if rev else (lambda n: n)
    has_prev = o_prev is not None

    def body(*refs):
        q_ref, k_ref, v_ref, gb_ref = refs[:4]
        prev_ref = refs[4] if has_prev else None
        o_ref, st_ref, inv_ref, s_ref = refs[-4:]
        n = pl.program_id(1)

        @pl.when(n == 0)
        def _():
            s_ref[...] = jnp.zeros_like(s_ref)

        sls = [slice(h * LANES, (h + 1) * LANES) for h in range(heads)]
        ss = [s_ref[h] for h in range(heads)]
        for h in range(heads):
            st_ref[0, 0, h] = ss[h]
        os_, s_new, inv = _gdn_chunk([q_ref[0, :, sl] for sl in sls], [k_ref[0, :, sl] for sl in sls],
                                     [v_ref[0, :, sl] for sl in sls], gb_ref[0], ss,
                                     [direction * heads + h for h in range(heads)],
                                     [2 * heads + direction * heads + h for h in range(heads)], rev, want_inv=True)
        for h, sl in enumerate(sls):
            s_ref[h] = s_new[h]
            inv_ref[0, 0, h] = inv[h]
            o_ref[0, :, sl] = os_[h] + prev_ref[0, :, sl] if has_prev else os_[h]

    blk = pl.BlockSpec((1, CHUNK, d), lambda b, n: (b, cm(n), 0))
    gblk = pl.BlockSpec((1, CHUNK, LANES), lambda b, n: (b, cm(n), 0))
    st_blk = pl.BlockSpec((1, 1, heads, LANES, LANES), lambda b, n: (b, cm(n), 0, 0, 0))
    inv_blk = pl.BlockSpec((1, 1, heads, CHUNK, CHUNK), lambda b, n: (b, cm(n), 0, 0, 0))
    return pl.pallas_call(
        body, name=name, grid=(bsz, nc),
        in_specs=[blk, blk, blk, gblk] + ([blk] if has_prev else []),
        out_specs=[blk, st_blk, inv_blk],
        out_shape=[jax.ShapeDtypeStruct((bsz, lp, d), F32),
                   jax.ShapeDtypeStruct((bsz, nc, heads, LANES, LANES), F32),
                   jax.ShapeDtypeStruct((bsz, nc, heads, CHUNK, CHUNK), F32)],
        scratch_shapes=[pltpu.VMEM((heads, LANES, LANES), F32)],
        compiler_params=_params(("parallel", "arbitrary")),
    )(q, k, v, gb, *([o_prev] if has_prev else []))


def gdn_bwd(q, k, v, gb, states, invs, do, heads, direction, prev, name):
    bsz, lp, d = q.shape
    nc = lp // CHUNK
    rev = direction == 1
    cm = (lambda n: n) if rev else (lambda n: nc - 1 - n)
    has_prev = prev is not None

    def body(*refs):
        q_ref, k_ref, v_ref, gb_ref, st_ref, inv_ref, do_ref = refs[:7]
        prev_refs = refs[7:11] if has_prev else None
        dq_ref, dk_ref, dv_ref, dgb_ref, ds_ref = refs[-5:]
        n = pl.program_id(1)

        @pl.when(n == 0)
        def _():
            ds_ref[...] = jnp.zeros_like(ds_ref)

        sls = [slice(h * LANES, (h + 1) * LANES) for h in range(heads)]
        f = functools.partial(_gdn_chunk, g_lanes=[direction * heads + h for h in range(heads)],
                              b_lanes=[2 * heads + direction * heads + h for h in range(heads)], rev=rev,
                              invs=[inv_ref[0, 0, h] for h in range(heads)])
        _, pull = jax.vjp(f, [q_ref[0, :, sl] for sl in sls], [k_ref[0, :, sl] for sl in sls],
                          [v_ref[0, :, sl] for sl in sls], gb_ref[0], [st_ref[0, 0, h] for h in range(heads)])
        dq, dk, dv, dgb, ds = pull(([do_ref[0, :, sl] for sl in sls], [ds_ref[h] for h in range(heads)]))
        for h, sl in enumerate(sls):
            ds_ref[h] = ds[h]
            if has_prev:
                dq[h], dk[h], dv[h] = (dq[h] + prev_refs[0][0, :, sl], dk[h] + prev_refs[1][0, :, sl],
                                       dv[h] + prev_refs[2][0, :, sl])
            dq_ref[0, :, sl] = dq[h]
            dk_ref[0, :, sl] = dk[h]
            dv_ref[0, :, sl] = dv[h]
        dgb_ref[0] = dgb + prev_refs[3][0] if has_prev else dgb

    blk = pl.BlockSpec((1, CHUNK, d), lambda b, n: (b, cm(n), 0))
    gblk = pl.BlockSpec((1, CHUNK, LANES), lambda b, n: (b, cm(n), 0))
    st_blk = pl.BlockSpec((1, 1, heads, LANES, LANES), lambda b, n: (b, cm(n), 0, 0, 0))
    inv_blk = pl.BlockSpec((1, 1, heads, CHUNK, CHUNK), lambda b, n: (b, cm(n), 0, 0, 0))
    big = jax.ShapeDtypeStruct((bsz, lp, d), F32)
    return pl.pallas_call(
        body, name=name, grid=(bsz, nc),
        in_specs=[blk, blk, blk, gblk, st_blk, inv_blk, blk] + ([blk, blk, blk, gblk] if has_prev else []),
        out_specs=[blk, blk, blk, gblk],
        out_shape=[big, big, big, jax.ShapeDtypeStruct((bsz, lp, LANES), F32)],
        scratch_shapes=[pltpu.VMEM((heads, LANES, LANES), F32)],
        compiler_params=_params(("parallel", "arbitrary")),
    )(q, k, v, gb, states, invs, do, *(list(prev) if has_prev else []))


def _ret_chunk(q, k, v, r, logit, lane, rev):
    dk = q.shape[1]
    lg = jax.nn.log_sigmoid(_lane_pick(logit, lane))
    k = k * (dk ** -0.5)
    ii = lax.broadcasted_iota(jnp.int32, (CHUNK, CHUNK), 0)
    jj = lax.broadcasted_iota(jnp.int32, (CHUNK, CHUNK), 1)
    pos = lax.broadcasted_iota(jnp.int32, (CHUNK, 1), 0)
    if rev:
        incl, rel = ii <= jj, (jj - ii)
        seen = (CHUNK - 1 - pos)
    else:
        incl, rel = ii >= jj, (ii - jj)
        seen = pos
    relf = jnp.where(incl, rel, 0).astype(F32)
    seenf = seen.astype(F32)
    intra = jnp.where(incl, jnp.exp(relf * lg), 0.0)
    qk = _bdot(q, k, 1, 1) * intra
    q_dec = q * jnp.exp(lg * (seenf + 1.0))
    k_dec = k * jnp.exp(lg * (CHUNK - 1.0 - seenf))
    o = _bdot(q_dec, r) + _bdot(qk, v)
    r_new = r * jnp.exp(lg * CHUNK) + _bdot(k_dec, v, 0, 0)
    return o, r_new


def ret_fwd(qk, v_arr, v_off, logit, heads, direction, o_prev, name):
    bsz, lp, d2 = qk.shape
    d = d2 // 2
    dkh, dvh = d // heads, 2 * d // heads
    nc = lp // CHUNK
    rev = direction == 1
    cm = (lambda n: nc - 1 - n) if rev else (lambda n: n)
    has_prev = o_prev is not None
    v_cb = v_off * LANES // (2 * d)
    assert v_cb * 2 * d == v_off * LANES

    def body(*refs):
        q_ref, k_ref, v_ref, lg_ref = refs[:4]
        prev_ref = refs[4] if has_prev else None
        o_ref, st_ref, r_ref = refs[-3], refs[-2], refs[-1]
        n = pl.program_id(1)

        @pl.when(n == 0)
        def _():
            r_ref[...] = jnp.zeros_like(r_ref)

        lgv = lg_ref[...]
        for h in range(heads):
            ks, vs = slice(h * dkh, (h + 1) * dkh), slice(h * dvh, (h + 1) * dvh)
            r = r_ref[h]
            st_ref[0, 0, h] = r
            o, r_new = _ret_chunk(q_ref[0, :, ks], k_ref[0, :, ks], v_ref[0, :, vs], r, lgv, h, rev)
            r_ref[h] = r_new
            o_ref[0, :, vs] = o + prev_ref[0, :, vs] if has_prev else o

    qblk = pl.BlockSpec((1, CHUNK, d), lambda b, n: (b, cm(n), 0))
    kblk = pl.BlockSpec((1, CHUNK, d), lambda b, n: (b, cm(n), 1))
    vblk = pl.BlockSpec((1, CHUNK, 2 * d), lambda b, n: (b, cm(n), v_cb))
    oblk = pl.BlockSpec((1, CHUNK, 2 * d), lambda b, n: (b, cm(n), 0))
    st_blk = pl.BlockSpec((1, 1, heads, dkh, dvh), lambda b, n: (b, cm(n), 0, 0, 0))
    return pl.pallas_call(
        body, name=name, grid=(bsz, nc),
        in_specs=[qblk, kblk, vblk, pl.BlockSpec((1, LANES), lambda b, n: (0, 0))] + ([oblk] if has_prev else []),
        out_specs=[oblk, st_blk],
        out_shape=[jax.ShapeDtypeStruct((bsz, lp, 2 * d), F32),
                   jax.ShapeDtypeStruct((bsz, nc, heads, dkh, dvh), F32)],
        scratch_shapes=[pltpu.VMEM((heads, dkh, dvh), F32)],
        compiler_params=_params(("parallel", "arbitrary")),
    )(qk, qk, v_arr, logit, *([o_prev] if has_prev else []))


def ret_bwd(qk, v_arr, v_off, logit, states, do, heads, direction, prev, name, dv_dtype=F32):
    bsz, lp, d2 = qk.shape
    d = d2 // 2
    dkh, dvh = d // heads, 2 * d // heads
    nc = lp // CHUNK
    rev = direction == 1
    cm = (lambda n: n) if rev else (lambda n: nc - 1 - n)
    has_prev = prev is not None
    v_cb = v_off * LANES // (2 * d)

    def body(*refs):
        q_ref, k_ref, v_ref, lg_ref, st_ref, do_ref = refs[:6]
        prev_refs = refs[6:8] if has_prev else None
        dqk_ref, dv_ref, dlg_ref, dr_ref = refs[-4:]
        b, n = pl.program_id(0), pl.program_id(1)

        @pl.when(n == 0)
        def _():
            dr_ref[...] = jnp.zeros_like(dr_ref)

        lgv = lg_ref[...]
        dlg = jnp.zeros((1, LANES), F32)
        for h in range(heads):
            ks, vs = slice(h * dkh, (h + 1) * dkh), slice(h * dvh, (h + 1) * dvh)
            f = functools.partial(_ret_chunk, lane=h, rev=rev)
            _, pull = jax.vjp(f, q_ref[0, :, ks], k_ref[0, :, ks], v_ref[0, :, vs], st_ref[0, 0, h], lgv)
            dq, dk, dv, dr, dlg_h = pull((do_ref[0, :, vs], dr_ref[h]))
            dr_ref[h] = dr
            dlg = dlg + dlg_h
            kks = slice(d + h * dkh, d + (h + 1) * dkh)
            if has_prev:
                dq, dk, dv = dq + prev_refs[0][0, :, ks], dk + prev_refs[0][0, :, kks], dv + prev_refs[1][0, :, vs]
            dqk_ref[0, :, ks] = dq
            dqk_ref[0, :, kks] = dk
            dv_ref[0, :, vs] = dv.astype(dv_ref.dtype)
        first = jnp.logical_and(b == 0, n == 0)

        @pl.when(first)
        def _():
            dlg_ref[...] = dlg

        @pl.when(jnp.logical_not(first))
        def _():
            dlg_ref[...] += dlg

    qblk = pl.BlockSpec((1, CHUNK, d), lambda b, n: (b, cm(n), 0))
    kblk = pl.BlockSpec((1, CHUNK, d), lambda b, n: (b, cm(n), 1))
    vblk = pl.BlockSpec((1, CHUNK, 2 * d), lambda b, n: (b, cm(n), v_cb))
    oblk = pl.BlockSpec((1, CHUNK, 2 * d), lambda b, n: (b, cm(n), 0))
    lblk = pl.BlockSpec((1, LANES), lambda b, n: (0, 0))
    st_blk = pl.BlockSpec((1, 1, heads, dkh, dvh), lambda b, n: (b, cm(n), 0, 0, 0))
    return pl.pallas_call(
        body, name=name, grid=(bsz, nc),
        in_specs=[qblk, kblk, vblk, lblk, st_blk, oblk] + ([oblk, oblk] if has_prev else []),
        out_specs=[oblk, oblk, lblk],
        out_shape=[jax.ShapeDtypeStruct((bsz, lp, 2 * d), F32), jax.ShapeDtypeStruct((bsz, lp, 2 * d), dv_dtype),
                   jax.ShapeDtypeStruct((1, LANES), F32)],
        scratch_shapes=[pltpu.VMEM((heads, dkh, dvh), F32)],
        compiler_params=_params(("arbitrary", "arbitrary")),
    )(qk, qk, v_arr, logit, states, do, *(list(prev) if has_prev else []))


def _flip(v, bit):
    return 1 - v if bit else v


def _peer(x, y, c, off):
    return (_flip(x, off & 4), _flip(y, off & 2), _flip(c, off & 1))


def all_gather_hbm(block, name):
    r, lanes = block.shape

    def body(x_ref, out_ref, send_sems, recv_sems, local_sem):
        x, y, c = lax.axis_index("x"), lax.axis_index("y"), lax.axis_index("c")
        me, sibling = (x, y, c), (x, y, 1 - c)
        chips = [(1 - x, y), (x, 1 - y), (1 - x, 1 - y)]

        def slot(px, py, pc):
            return out_ref.at[4 * px + 2 * py + pc]

        def copy(k, block_of, to, src=None):
            return pltpu.make_async_remote_copy(
                src_ref=slot(*block_of) if src is None else src, dst_ref=slot(*block_of),
                send_sem=send_sems.at[k], recv_sem=recv_sems.at[k], device_id=to, device_id_type=pl.DeviceIdType.MESH)

        mine = pltpu.make_async_copy(x_ref, slot(*me), local_sem)
        mine.start()
        first = [copy(0, me, sibling, src=x_ref)]
        first += [copy(1 + j, me, (*chip, c), src=x_ref) for j, chip in enumerate(chips)]
        for cp in first:
            cp.start()
        passed = [copy(4 + j, (*chip, c), sibling) for j, chip in enumerate(chips)]
        for j, chip in enumerate(chips):
            copy(1 + j, (*chip, c), me).wait_recv()
            passed[j].start()
        copy(0, sibling, me).wait_recv()
        for j, chip in enumerate(chips):
            copy(4 + j, (*chip, 1 - c), me).wait_recv()
        for cp in first + passed:
            cp.wait_send()
        mine.wait()

    return pl.pallas_call(
        body, name=name, out_shape=jax.ShapeDtypeStruct((N_DEV, r, lanes), block.dtype),
        in_specs=[pl.BlockSpec(memory_space=pl.ANY)], out_specs=pl.BlockSpec(memory_space=pl.ANY),
        scratch_shapes=[pltpu.SemaphoreType.DMA((7,)), pltpu.SemaphoreType.DMA((7,)), pltpu.SemaphoreType.DMA],
    )(block)


def all_gather_small(block, name):
    r, lanes = block.shape

    def body(x_ref, out_ref, send_sems, recv_sems):
        x, y, c = lax.axis_index("x"), lax.axis_index("y"), lax.axis_index("c")
        me = 4 * x + 2 * y + c
        out_ref[me] = x_ref[...]
        copies = []
        for off in range(1, N_DEV):
            copies.append(pltpu.make_async_remote_copy(
                src_ref=x_ref, dst_ref=out_ref.at[me], send_sem=send_sems.at[off - 1], recv_sem=recv_sems.at[off - 1],
                device_id=_peer(x, y, c, off), device_id_type=pl.DeviceIdType.MESH))
        for cp in copies:
            cp.start()
        for cp in copies:
            cp.wait()

    return pl.pallas_call(
        body, name=name, out_shape=jax.ShapeDtypeStruct((N_DEV, r, lanes), block.dtype),
        in_specs=[pl.BlockSpec(memory_space=pltpu.VMEM)], out_specs=pl.BlockSpec(memory_space=pltpu.VMEM),
        scratch_shapes=[pltpu.SemaphoreType.DMA((7,)), pltpu.SemaphoreType.DMA((7,))],
    )(block)


def all_to_all_hbm(pieces, name):
    def body(x_ref, out_ref, send_sems, recv_sems, local_sem):
        x, y, c = lax.axis_index("x"), lax.axis_index("y"), lax.axis_index("c")
        me = 4 * x + 2 * y + c
        mine = pltpu.make_async_copy(x_ref.at[me], out_ref.at[me], local_sem)
        mine.start()
        copies = []
        for off in range(1, N_DEV):
            px, py, pc = _peer(x, y, c, off)
            copies.append(pltpu.make_async_remote_copy(
                src_ref=x_ref.at[4 * px + 2 * py + pc], dst_ref=out_ref.at[me],
                send_sem=send_sems.at[off - 1], recv_sem=recv_sems.at[off - 1],
                device_id=(px, py, pc), device_id_type=pl.DeviceIdType.MESH))
        for cp in copies:
            cp.start()
        for cp in copies:
            cp.wait()
        mine.wait()

    return pl.pallas_call(
        body, name=name, out_shape=jax.ShapeDtypeStruct(pieces.shape, pieces.dtype),
        in_specs=[pl.BlockSpec(memory_space=pl.ANY)], out_specs=pl.BlockSpec(memory_space=pl.ANY),
        scratch_shapes=[pltpu.SemaphoreType.DMA((7,)), pltpu.SemaphoreType.DMA((7,)), pltpu.SemaphoreType.DMA],
    )(pieces)


def adamw(w, m, v, g8, name):
    r = w.shape[0]
    tm = _tile(r, 1024, 8)

    def body(w_ref, m_ref, v_ref, g_ref, g_out, d_out, m_out, v_out):
        g = g_ref[0].astype(F32)
        for s in range(1, N_DEV):
            g = g + g_ref[s].astype(F32)
        mn = ADAM_B1 * m_ref[...] + (1.0 - ADAM_B1) * g
        vn = ADAM_B2 * v_ref[...] + (1.0 - ADAM_B2) * (g * g)
        m_hat = mn / (1.0 - ADAM_B1 ** ADAM_STEP)
        v_hat = vn / (1.0 - ADAM_B2 ** ADAM_STEP)
        g_out[...] = g
        d_out[...] = -ADAM_LR * (m_hat / (jnp.sqrt(v_hat) + ADAM_EPS) + ADAM_WD * w_ref[...])
        m_out[...] = mn
        v_out[...] = vn

    blk = pl.BlockSpec((tm, LANES), lambda i: (i, 0))
    shp = jax.ShapeDtypeStruct((r, LANES), F32)
    return pl.pallas_call(
        body, name=name, grid=(r // tm,),
        in_specs=[blk, blk, blk, pl.BlockSpec((N_DEV, tm, LANES), lambda i: (0, i, 0))],
        out_specs=[blk, blk, blk, blk], out_shape=[shp, shp, shp, shp],
        compiler_params=_params(("parallel",)),
    )(w, m, v, g8)


def _pack(blocks, rows_mult):
    flat = jnp.concatenate([b.reshape(-1) for b in blocks])
    unit = rows_mult * LANES
    total = -(-flat.shape[0] // unit) * unit
    return jnp.pad(flat, (0, total - flat.shape[0])).reshape(-1, LANES)


def _unpack(packed, shapes):
    flat = packed.reshape(-1)
    out, pos = [], 0
    for s in shapes:
        n = math.prod(s)
        out.append(flat[pos:pos + n].reshape(s))
        pos += n
    return out


def _gathered_full(gathered, shapes, axes):
    per_dev = [_unpack(gathered[d], shapes) for d in range(N_DEV)]
    return [jnp.concatenate([per_dev[d][k] for d in range(N_DEV)], axis=axes[k]) for k in range(len(shapes))]


def _pieces_by_dest(fulls, axes, rows_mult):
    packs = []
    for d in range(N_DEV):
        blocks = []
        for f, ax in zip(fulls, axes):
            n = f.shape[ax] // N_DEV
            blocks.append(lax.slice_in_dim(f, d * n, (d + 1) * n, axis=ax))
        packs.append(_pack(blocks, rows_mult))
    return jnp.stack(packs)


class Layout:
    def __init__(self, d):
        self.d = d
        self.h = d // 128
        self.hr = d // 256
        self.z = 3 * d
        self.qb = 4 * d
        self.vb = 6 * d
        self.gb = 8 * d
        self.ga = 10 * d
        self.gbt = 11 * d
        self.ab = 12 * d
        self.used = 12 * d + LANES
        self.np = -(-self.used // 512) * 512

    def relayout_w_in(self, w):
        d, h4 = self.d, 4 * self.h
        return jnp.concatenate([w[:, :4 * d], w[:, 4 * d + h4:], w[:, 4 * d:4 * d + h4],
                                jnp.zeros((d, self.np - 12 * d - h4), w.dtype)], axis=1)

    def unlayout_w_in(self, w):
        d, h4 = self.d, 4 * self.h
        return jnp.concatenate([w[:, :4 * d], w[:, 12 * d:12 * d + h4], w[:, 4 * d:12 * d]], axis=1)


def _lane_row(vec):
    return jnp.pad(vec.reshape(-1), (0, LANES - vec.size)).reshape(1, LANES)


def _rope_tables(lp, half):
    inv = ROPE_BASE ** (-jnp.arange(half, dtype=F32) / half)
    pos = (jnp.arange(lp) - N_PAD).astype(F32)
    ang = pos[:, None] * inv[None, :]
    return jnp.cos(ang), jnp.sin(ang)


def local_step(x, target, meta, p):
    bsz, seq, d = x.shape
    lay = Layout(d)
    h_gdn, h_ret = lay.h, lay.hr
    lp = seq + CHUNK
    t_all = bsz * lp
    depth = p["w_up_a"].shape[0]
    ff = p["w_ffn_out"].shape[1]
    tm = _tile(lp, 512, 8)
    tmw = _tile(lp, 256, 8)
    cb = lambda cols: cols // LANES
    flat = lambda a: a.reshape(t_all, a.shape[-1])
    unflat = lambda a: a.reshape(bsz, lp, a.shape[-1])
    cos, sin = _rope_tables(lp, LANES)
    f_gb = make_f_gb(h_gdn)
    rope_f, rope_b = make_f_rope(1.0), make_f_rope(-1.0)

    head = jnp.concatenate([jnp.zeros((N_PAD, d), F32), meta], axis=0)
    h = jnp.concatenate([jnp.broadcast_to(head[None], (bsz, CHUNK, d)), x], axis=1)
    tgt = jnp.pad(target, ((0, 0), (CHUNK, 0), (0, 0)))

    saved = []
    for l in range(depth):
        s = {"h_in": h}
        nm = lambda k: f"l{l}_{k}"
        g_mix, g_ffn = p["norm_mix"][l][None], p["norm_ffn"][l][None]
        alog, dtb = _lane_row(p["gdn_a_log"][l]), _lane_row(p["gdn_dt_bias"][l])
        gain_a = p["gdn_norm"][l][None]
        logits = [_lane_row(p["ret_decay_logit"][l][0]), _lane_row(p["ret_decay_logit"][l][1])]
        cw = p["conv_w"][l]
        (hn,) = rowwise(nm("rms_mix"), f_rms, [Row(h, d)], [], [g_mix], [(d, d)], 1, tm, BF16)
        proj = unflat(matmul(flat(hn), p["w_in"][l], name=nm("mm_in")))
        qa = conv_fwd(proj, 0, cw[:, :d], True, nm("conv_q"))
        ka = conv_fwd(proj, cb(d), cw[:, d:2 * d], True, nm("conv_k"))
        va = conv_fwd(proj, cb(2 * d), cw[:, 2 * d:], False, nm("conv_v"))
        (gb,) = rowwise(nm("gb"), f_gb, [Row(proj, LANES, cb(lay.ab))], [], [alog, dtb], [(LANES, LANES)], 1, tm)
        o0, st_a0, iv_a0 = gdn_fwd(qa, ka, va, gb, h_gdn, 0, None, nm("gdn_f0"))
        oa, st_a1, iv_a1 = gdn_fwd(qa, ka, va, gb, h_gdn, 1, o0, nm("gdn_f1"))
        (oan,) = rowwise(nm("gdn_out"), f_gdn_out, [Row(oa, LANES), Row(proj, LANES, cb(lay.z))], [], [gain_a],
                         [(d, LANES)], h_gdn, tm, BF16)
        ya = unflat(matmul(flat(oan), p["w_up_a"][l], name=nm("mm_up_a")))
        (qkr,) = rowwise(nm("rope"), rope_f, [Row(proj, 2 * LANES, cb(lay.qb) // 2)], [Tab(cos), Tab(sin)], [],
                         [(2 * d, 2 * LANES)], 2 * h_ret, tm)
        r0, st_b0 = ret_fwd(qkr, proj, cb(lay.vb), logits[0], h_ret, 0, None, nm("ret_f0"))
        ob, st_b1 = ret_fwd(qkr, proj, cb(lay.vb), logits[1], h_ret, 1, r0, nm("ret_f1"))
        (obn,) = rowwise(nm("ret_out"), f_ret_out, [Row(ob, 4 * LANES), Row(proj, 4 * LANES, cb(lay.gb) // 4)], [], [],
                         [(2 * d, 4 * LANES)], h_ret, tmw, BF16)
        yb = unflat(matmul(flat(obn), p["w_up_b"][l], name=nm("mm_up_b")))
        (mg,) = rowwise(nm("merge"), f_merge,
                        [Row(proj, LANES, cb(lay.ga)), Row(proj, LANES, cb(lay.gbt)), Row(ya, LANES), Row(yb, LANES)],
                        [], [], [(d, LANES)], cb(d), tm, BF16)
        h_mid = unflat(matmul(flat(mg), p["w_out"][l], add=flat(h), name=nm("mm_out")))
        (hn2,) = rowwise(nm("rms_ffn"), f_rms, [Row(h_mid, d)], [], [g_ffn], [(d, d)], 1, tm, BF16)
        ffp = unflat(matmul(flat(hn2), p["w_ffn_in"][l], name=nm("mm_ffn_in")))
        (act,) = rowwise(nm("swiglu"), f_swiglu, [Row(ffp, 2 * ff)], [], [], [(ff, ff)], 1, tmw, BF16)
        h = unflat(matmul(flat(act), p["w_ffn_out"][l], add=flat(h_mid), name=nm("mm_ffn_out")))
        s.update(hn=hn, proj=proj, qa=qa, ka=ka, va=va, gb=gb, st_a=(st_a0, st_a1), iv_a=(iv_a0, iv_a1), oa=oa,
                 oan=oan, ya=ya, qkr=qkr, st_b=(st_b0, st_b1), ob=ob, obn=obn, yb=yb, mg=mg, h_mid=h_mid, hn2=hn2,
                 ffp=ffp, act=act, logits=logits, alog=alog, dtb=dtb, gain_a=gain_a, cw=cw, g_mix=g_mix, g_ffn=g_ffn)
        saved.append(s)

    dh, d_final, loss_row = loss_head(h, p["norm_final"][None], tgt, tm)

    grads = {k: [None] * depth for k in ("norm_mix", "w_in", "conv_w", "gdn_a_log", "gdn_dt_bias", "gdn_norm",
                                          "ret_decay_logit", "w_up_a", "w_up_b", "w_out", "norm_ffn", "w_ffn_in",
                                          "w_ffn_out")}
    for l in reversed(range(depth)):
        s = saved[l]
        nm = lambda k: f"l{l}_{k}"
        proj = s["proj"]
        dhf = flat(dh)
        grads["w_ffn_out"][l] = matmul(flat(s["act"]), dhf, ta=True, name=nm("mmg_ffn_out"))
        dact = unflat(matmul(dhf, p["w_ffn_out"][l], tb=True, name=nm("mmb_ffn_out")))
        (dffp,), _ = rowwise_vjp(nm("swiglu_b"), f_swiglu, [Row(s["ffp"], 2 * ff)], [], [], [dact], 1, tmw, narrow=(0,))
        grads["w_ffn_in"][l] = matmul(flat(s["hn2"]), flat(dffp), ta=True, name=nm("mmg_ffn_in"))
        dhn2 = unflat(matmul(flat(dffp), p["w_ffn_in"][l], tb=True, name=nm("mmb_ffn_in")))
        (dh_mid,), (dg_ffn,) = rowwise_vjp(nm("rms_ffn_b"), f_rms, [Row(s["h_mid"], d)], [], [s["g_ffn"]], [dhn2], 1, tmw,
                                           adds={0: dh})
        grads["norm_ffn"][l] = dg_ffn[0]
        dmf = flat(dh_mid)
        grads["w_out"][l] = matmul(flat(s["mg"]), dmf, ta=True, name=nm("mmg_out"))
        dmg = unflat(matmul(dmf, p["w_out"][l], tb=True, name=nm("mmb_out")))
        (dga, dgbt, dya, dyb), _ = rowwise_vjp(
            nm("merge_b"), f_merge,
            [Row(proj, LANES, cb(lay.ga)), Row(proj, LANES, cb(lay.gbt)), Row(s["ya"], LANES), Row(s["yb"], LANES)],
            [], [], [dmg], cb(d), tm, narrow=(0, 1, 2, 3))
        grads["w_up_b"][l] = matmul(flat(s["obn"]), flat(dyb), ta=True, name=nm("mmg_up_b"))
        dobn = unflat(matmul(flat(dyb), p["w_up_b"][l], tb=True, name=nm("mmb_up_b")))
        (dob, dg_b), _ = rowwise_vjp(nm("ret_out_b"), f_ret_out,
                                     [Row(s["ob"], 4 * LANES), Row(proj, 4 * LANES, cb(lay.gb) // 4)], [], [], [dobn],
                                     h_ret, tmw, narrow=(1,))
        r1 = ret_bwd(s["qkr"], proj, cb(lay.vb), s["logits"][1], s["st_b"][1], dob, h_ret, 1, None, nm("ret_b1"))
        r0 = ret_bwd(s["qkr"], proj, cb(lay.vb), s["logits"][0], s["st_b"][0], dob, h_ret, 0, r1[:2], nm("ret_b0"),
                     dv_dtype=BF16)
        (dqk,) = rowwise(nm("rope_b"), rope_b, [Row(r0[0], 2 * LANES)], [Tab(cos), Tab(sin)], [],
                         [(2 * d, 2 * LANES)], 2 * h_ret, tm, BF16)
        dv_b = r0[1]
        grads["ret_decay_logit"][l] = jnp.stack([r0[2][0, :h_ret], r1[2][0, :h_ret]])
        grads["w_up_a"][l] = matmul(flat(s["oan"]), flat(dya), ta=True, name=nm("mmg_up_a"))
        doan = unflat(matmul(flat(dya), p["w_up_a"][l], tb=True, name=nm("mmb_up_a")))
        (doa, dz), (dgain_a,) = rowwise_vjp(nm("gdn_out_b"), f_gdn_out,
                                            [Row(s["oa"], LANES), Row(proj, LANES, cb(lay.z))], [], [s["gain_a"]],
                                            [doan], h_gdn, tm, narrow=(1,))
        grads["gdn_norm"][l] = dgain_a[0]
        a1 = gdn_bwd(s["qa"], s["ka"], s["va"], s["gb"], s["st_a"][1], s["iv_a"][1], doa, h_gdn, 1, None, nm("gdn_b1"))
        a0 = gdn_bwd(s["qa"], s["ka"], s["va"], s["gb"], s["st_a"][0], s["iv_a"][0], doa, h_gdn, 0, a1, nm("gdn_b0"))
        (dab,), (dalog, ddtb) = rowwise_vjp(nm("gb_b"), f_gb, [Row(proj, LANES, cb(lay.ab))], [], [s["alog"], s["dtb"]],
                                            [a0[3]], 1, tm, narrow=(0,))
        grads["gdn_a_log"][l] = dalog[0, :2 * h_gdn].reshape(2, h_gdn)
        grads["gdn_dt_bias"][l] = ddtb[0, :2 * h_gdn].reshape(2, h_gdn)
        cw = s["cw"]
        dxq, dwq = conv_bwd(proj, 0, cw[:, :d], a0[0], True, nm("conv_q_b"))
        dxk, dwk = conv_bwd(proj, cb(d), cw[:, d:2 * d], a0[1], True, nm("conv_k_b"))
        dxv, dwv = conv_bwd(proj, cb(2 * d), cw[:, 2 * d:], a0[2], False, nm("conv_v_b"))
        grads["conv_w"][l] = jnp.concatenate([dwq, dwk, dwv], axis=1)
        dproj = jnp.concatenate([dxq, dxk, dxv, dz, dqk, dv_b, dg_b, dga, dgbt, dab,
                                 jnp.zeros((bsz, lp, lay.np - lay.used), BF16)], axis=-1)
        grads["w_in"][l] = matmul(flat(s["hn"]), flat(dproj), ta=True, name=nm("mmg_in"))
        dhn = unflat(matmul(flat(dproj), p["w_in"][l], tb=True, name=nm("mmb_in")))
        (dh,), (dg_mix,) = rowwise_vjp(nm("rms_mix_b"), f_rms, [Row(s["h_in"], d)], [], [s["g_mix"]], [dhn], 1, tmw,
                                       adds={0: dh_mid})
        grads["norm_mix"][l] = dg_mix[0]

    out = {k: jnp.stack(v) for k, v in grads.items()}
    out["norm_final"] = d_final[0]
    grad_x = dh[:, CHUNK:]
    grad_meta = jnp.sum(dh[:, N_PAD:CHUNK], axis=0)
    return loss_row, grad_x, grad_meta, out


BIG = ("w_in", "w_up_a", "w_up_b", "w_out", "w_ffn_in", "w_ffn_out")
BIG_AXES = (2, 1, 1, 1, 2, 1)
SMALL_SHARDED = ("meta_tokens", "conv_w")
SMALL_AXES = (1, 2)
REPLICATED = ("norm_mix", "gdn_a_log", "gdn_dt_bias", "gdn_norm", "ret_decay_logit", "norm_ffn", "norm_final")
WEIGHTS = ("meta_tokens", "norm_mix", "w_in", "conv_w", "gdn_a_log", "gdn_dt_bias", "gdn_norm", "ret_decay_logit",
           "w_up_a", "w_up_b", "w_out", "norm_ffn", "w_ffn_in", "w_ffn_out", "norm_final")


def kernel(x, meta_tokens, norm_mix, w_in, conv_w, gdn_a_log, gdn_dt_bias, gdn_norm, ret_decay_logit, w_up_a, w_up_b, w_out, norm_ffn, w_ffn_in, w_ffn_out, norm_final, loss_target, m_meta_tokens, m_norm_mix, m_w_in, m_conv_w, m_gdn_a_log, m_gdn_dt_bias, m_gdn_norm, m_ret_decay_logit, m_w_up_a, m_w_up_b, m_w_out, m_norm_ffn, m_w_ffn_in, m_w_ffn_out, m_norm_final, v_meta_tokens, v_norm_mix, v_w_in, v_conv_w, v_gdn_a_log, v_gdn_dt_bias, v_gdn_norm, v_ret_decay_logit, v_w_up_a, v_w_up_b, v_w_out, v_norm_ffn, v_w_ffn_in, v_w_ffn_out, v_norm_final):
    w = dict(meta_tokens=meta_tokens, norm_mix=norm_mix, w_in=w_in, conv_w=conv_w, gdn_a_log=gdn_a_log,
             gdn_dt_bias=gdn_dt_bias, gdn_norm=gdn_norm, ret_decay_logit=ret_decay_logit, w_up_a=w_up_a,
             w_up_b=w_up_b, w_out=w_out, norm_ffn=norm_ffn, w_ffn_in=w_ffn_in, w_ffn_out=w_ffn_out,
             norm_final=norm_final)
    m = dict(meta_tokens=m_meta_tokens, norm_mix=m_norm_mix, w_in=m_w_in, conv_w=m_conv_w, gdn_a_log=m_gdn_a_log,
             gdn_dt_bias=m_gdn_dt_bias, gdn_norm=m_gdn_norm, ret_decay_logit=m_ret_decay_logit, w_up_a=m_w_up_a,
             w_up_b=m_w_up_b, w_out=m_w_out, norm_ffn=m_norm_ffn, w_ffn_in=m_w_ffn_in, w_ffn_out=m_w_ffn_out,
             norm_final=m_norm_final)
    v = dict(meta_tokens=v_meta_tokens, norm_mix=v_norm_mix, w_in=v_w_in, conv_w=v_conv_w, gdn_a_log=v_gdn_a_log,
             gdn_dt_bias=v_gdn_dt_bias, gdn_norm=v_gdn_norm, ret_decay_logit=v_ret_decay_logit, w_up_a=v_w_up_a,
             w_up_b=v_w_up_b, w_out=v_w_out, norm_ffn=v_norm_ffn, w_ffn_in=v_w_ffn_in, w_ffn_out=v_w_ffn_out,
             norm_final=v_norm_final)
    d = x.shape[-1]
    lay = Layout(d)
    sharded = BIG + SMALL_SHARDED
    sharded_axes = BIG_AXES + SMALL_AXES

    big_shapes = [w[k].shape for k in BIG]
    gathered = all_gather_hbm(_pack([w[k].astype(BF16) for k in BIG], 16), "gather_weights")
    full = dict(zip(BIG, _gathered_full(gathered, big_shapes, BIG_AXES)))
    small_shapes = [w[k].shape for k in SMALL_SHARDED]
    gathered_s = all_gather_small(_pack([w[k] for k in SMALL_SHARDED], 8), "gather_small")
    full.update(zip(SMALL_SHARDED, _gathered_full(gathered_s, small_shapes, SMALL_AXES)))
    p = {k: w[k] for k in REPLICATED}
    p.update({k: full[k] for k in BIG + ("conv_w",)})
    p["w_in"] = jnp.stack([lay.relayout_w_in(full["w_in"][l]) for l in range(full["w_in"].shape[0])])

    loss_row, grad_x, grad_meta, g = local_step(x, loss_target, full["meta_tokens"], p)
    g["meta_tokens"] = grad_meta
    g["w_in"] = jnp.stack([lay.unlayout_w_in(g["w_in"][l]) for l in range(g["w_in"].shape[0])])

    pieces = all_to_all_hbm(_pieces_by_dest([g[k].astype(BF16) for k in sharded], sharded_axes, 1024),
                            "exchange_grads")
    local_shapes = [w[k].shape for k in sharded]
    res = adamw(_pack([w[k] for k in sharded], 1024), _pack([m[k] for k in sharded], 1024),
                _pack([v[k] for k in sharded], 1024), pieces, "adamw_sharded")
    outs = {kind: dict(zip(sharded, _unpack(r, local_shapes))) for kind, r in zip(("g", "d", "m", "v"), res)}

    rep_shapes = [w[k].shape for k in REPLICATED]
    part = _pack([g[k] for k in REPLICATED] + [loss_row[0, :1]], 8)
    parts = all_gather_small(part, "gather_replicated")
    pad1 = lambda a: _pack([a[k] for k in REPLICATED] + [jnp.zeros((1,), F32)], 8)
    res_r = adamw(pad1(w), pad1(m), pad1(v), parts, "adamw_replicated")
    for kind, r in zip(("g", "d", "m", "v"), res_r):
        outs[kind].update(zip(REPLICATED, _unpack(r, rep_shapes)))
    loss = res_r[0].reshape(-1)[sum(math.prod(s) for s in rep_shapes)]

    return (loss, grad_x, *[outs["g"][k] for k in WEIGHTS], *[outs["d"][k] for k in WEIGHTS],
            *[outs["m"][k] for k in WEIGHTS], *[outs["v"][k] for k in WEIGHTS])
```

```python
import functools
import math

import jax
import jax.numpy as jnp
from jax import lax
from jax.experimental import pallas as pl
from jax.experimental.pallas import tpu as pltpu

F32 = jnp.float32
BF16 = jnp.bfloat16
HIGHEST = lax.Precision.HIGHEST

LANES = 128
CHUNK = 64
N_META = 16
N_PAD = CHUNK - N_META
CONV_K = 5
EPS = 1e-6
ROPE_BASE = 10000.0
N_DEV = 8
VMEM_LIMIT = 56 * 1024 * 1024
MATMUL_VMEM = 40 * 1024 * 1024

ADAM_LR, ADAM_B1, ADAM_B2, ADAM_EPS, ADAM_WD, ADAM_STEP = 0.001, 0.9, 0.999, 1e-08, 0.01, 10


def _tile(n, cap, mult):
    if n <= cap:
        return n
    best = None
    for t in range(mult, cap + 1, mult):
        if n % t == 0:
            best = t
    assert best is not None, (n, cap, mult)
    return best


def _params(sem):
    return pltpu.CompilerParams(dimension_semantics=sem, vmem_limit_bytes=VMEM_LIMIT)


def _raw_dot(a, b, ca, cb, exact):
    dims = (((ca,), (cb,)), ((), ()))
    a_hi, b_hi = a.astype(BF16), b.astype(BF16)
    out = lax.dot_general(a_hi, b_hi, dims, preferred_element_type=F32)
    if exact:
        a_lo = (a - a_hi.astype(F32)).astype(BF16)
        b_lo = (b - b_hi.astype(F32)).astype(BF16)
        out = out + lax.dot_general(a_hi, b_lo, dims, preferred_element_type=F32)
        out = out + lax.dot_general(a_lo, b_hi, dims, preferred_element_type=F32)
    return out


@functools.partial(jax.custom_vjp, nondiff_argnums=(2, 3, 4))
def _dot(a, b, ca, cb, exact):
    return _raw_dot(a, b, ca, cb, exact)


def _dot_fwd(a, b, ca, cb, exact):
    return _raw_dot(a, b, ca, cb, exact), (a, b)


def _dot_bwd(ca, cb, exact, res, g):
    a, b = res
    if ca == 1:
        da = _raw_dot(g, b, 1, 1 if cb == 0 else 0, exact)
    else:
        da = _raw_dot(b, g, 1 if cb == 0 else 0, 1, exact)
    if cb == 0:
        db = _raw_dot(a, g, 0 if ca == 1 else 1, 0, exact)
    else:
        db = _raw_dot(g, a, 0, 0 if ca == 1 else 1, exact)
    return da, db


_dot.defvjp(_dot_fwd, _dot_bwd)


@functools.partial(jax.custom_vjp, nondiff_argnums=(1, 2))
def _split(x, n, axis):
    return lax.slice_in_dim(x, 0, n, axis=axis), lax.slice_in_dim(x, n, x.shape[axis], axis=axis)


_split.defvjp(lambda x, n, axis: (_split(x, n, axis), None),
              lambda n, axis, _, g: (jnp.concatenate([g[0], g[1]], axis=axis),))


def _bdot(a, b, ca=1, cb=0):
    return _dot(a, b, ca, cb, False)


def _xdot(a, b, ca=1, cb=0):
    return _dot(a, b, ca, cb, True)


def matmul(a, b, *, ta=False, tb=False, add=None, name):
    m, k = (a.shape[1], a.shape[0]) if ta else a.shape
    k2, n = (b.shape[1], b.shape[0]) if tb else b.shape
    assert k == k2, (a.shape, b.shape, ta, tb)
    has_add = add is not None
    tm_cap, tn_cap = 2080, 1408
    while True:
        tm = _tile(m, tm_cap, 128 if ta else 16)
        tn = _tile(n, tn_cap, 128)
        tk = _tile(k, 1664, 128 if (not ta or tb) else 16)
        nk = k // tk
        need = 2 * (tm * tk * a.dtype.itemsize + tk * tn * b.dtype.itemsize + tm * tn * 4 * (2 if has_add else 1))
        need += tm * tn * 4 if nk > 1 else 0
        if need <= MATMUL_VMEM or (tm_cap <= 256 and tn_cap <= 256):
            break
        if tm_cap >= tn_cap:
            tm_cap //= 2
        else:
            tn_cap //= 2
    ca, cb = (0 if ta else 1), (1 if tb else 0)
    a_spec = pl.BlockSpec((tk, tm), lambda i, j, kk: (kk, i)) if ta else pl.BlockSpec((tm, tk), lambda i, j, kk: (i, kk))
    b_spec = pl.BlockSpec((tn, tk), lambda i, j, kk: (j, kk)) if tb else pl.BlockSpec((tk, tn), lambda i, j, kk: (kk, j))
    o_spec = pl.BlockSpec((tm, tn), lambda i, j, kk: (i, j))

    def body(*refs):
        a_ref, b_ref = refs[0], refs[1]
        add_ref = refs[2] if has_add else None
        o_ref = refs[3] if has_add else refs[2]
        def part():
            return _raw_dot(a_ref[...], b_ref[...], ca, cb, False)

        if nk == 1:
            o_ref[...] = part() + add_ref[...] if has_add else part()
            return
        acc_ref = refs[-1]
        kk = pl.program_id(2)

        @pl.when(kk == 0)
        def _():
            acc_ref[...] = part()

        @pl.when(jnp.logical_and(kk > 0, kk < nk - 1))
        def _():
            acc_ref[...] += part()

        @pl.when(kk == nk - 1)
        def _():
            o_ref[...] = acc_ref[...] + part() + add_ref[...] if has_add else acc_ref[...] + part()

    ins = [a, b] + ([add] if has_add else [])
    in_specs = [a_spec, b_spec] + ([o_spec] if has_add else [])
    return pl.pallas_call(
        body, name=name, grid=(m // tm, n // tn, nk), in_specs=in_specs, out_specs=o_spec,
        out_shape=jax.ShapeDtypeStruct((m, n), F32),
        scratch_shapes=[pltpu.VMEM((tm, tn), F32)] if nk > 1 else [],
        compiler_params=_params(("parallel", "parallel", "arbitrary")),
    )(*ins)


class Row:
    def __init__(self, arr, bc, off=0, per_head=True, diff=True):
        self.arr, self.bc, self.off, self.per_head, self.diff = arr, bc, off, per_head, diff


class Tab:
    def __init__(self, arr):
        self.arr = arr


def _row_specs(rows, tabs, pars, tm):
    specs = []
    for r in rows:
        specs.append(pl.BlockSpec((1, tm, r.bc), functools.partial(
            lambda b, i, h, off, ph: (b, i, off + (h if ph else 0)), off=r.off, ph=r.per_head)))
    for t in tabs:
        specs.append(pl.BlockSpec((tm, t.arr.shape[1]), lambda b, i, h: (i, 0)))
    for p in pars:
        specs.append(pl.BlockSpec(p.shape, lambda b, i, h: (0, 0)))
    return specs


def rowwise(name, fn, rows, tabs, pars, outs, nh, tm, out_dtype=F32):
    bsz, lp = rows[0].arr.shape[:2]
    nr, nt, npar = len(rows), len(tabs), len(pars)

    def body(*refs):
        t0 = pl.program_id(1) * tm
        ins = [refs[k][0] for k in range(nr)] + [refs[nr + k][...] for k in range(nt + npar)]
        res = fn(t0, *ins)
        for o_ref, o in zip(refs[nr + nt + npar:], res):
            o_ref[0] = o.astype(o_ref.dtype)

    return pl.pallas_call(
        body, name=name, grid=(bsz, lp // tm, nh),
        in_specs=_row_specs(rows, tabs, pars, tm),
        out_specs=[pl.BlockSpec((1, tm, bc), lambda b, i, h: (b, i, h)) for _, bc in outs],
        out_shape=[jax.ShapeDtypeStruct((bsz, lp, c), out_dtype) for c, _ in outs],
        compiler_params=_params(("parallel", "parallel", "parallel")),
    )(*[r.arr for r in rows], *[t.arr for t in tabs], *pars)


def rowwise_vjp(name, fn, rows, tabs, pars, couts, nh, tm, adds=None, narrow=()):
    bsz, lp = rows[0].arr.shape[:2]
    nr, nt, npar, nco = len(rows), len(tabs), len(pars), len(couts)
    adds = adds or {}
    add_keys = sorted(adds)
    diff_idx = [k for k, r in enumerate(rows) if r.diff]
    for k in diff_idx:
        assert rows[k].per_head or nh == 1

    def body(*refs):
        b, i, h = pl.program_id(0), pl.program_id(1), pl.program_id(2)
        t0 = i * tm
        pos = 0
        row_v = [refs[k][0] for k in range(nr)]
        pos += nr
        tab_v = [refs[pos + k][...] for k in range(nt)]
        pos += nt
        par_v = [refs[pos + k][...] for k in range(npar)]
        pos += npar
        co_v = [refs[pos + k][0] for k in range(nco)]
        pos += nco
        add_v = {key: refs[pos + k][0] for k, key in enumerate(add_keys)}
        pos += len(add_keys)
        drow_refs = refs[pos:pos + len(diff_idx)]
        dpar_refs = refs[pos + len(diff_idx):]

        def f(dvals, pvals):
            full = list(row_v)
            for k, v in zip(diff_idx, dvals):
                full[k] = v
            return tuple(fn(t0, *full, *tab_v, *pvals))

        _, pull = jax.vjp(f, [row_v[k] for k in diff_idx], par_v)
        d_rows, d_pars = pull(tuple(co_v))
        for ref, k, d in zip(drow_refs, diff_idx, d_rows):
            ref[0] = (d + add_v[k] if k in add_v else d).astype(ref.dtype)
        first = jnp.logical_and(jnp.logical_and(b == 0, i == 0), h == 0)
        for ref, d in zip(dpar_refs, d_pars):
            @pl.when(first)
            def _(ref=ref, d=d):
                ref[...] = d

            @pl.when(jnp.logical_not(first))
            def _(ref=ref, d=d):
                ref[...] += d

    out_block = lambda bc: pl.BlockSpec((1, tm, bc), lambda b, i, h: (b, i, h))
    in_specs = _row_specs(rows, tabs, pars, tm)
    in_specs += [out_block(c.shape[2] // nh) for c in couts]
    in_specs += [out_block(rows[k].bc) for k in add_keys]
    out_specs = [out_block(rows[k].bc) for k in diff_idx]
    out_specs += [pl.BlockSpec(p.shape, lambda b, i, h: (0, 0)) for p in pars]
    out_shape = [jax.ShapeDtypeStruct((bsz, lp, nh * rows[k].bc), BF16 if k in narrow else F32) for k in diff_idx]
    out_shape += [jax.ShapeDtypeStruct(p.shape, F32) for p in pars]
    res = pl.pallas_call(
        body, name=name, grid=(bsz, lp // tm, nh), in_specs=in_specs, out_specs=out_specs, out_shape=out_shape,
        compiler_params=_params(("arbitrary", "arbitrary", "arbitrary")),
    )(*[r.arr for r in rows], *[t.arr for t in tabs], *pars, *couts, *[adds[k] for k in add_keys])
    return res[:len(diff_idx)], res[len(diff_idx):]


def _real_rows(t0, tm):
    return (t0 + lax.broadcasted_iota(jnp.int32, (tm, 1), 0)) >= N_PAD


def f_rms(t0, x, gain):
    return (x * lax.rsqrt(jnp.mean(x * x, axis=-1, keepdims=True) + EPS) * gain,)


def make_f_gb(heads):
    def f_gb(t0, ab, alog, dtb):
        lane = lax.broadcasted_iota(jnp.int32, ab.shape, 1)
        g = -jnp.exp(alog) * jax.nn.softplus(ab + dtb)
        beta = jax.nn.sigmoid(ab)
        out = jnp.where(lane < 2 * heads, g, jnp.where(lane < 4 * heads, beta, 0.0))
        return (jnp.where(_real_rows(t0, ab.shape[0]), out, 0.0),)
    return f_gb


def f_gdn_out(t0, o, z, gain):
    on = o * lax.rsqrt(jnp.mean(o * o, axis=-1, keepdims=True) + EPS)
    return (on * gain * jax.nn.silu(z),)


def f_ret_out(t0, o, g):
    on = o * lax.rsqrt(jnp.mean(o * o, axis=-1, keepdims=True) + EPS)
    return (on * jax.nn.silu(g),)


def f_merge(t0, ga, gb, ya, yb):
    return (jax.nn.sigmoid(ga) * ya + jax.nn.sigmoid(gb) * yb,)


def f_swiglu(t0, x):
    gate, up = _split(x, x.shape[1] // 2, 1)
    return (jax.nn.silu(gate) * up,)


def make_f_rope(sign):
    def f_rope(t0, x, cos, sin):
        half = x.shape[1] // 2
        x1, x2 = x[:, :half], x[:, half:]
        s = sin * sign
        return (jnp.concatenate([x1 * cos - x2 * s, x1 * s + x2 * cos], axis=1),)
    return f_rope


def loss_head(h, gain, target, tm):
    bsz, lp, d = h.shape

    def body(h_ref, g_ref, t_ref, dh_ref, dg_ref, loss_ref):
        b, i = pl.program_id(0), pl.program_id(1)
        rows = (i * tm + lax.broadcasted_iota(jnp.int32, (tm, 1), 0)) >= CHUNK
        tgt = t_ref[0]

        def f(x, gain_v):
            y = f_rms(0, x, gain_v)[0]
            err = jnp.where(rows, y - tgt, 0.0)
            return 0.5 * jnp.sum(jnp.mean(err * err, axis=-1, keepdims=True), keepdims=True)

        val, pull = jax.vjp(f, h_ref[0], g_ref[...])
        dh, dg = pull(jnp.ones((1, 1), F32))
        dh_ref[0] = dh
        first = jnp.logical_and(b == 0, i == 0)
        val_row = jnp.broadcast_to(val, (1, LANES))

        @pl.when(first)
        def _():
            dg_ref[...] = dg
            loss_ref[...] = val_row

        @pl.when(jnp.logical_not(first))
        def _():
            dg_ref[...] += dg
            loss_ref[...] += val_row

    blk = pl.BlockSpec((1, tm, d), lambda b, i: (b, i, 0))
    return pl.pallas_call(
        body, name="loss_head", grid=(bsz, lp // tm),
        in_specs=[blk, pl.BlockSpec((1, d), lambda b, i: (0, 0)), blk],
        out_specs=[blk, pl.BlockSpec((1, d), lambda b, i: (0, 0)), pl.BlockSpec((1, LANES), lambda b, i: (0, 0))],
        out_shape=[jax.ShapeDtypeStruct((bsz, lp, d), F32), jax.ShapeDtypeStruct((1, d), F32),
                   jax.ShapeDtypeStruct((1, LANES), F32)],
        compiler_params=_params(("arbitrary", "arbitrary")),
    )(h, gain, target)


def _conv_pre(x, w_ref):
    lp = x.shape[0]
    acc = w_ref[2:3, :] * x
    for k in (0, 1, 3, 4):
        acc = acc + w_ref[k:k + 1, :] * pltpu.roll(x, (2 - k) % lp, 0)
    return acc


def conv_fwd(proj, off, w, l2, name):
    bsz, lp, _ = proj.shape
    d = w.shape[1]

    def body(x_ref, w_ref, o_ref):
        x = x_ref[0]
        s = jnp.where(_real_rows(0, lp), jax.nn.silu(_conv_pre(x, w_ref)), 0.0)
        if l2:
            s = s * lax.rsqrt(jnp.sum(s * s, axis=-1, keepdims=True) + EPS)
        o_ref[0] = s

    return pl.pallas_call(
        body, name=name, grid=(d // LANES, bsz),
        in_specs=[pl.BlockSpec((1, lp, LANES), lambda j, b: (b, 0, off + j)),
                  pl.BlockSpec((CONV_K, LANES), lambda j, b: (0, j))],
        out_specs=pl.BlockSpec((1, lp, LANES), lambda j, b: (b, 0, j)),
        out_shape=jax.ShapeDtypeStruct((bsz, lp, d), F32),
        compiler_params=_params(("parallel", "parallel")),
    )(proj, w)


def conv_bwd(proj, off, w, dy, l2, name):
    bsz, lp, _ = proj.shape
    d = w.shape[1]

    def body(x_ref, w_ref, dy_ref, dx_ref, dw_ref):
        b = pl.program_id(1)
        x, g = x_ref[0], dy_ref[0]
        real = _real_rows(0, lp)
        c = _conv_pre(x, w_ref)
        sg = jax.nn.sigmoid(c)
        s = jnp.where(real, c * sg, 0.0)
        if l2:
            r = lax.rsqrt(jnp.sum(s * s, axis=-1, keepdims=True) + EPS)
            g = r * g - s * (r * r * r) * jnp.sum(g * s, axis=-1, keepdims=True)
        dc = jnp.where(real, g * (sg * (1.0 + c * (1.0 - sg))), 0.0)
        dx = w_ref[2:3, :] * dc
        for k in (0, 1, 3, 4):
            dx = dx + w_ref[k:k + 1, :] * pltpu.roll(dc, (k - 2) % lp, 0)
        dx_ref[0] = jnp.where(real, dx, 0.0).astype(dx_ref.dtype)
        tap_row = lax.broadcasted_iota(jnp.int32, (CONV_K, LANES), 0)
        dw = jnp.zeros((CONV_K, LANES), F32)
        for k in range(CONV_K):
            xs = x if k == 2 else pltpu.roll(x, (2 - k) % lp, 0)
            dw = dw + jnp.where(tap_row == k, jnp.sum(dc * xs, axis=0, keepdims=True), 0.0)

        @pl.when(b == 0)
        def _():
            dw_ref[...] = dw

        @pl.when(b > 0)
        def _():
            dw_ref[...] += dw

    blk = pl.BlockSpec((1, lp, LANES), lambda j, b: (b, 0, j))
    return pl.pallas_call(
        body, name=name, grid=(d // LANES, bsz),
        in_specs=[pl.BlockSpec((1, lp, LANES), lambda j, b: (b, 0, off + j)),
                  pl.BlockSpec((CONV_K, LANES), lambda j, b: (0, j)), blk],
        out_specs=[blk, pl.BlockSpec((CONV_K, LANES), lambda j, b: (0, j))],
        out_shape=[jax.ShapeDtypeStruct((bsz, lp, d), BF16), jax.ShapeDtypeStruct((CONV_K, d), F32)],
        compiler_params=_params(("parallel", "arbitrary")),
    )(proj, w, dy)


def _tri_masks(rev):
    ii = lax.broadcasted_iota(jnp.int32, (CHUNK, CHUNK), 0)
    jj = lax.broadcasted_iota(jnp.int32, (CHUNK, CHUNK), 1)
    return ((ii <= jj), (ii < jj)) if rev else ((ii >= jj), (ii > jj))


def _lane_pick(block, lane):
    sel = lax.broadcasted_iota(jnp.int32, block.shape, 1) == lane
    return jnp.sum(jnp.where(sel, block, 0.0), axis=1, keepdims=True)


def _cumsum_impl(x, rev):
    n = x.shape[0]
    row = lax.broadcasted_iota(jnp.int32, x.shape, 0)
    step = 1
    while step < n:
        if rev:
            x = x + jnp.where(row < n - step, pltpu.roll(x, n - step, 0), 0.0)
        else:
            x = x + jnp.where(row >= step, pltpu.roll(x, step, 0), 0.0)
        step *= 2
    return x


@functools.partial(jax.custom_vjp, nondiff_argnums=(1,))
def _cumsum_rows(x, rev):
    return _cumsum_impl(x, rev)


_cumsum_rows.defvjp(lambda x, rev: (_cumsum_impl(x, rev), None),
                    lambda rev, _, g: (_cumsum_impl(g, not rev),))


def _unit_inv_impl(m):
    n = m.shape[0]
    eye = (lax.broadcasted_iota(jnp.int32, (n, n), 0) == lax.broadcasted_iota(jnp.int32, (n, n), 1)).astype(F32)
    p = -m
    inv = eye + p
    step = 2
    while step < n:
        p = _raw_dot(p, p, 1, 0, True)
        inv = inv + _raw_dot(inv, p, 1, 0, True)
        step *= 2
    return inv


@jax.custom_vjp
def _unit_inv(m):
    return _unit_inv_impl(m)


def _unit_inv_fwd(m):
    inv = _unit_inv_impl(m)
    return inv, inv


def _unit_inv_bwd(inv, g):
    return (-_raw_dot(_raw_dot(inv, g, 0, 0, True), inv, 1, 1, True),)


_unit_inv.defvjp(_unit_inv_fwd, _unit_inv_bwd)


GROUP = 2


def _unit_inv_all_impl(ms):
    n = ms[0].shape[0]
    width = GROUP * n
    dims = (((1,), (0,)), ((), ()))
    lane = lax.broadcasted_iota(jnp.int32, (n, width), 1)
    row = lax.broadcasted_iota(jnp.int32, (n, width), 0)
    blocks = [jnp.logical_and(lane >= t * n, lane < (t + 1) * n) for t in range(GROUP)]

    def halves(x):
        hi = x.astype(BF16)
        return hi, (x - hi.astype(F32)).astype(BF16)

    def xdot(a, b):
        diag = jnp.concatenate([jnp.where(blk, b, 0.0) for blk in blocks], axis=0)
        a_hi, a_lo = halves(a)
        d_hi, d_lo = halves(diag)
        out = lax.dot_general(a_hi, d_hi, dims, preferred_element_type=F32)
        out = out + lax.dot_general(a_hi, d_lo, dims, preferred_element_type=F32)
        return out + lax.dot_general(a_lo, d_hi, dims, preferred_element_type=F32)

    eye = (row == lane % n).astype(F32)
    groups = range(len(ms) // GROUP)
    p = [-jnp.concatenate(ms[t * GROUP:(t + 1) * GROUP], axis=1) for t in groups]
    inv = [eye + p[t] for t in groups]
    step = 2
    while step < n:
        p = [xdot(p[t], p[t]) for t in groups]
        inv = [inv[t] + xdot(inv[t], p[t]) for t in groups]
        step *= 2
    return [inv[t][:, u * n:(u + 1) * n] for t in groups for u in range(GROUP)]


@jax.custom_vjp
def _unit_inv_all(ms):
    return _unit_inv_all_impl(ms)


def _unit_inv_all_fwd(ms):
    out = _unit_inv_all_impl(ms)
    return out, out


_unit_inv_all.defvjp(_unit_inv_all_fwd, lambda invs, g: ([_unit_inv_bwd(i, gi)[0] for i, gi in zip(invs, g)],))


@jax.custom_vjp
def _unit_inv_known(m, inv):
    return inv


_unit_inv_known.defvjp(lambda m, inv: (inv, inv),
                       lambda inv, g: (_unit_inv_bwd(inv, g)[0], jnp.zeros_like(inv)))


def _gdn_chunk(qs, ks, vs, gb, ss, g_lanes, b_lanes, rev, invs=None, want_inv=False):
    nh = len(qs)
    dk = qs[0].shape[1]
    incl, strict = _tri_masks(rev)
    hs = range(nh)
    g = [_lane_pick(gb, l) for l in g_lanes]
    beta = [_lane_pick(gb, l) for l in b_lanes]
    qs = [q * (dk ** -0.5) for q in qs]
    gc_sq = [_cumsum_rows(jnp.broadcast_to(g[h], (CHUNK, CHUNK)), rev) for h in hs]
    gc = [_cumsum_rows(jnp.broadcast_to(g[h], (CHUNK, dk)), rev) for h in hs]
    g_last = [jnp.sum(g[h], axis=0, keepdims=True) for h in hs]
    decay = [jnp.where(incl, jnp.exp(jnp.where(incl, gc_sq[h] - gc_sq[h].T, 0.0)), 0.0) for h in hs]
    kb = [ks[h] * beta[h] for h in hs]
    kk = [_split(_bdot(jnp.concatenate([kb[h], qs[h]], axis=0), ks[h], 1, 1), CHUNK, 0) for h in hs]
    m = [jnp.where(strict, kk[h][0] * decay[h], 0.0) for h in hs]
    qk = [kk[h][1] * decay[h] for h in hs]
    if invs is not None:
        inv = [_unit_inv_known(m[h], invs[h]) for h in hs]
    elif nh % GROUP == 0:
        inv = _unit_inv_all(m)
    else:
        inv = [_unit_inv(m[h]) for h in hs]
    e_gc = [jnp.exp(gc[h]) for h in hs]
    uw = [_split(_bdot(inv[h], jnp.concatenate([vs[h] * beta[h], kb[h] * e_gc[h]], axis=1)), vs[h].shape[1], 1)
          for h in hs]
    u = [uw[h][0] for h in hs]
    w = [uw[h][1] for h in hs]
    q_dec = [qs[h] * e_gc[h] for h in hs]
    k_dec = [ks[h] * jnp.exp(g_last[h] - gc[h]) for h in hs]
    ws = [_split(_bdot(jnp.concatenate([w[h], q_dec[h]], axis=0), ss[h]), CHUNK, 0) for h in hs]
    v_new = [u[h] - ws[h][0] for h in hs]
    o = [ws[h][1] + _bdot(qk[h], v_new[h]) for h in hs]
    s_new = [ss[h] * jnp.exp(g_last[h]) + _bdot(k_dec[h], v_new[h], 0, 0) for h in hs]
    return (o, s_new, inv) if want_inv else (o, s_new)


def gdn_fwd(q, k, v, gb, heads, direction, o_prev, name):
    bsz, lp, d = q.shape
    nc = lp // CHUNK
    rev = direction == 1
    cm = (lambda n: nc - 1 - n) if rev else (lambda n: n)
    has_prev = o_prev is not None

    def body(*refs):
        q_ref, k_ref, v_ref, gb_ref = refs[:4]
        prev_ref = refs[4] if has_prev else None
        o_ref, st_ref, inv_ref, s_ref = refs[-4:]
        n = pl.program_id(1)

        @pl.when(n == 0)
        def _():
            s_ref[...] = jnp.zeros_like(s_ref)

        sls = [slice(h * LANES, (h + 1) * LANES) for h in range(heads)]
        ss = [s_ref[h] for h in range(heads)]
        for h in range(heads):
            st_ref[0, 0, h] = ss[h]
        os_, s_new, inv = _gdn_chunk([q_ref[0, :, sl] for sl in sls], [k_ref[0, :, sl] for sl in sls],
                                     [v_ref[0, :, sl] for sl in sls], gb_ref[0], ss,
                                     [direction * heads + h for h in range(heads)],
                                     [2 * heads + direction * heads + h for h in range(heads)], rev, want_inv=True)
        for h, sl in enumerate(sls):
            s_ref[h] = s_new[h]
            inv_ref[0, 0, h] = inv[h]
            o_ref[0, :, sl] = os_[h] + prev_ref[0, :, sl] if has_prev else os_[h]

    blk = pl.BlockSpec((1, CHUNK, d), lambda b, n: (b, cm(n), 0))
    gblk = pl.BlockSpec((1, CHUNK, LANES), lambda b, n: (b, cm(n), 0))
    st_blk = pl.BlockSpec((1, 1, heads, LANES, LANES), lambda b, n: (b, cm(n), 0, 0, 0))
    inv_blk = pl.BlockSpec((1, 1, heads, CHUNK, CHUNK), lambda b, n: (b, cm(n), 0, 0, 0))
    return pl.pallas_call(
        body, name=name, grid=(bsz, nc),
        in_specs=[blk, blk, blk, gblk] + ([blk] if has_prev else []),
        out_specs=[blk, st_blk, inv_blk],
        out_shape=[jax.ShapeDtypeStruct((bsz, lp, d), F32),
                   jax.ShapeDtypeStruct((bsz, nc, heads, LANES, LANES), F32),
                   jax.ShapeDtypeStruct((bsz, nc, heads, CHUNK, CHUNK), F32)],
        scratch_shapes=[pltpu.VMEM((heads, LANES, LANES), F32)],
        compiler_params=_params(("parallel", "arbitrary")),
    )(q, k, v, gb, *([o_prev] if has_prev else []))


def gdn_bwd(q, k, v, gb, states, invs, do, heads, direction, prev, name):
    bsz, lp, d = q.shape
    nc = lp // CHUNK
    rev = direction == 1
    cm = (lambda n: n) if rev else (lambda n: nc - 1 - n)
    has_prev = prev is not None

    def body(*refs):
        q_ref, k_ref, v_ref, gb_ref, st_ref, inv_ref, do_ref = refs[:7]
        prev_refs = refs[7:11] if has_prev else None
        dq_ref, dk_ref, dv_ref, dgb_ref, ds_ref = refs[-5:]
        n = pl.program_id(1)

        @pl.when(n == 0)
        def _():
            ds_ref[...] = jnp.zeros_like(ds_ref)

        sls = [slice(h * LANES, (h + 1) * LANES) for h in range(heads)]
        f = functools.partial(_gdn_chunk, g_lanes=[direction * heads + h for h in range(heads)],
                              b_lanes=[2 * heads + direction * heads + h for h in range(heads)], rev=rev,
                              invs=[inv_ref[0, 0, h] for h in range(heads)])
        _, pull = jax.vjp(f, [q_ref[0, :, sl] for sl in sls], [k_ref[0, :, sl] for sl in sls],
                          [v_ref[0, :, sl] for sl in sls], gb_ref[0], [st_ref[0, 0, h] for h in range(heads)])
        dq, dk, dv, dgb, ds = pull(([do_ref[0, :, sl] for sl in sls], [ds_ref[h] for h in range(heads)]))
        for h, sl in enumerate(sls):
            ds_ref[h] = ds[h]
            if has_prev:
                dq[h], dk[h], dv[h] = (dq[h] + prev_refs[0][0, :, sl], dk[h] + prev_refs[1][0, :, sl],
                                       dv[h] + prev_refs[2][0, :, sl])
            dq_ref[0, :, sl] = dq[h]
            dk_ref[0, :, sl] = dk[h]
            dv_ref[0, :, sl] = dv[h]
        dgb_ref[0] = dgb + prev_refs[3][0] if has_prev else dgb

    blk = pl.BlockSpec((1, CHUNK, d), lambda b, n: (b, cm(n), 0))
    gblk = pl.BlockSpec((1, CHUNK, LANES), lambda b, n: (b, cm(n), 0))
    st_blk = pl.BlockSpec((1, 1, heads, LANES, LANES), lambda b, n: (b, cm(n), 0, 0, 0))
    inv_blk = pl.BlockSpec((1, 1, heads, CHUNK, CHUNK), lambda b, n: (b, cm(n), 0, 0, 0))
    big = jax.ShapeDtypeStruct((bsz, lp, d), F32)
    return pl.pallas_call(
        body, name=name, grid=(bsz, nc),
        in_specs=[blk, blk, blk, gblk, st_blk, inv_blk, blk] + ([blk, blk, blk, gblk] if has_prev else []),
        out_specs=[blk, blk, blk, gblk],
        out_shape=[big, big, big, jax.ShapeDtypeStruct((bsz, lp, LANES), F32)],
        scratch_shapes=[pltpu.VMEM((heads, LANES, LANES), F32)],
        compiler_params=_params(("parallel", "arbitrary")),
    )(q, k, v, gb, states, invs, do, *(list(prev) if has_prev else []))


def _ret_chunk(q, k, v, r, logit, lane, rev):
    dk = q.shape[1]
    lg = jax.nn.log_sigmoid(_lane_pick(logit, lane))
    k = k * (dk ** -0.5)
    ii = lax.broadcasted_iota(jnp.int32, (CHUNK, CHUNK), 0)
    jj = lax.broadcasted_iota(jnp.int32, (CHUNK, CHUNK), 1)
    pos = lax.broadcasted_iota(jnp.int32, (CHUNK, 1), 0)
    if rev:
        incl, rel = ii <= jj, (jj - ii)
        seen = (CHUNK - 1 - pos)
    else:
        incl, rel = ii >= jj, (ii - jj)
        seen = pos
    relf = jnp.where(incl, rel, 0).astype(F32)
    seenf = seen.astype(F32)
    intra = jnp.where(incl, jnp.exp(relf * lg), 0.0)
    qk = _bdot(q, k, 1, 1) * intra
    q_dec = q * jnp.exp(lg * (seenf + 1.0))
    k_dec = k * jnp.exp(lg * (CHUNK - 1.0 - seenf))
    o = _bdot(q_dec, r) + _bdot(qk, v)
    r_new = r * jnp.exp(lg * CHUNK) + _bdot(k_dec, v, 0, 0)
    return o, r_new


def ret_fwd(qk, v_arr, v_off, logit, heads, direction, o_prev, name):
    bsz, lp, d2 = qk.shape
    d = d2 // 2
    dkh, dvh = d // heads, 2 * d // heads
    nc = lp // CHUNK
    rev = direction == 1
    cm = (lambda n: nc - 1 - n) if rev else (lambda n: n)
    has_prev = o_prev is not None
    v_cb = v_off * LANES // (2 * d)
    assert v_cb * 2 * d == v_off * LANES

    def body(*refs):
        q_ref, k_ref, v_ref, lg_ref = refs[:4]
        prev_ref = refs[4] if has_prev else None
        o_ref, st_ref, r_ref = refs[-3], refs[-2], refs[-1]
        n = pl.program_id(1)

        @pl.when(n == 0)
        def _():
            r_ref[...] = jnp.zeros_like(r_ref)

        lgv = lg_ref[...]
        for h in range(heads):
            ks, vs = slice(h * dkh, (h + 1) * dkh), slice(h * dvh, (h + 1) * dvh)
            r = r_ref[h]
            st_ref[0, 0, h] = r
            o, r_new = _ret_chunk(q_ref[0, :, ks], k_ref[0, :, ks], v_ref[0, :, vs], r, lgv, h, rev)
            r_ref[h] = r_new
            o_ref[0, :, vs] = o + prev_ref[0, :, vs] if has_prev else o

    qblk = pl.BlockSpec((1, CHUNK, d), lambda b, n: (b, cm(n), 0))
    kblk = pl.BlockSpec((1, CHUNK, d), lambda b, n: (b, cm(n), 1))
    vblk = pl.BlockSpec((1, CHUNK, 2 * d), lambda b, n: (b, cm(n), v_cb))
    oblk = pl.BlockSpec((1, CHUNK, 2 * d), lambda b, n: (b, cm(n), 0))
    st_blk = pl.BlockSpec((1, 1, heads, dkh, dvh), lambda b, n: (b, cm(n), 0, 0, 0))
    return pl.pallas_call(
        body, name=name, grid=(bsz, nc),
        in_specs=[qblk, kblk, vblk, pl.BlockSpec((1, LANES), lambda b, n: (0, 0))] + ([oblk] if has_prev else []),
        out_specs=[oblk, st_blk],
        out_shape=[jax.ShapeDtypeStruct((bsz, lp, 2 * d), F32),
                   jax.ShapeDtypeStruct((bsz, nc, heads, dkh, dvh), F32)],
        scratch_shapes=[pltpu.VMEM((heads, dkh, dvh), F32)],
        compiler_params=_params(("parallel", "arbitrary")),
    )(qk, qk, v_arr, logit, *([o_prev] if has_prev else []))


def ret_bwd(qk, v_arr, v_off, logit, states, do, heads, direction, prev, name, dv_dtype=F32):
    bsz, lp, d2 = qk.shape
    d = d2 // 2
    dkh, dvh = d // heads, 2 * d // heads
    nc = lp // CHUNK
    rev = direction == 1
    cm = (lambda n: n) if rev else (lambda n: nc - 1 - n)
    has_prev = prev is not None
    v_cb = v_off * LANES // (2 * d)

    def body(*refs):
        q_ref, k_ref, v_ref, lg_ref, st_ref, do_ref = refs[:6]
        prev_refs = refs[6:8] if has_prev else None
        dqk_ref, dv_ref, dlg_ref, dr_ref = refs[-4:]
        b, n = pl.program_id(0), pl.program_id(1)

        @pl.when(n == 0)
        def _():
            dr_ref[...] = jnp.zeros_like(dr_ref)

        lgv = lg_ref[...]
        dlg = jnp.zeros((1, LANES), F32)
        for h in range(heads):
            ks, vs = slice(h * dkh, (h + 1) * dkh), slice(h * dvh, (h + 1) * dvh)
            f = functools.partial(_ret_chunk, lane=h, rev=rev)
            _, pull = jax.vjp(f, q_ref[0, :, ks], k_ref[0, :, ks], v_ref[0, :, vs], st_ref[0, 0, h], lgv)
            dq, dk, dv, dr, dlg_h = pull((do_ref[0, :, vs], dr_ref[h]))
            dr_ref[h] = dr
            dlg = dlg + dlg_h
            kks = slice(d + h * dkh, d + (h + 1) * dkh)
            if has_prev:
                dq, dk, dv = dq + prev_refs[0][0, :, ks], dk + prev_refs[0][0, :, kks], dv + prev_refs[1][0, :, vs]
            dqk_ref[0, :, ks] = dq
            dqk_ref[0, :, kks] = dk
            dv_ref[0, :, vs] = dv.astype(dv_ref.dtype)
        first = jnp.logical_and(b == 0, n == 0)

        @pl.when(first)
        def _():
            dlg_ref[...] = dlg

        @pl.when(jnp.logical_not(first))
        def _():
            dlg_ref[...] += dlg

    qblk = pl.BlockSpec((1, CHUNK, d), lambda b, n: (b, cm(n), 0))
    kblk = pl.BlockSpec((1, CHUNK, d), lambda b, n: (b, cm(n), 1))
    vblk = pl.BlockSpec((1, CHUNK, 2 * d), lambda b, n: (b, cm(n), v_cb))
    oblk = pl.BlockSpec((1, CHUNK, 2 * d), lambda b, n: (b, cm(n), 0))
    lblk = pl.BlockSpec((1, LANES), lambda b, n: (0, 0))
    st_blk = pl.BlockSpec((1, 1, heads, dkh, dvh), lambda b, n: (b, cm(n), 0, 0, 0))
    return pl.pallas_call(
        body, name=name, grid=(bsz, nc),
        in_specs=[qblk, kblk, vblk, lblk, st_blk, oblk] + ([oblk, oblk] if has_prev else []),
        out_specs=[oblk, oblk, lblk],
        out_shape=[jax.ShapeDtypeStruct((bsz, lp, 2 * d), F32), jax.ShapeDtypeStruct((bsz, lp, 2 * d), dv_dtype),
                   jax.ShapeDtypeStruct((1, LANES), F32)],
        scratch_shapes=[pltpu.VMEM((heads, dkh, dvh), F32)],
        compiler_params=_params(("arbitrary", "arbitrary")),
    )(qk, qk, v_arr, logit, states, do, *(list(prev) if has_prev else []))


def _flip(v, bit):
    return 1 - v if bit else v


def _peer(x, y, c, off):
    return (_flip(x, off & 4), _flip(y, off & 2), _flip(c, off & 1))


def all_gather_hbm(block, name):
    r, lanes = block.shape

    def body(x_ref, out_ref, send_sems, recv_sems, local_sem):
        x, y, c = lax.axis_index("x"), lax.axis_index("y"), lax.axis_index("c")
        me, sibling = (x, y, c), (x, y, 1 - c)
        chips = [(1 - x, y), (x, 1 - y), (1 - x, 1 - y)]

        def slot(px, py, pc):
            return out_ref.at[4 * px + 2 * py + pc]

        def copy(k, block_of, to, src=None):
            return pltpu.make_async_remote_copy(
                src_ref=slot(*block_of) if src is None else src, dst_ref=slot(*block_of),
                send_sem=send_sems.at[k], recv_sem=recv_sems.at[k], device_id=to, device_id_type=pl.DeviceIdType.MESH)

        mine = pltpu.make_async_copy(x_ref, slot(*me), local_sem)
        mine.start()
        first = [copy(0, me, sibling, src=x_ref)]
        first += [copy(1 + j, me, (*chip, c), src=x_ref) for j, chip in enumerate(chips)]
        for cp in first:
            cp.start()
        passed = [copy(4 + j, (*chip, c), sibling) for j, chip in enumerate(chips)]
        for j, chip in enumerate(chips):
            copy(1 + j, (*chip, c), me).wait_recv()
            passed[j].start()
        copy(0, sibling, me).wait_recv()
        for j, chip in enumerate(chips):
            copy(4 + j, (*chip, 1 - c), me).wait_recv()
        for cp in first + passed:
            cp.wait_send()
        mine.wait()

    return pl.pallas_call(
        body, name=name, out_shape=jax.ShapeDtypeStruct((N_DEV, r, lanes), block.dtype),
        in_specs=[pl.BlockSpec(memory_space=pl.ANY)], out_specs=pl.BlockSpec(memory_space=pl.ANY),
        scratch_shapes=[pltpu.SemaphoreType.DMA((7,)), pltpu.SemaphoreType.DMA((7,)), pltpu.SemaphoreType.DMA],
    )(block)


def all_gather_small(block, name):
    r, lanes = block.shape

    def body(x_ref, out_ref, send_sems, recv_sems):
        x, y, c = lax.axis_index("x"), lax.axis_index("y"), lax.axis_index("c")
        me = 4 * x + 2 * y + c
        out_ref[me] = x_ref[...]
        copies = []
        for off in range(1, N_DEV):
            copies.append(pltpu.make_async_remote_copy(
                src_ref=x_ref, dst_ref=out_ref.at[me], send_sem=send_sems.at[off - 1], recv_sem=recv_sems.at[off - 1],
                device_id=_peer(x, y, c, off), device_id_type=pl.DeviceIdType.MESH))
        for cp in copies:
            cp.start()
        for cp in copies:
            cp.wait()

    return pl.pallas_call(
        body, name=name, out_shape=jax.ShapeDtypeStruct((N_DEV, r, lanes), block.dtype),
        in_specs=[pl.BlockSpec(memory_space=pltpu.VMEM)], out_specs=pl.BlockSpec(memory_space=pltpu.VMEM),
        scratch_shapes=[pltpu.SemaphoreType.DMA((7,)), pltpu.SemaphoreType.DMA((7,))],
    )(block)


def all_to_all_hbm(pieces, name):
    def body(x_ref, out_ref, send_sems, recv_sems, local_sem):
        x, y, c = lax.axis_index("x"), lax.axis_index("y"), lax.axis_index("c")
        me = 4 * x + 2 * y + c
        mine = pltpu.make_async_copy(x_ref.at[me], out_ref.at[me], local_sem)
        mine.start()
        copies = []
        for off in range(1, N_DEV):
            px, py, pc = _peer(x, y, c, off)
            copies.append(pltpu.make_async_remote_copy(
                src_ref=x_ref.at[4 * px + 2 * py + pc], dst_ref=out_ref.at[me],
                send_sem=send_sems.at[off - 1], recv_sem=recv_sems.at[off - 1],
                device_id=(px, py, pc), device_id_type=pl.DeviceIdType.MESH))
        for cp in copies:
            cp.start()
        for cp in copies:
            cp.wait()
        mine.wait()

    return pl.pallas_call(
        body, name=name, out_shape=jax.ShapeDtypeStruct(pieces.shape, pieces.dtype),
        in_specs=[pl.BlockSpec(memory_space=pl.ANY)], out_specs=pl.BlockSpec(memory_space=pl.ANY),
        scratch_shapes=[pltpu.SemaphoreType.DMA((7,)), pltpu.SemaphoreType.DMA((7,)), pltpu.SemaphoreType.DMA],
    )(pieces)


def adamw(w, m, v, g8, name):
    r = w.shape[0]
    tm = _tile(r, 1024, 8)

    def body(w_ref, m_ref, v_ref, g_ref, g_out, d_out, m_out, v_out):
        g = g_ref[0].astype(F32)
        for s in range(1, N_DEV):
            g = g + g_ref[s].astype(F32)
        mn = ADAM_B1 * m_ref[...] + (1.0 - ADAM_B1) * g
        vn = ADAM_B2 * v_ref[...] + (1.0 - ADAM_B2) * (g * g)
        m_hat = mn / (1.0 - ADAM_B1 ** ADAM_STEP)
        v_hat = vn / (1.0 - ADAM_B2 ** ADAM_STEP)
        g_out[...] = g
        d_out[...] = -ADAM_LR * (m_hat / (jnp.sqrt(v_hat) + ADAM_EPS) + ADAM_WD * w_ref[...])
        m_out[...] = mn
        v_out[...] = vn

    blk = pl.BlockSpec((tm, LANES), lambda i: (i, 0))
    shp = jax.ShapeDtypeStruct((r, LANES), F32)
    return pl.pallas_call(
        body, name=name, grid=(r // tm,),
        in_specs=[blk, blk, blk, pl.BlockSpec((N_DEV, tm, LANES), lambda i: (0, i, 0))],
        out_specs=[blk, blk, blk, blk], out_shape=[shp, shp, shp, shp],
        compiler_params=_params(("parallel",)),
    )(w, m, v, g8)


def _pack(blocks, rows_mult):
    flat = jnp.concatenate([b.reshape(-1) for b in blocks])
    unit = rows_mult * LANES
    total = -(-flat.shape[0] // unit) * unit
    return jnp.pad(flat, (0, total - flat.shape[0])).reshape(-1, LANES)


def _unpack(packed, shapes):
    flat = packed.reshape(-1)
    out, pos = [], 0
    for s in shapes:
        n = math.prod(s)
        out.append(flat[pos:pos + n].reshape(s))
        pos += n
    return out


def _gathered_full(gathered, shapes, axes):
    per_dev = [_unpack(gathered[d], shapes) for d in range(N_DEV)]
    return [jnp.concatenate([per_dev[d][k] for d in range(N_DEV)], axis=axes[k]) for k in range(len(shapes))]


def _pieces_by_dest(fulls, axes, rows_mult):
    packs = []
    for d in range(N_DEV):
        blocks = []
        for f, ax in zip(fulls, axes):
            n = f.shape[ax] // N_DEV
            blocks.append(lax.slice_in_dim(f, d * n, (d + 1) * n, axis=ax))
        packs.append(_pack(blocks, rows_mult))
    return jnp.stack(packs)


class Layout:
    def __init__(self, d):
        self.d = d
        self.h = d // 128
        self.hr = d // 256
        self.z = 3 * d
        self.qb = 4 * d
        self.vb = 6 * d
        self.gb = 8 * d
        self.ga = 10 * d
        self.gbt = 11 * d
        self.ab = 12 * d
        self.used = 12 * d + LANES
        self.np = -(-self.used // 512) * 512

    def relayout_w_in(self, w):
        d, h4 = self.d, 4 * self.h
        return jnp.concatenate([w[:, :4 * d], w[:, 4 * d + h4:], w[:, 4 * d:4 * d + h4],
                                jnp.zeros((d, self.np - 12 * d - h4), w.dtype)], axis=1)

    def unlayout_w_in(self, w):
        d, h4 = self.d, 4 * self.h
        return jnp.concatenate([w[:, :4 * d], w[:, 12 * d:12 * d + h4], w[:, 4 * d:12 * d]], axis=1)


def _lane_row(vec):
    return jnp.pad(vec.reshape(-1), (0, LANES - vec.size)).reshape(1, LANES)


def _rope_tables(lp, half):
    inv = ROPE_BASE ** (-jnp.arange(half, dtype=F32) / half)
    pos = (jnp.arange(lp) - N_PAD).astype(F32)
    ang = pos[:, None] * inv[None, :]
    return jnp.cos(ang), jnp.sin(ang)


def local_step(x, target, meta, p):
    bsz, seq, d = x.shape
    lay = Layout(d)
    h_gdn, h_ret = lay.h, lay.hr
    lp = seq + CHUNK
    t_all = bsz * lp
    depth = p["w_up_a"].shape[0]
    ff = p["w_ffn_out"].shape[1]
    tm = _tile(lp, 512, 16)
    tmw = _tile(lp, 256, 16)
    tmn = _tile(lp, 1040, 16)
    cb = lambda cols: cols // LANES
    flat = lambda a: a.reshape(t_all, a.shape[-1])
    unflat = lambda a: a.reshape(bsz, lp, a.shape[-1])
    cos, sin = _rope_tables(lp, LANES)
    f_gb = make_f_gb(h_gdn)
    rope_f, rope_b = make_f_rope(1.0), make_f_rope(-1.0)

    head = jnp.concatenate([jnp.zeros((N_PAD, d), F32), meta], axis=0)
    h = jnp.concatenate([jnp.broadcast_to(head[None], (bsz, CHUNK, d)), x], axis=1)
    tgt = jnp.pad(target, ((0, 0), (CHUNK, 0), (0, 0)))

    saved = []
    for l in range(depth):
        s = {"h_in": h}
        nm = lambda k: f"l{l}_{k}"
        g_mix, g_ffn = p["norm_mix"][l][None], p["norm_ffn"][l][None]
        alog, dtb = _lane_row(p["gdn_a_log"][l]), _lane_row(p["gdn_dt_bias"][l])
        gain_a = p["gdn_norm"][l][None]
        logits = [_lane_row(p["ret_decay_logit"][l][0]), _lane_row(p["ret_decay_logit"][l][1])]
        cw = p["conv_w"][l]
        (hn,) = rowwise(nm("rms_mix"), f_rms, [Row(h, d)], [], [g_mix], [(d, d)], 1, tm, BF16)
        proj = unflat(matmul(flat(hn), p["w_in"][l], name=nm("mm_in")))
        qa = conv_fwd(proj, 0, cw[:, :d], True, nm("conv_q"))
        ka = conv_fwd(proj, cb(d), cw[:, d:2 * d], True, nm("conv_k"))
        va = conv_fwd(proj, cb(2 * d), cw[:, 2 * d:], False, nm("conv_v"))
        (gb,) = rowwise(nm("gb"), f_gb, [Row(proj, LANES, cb(lay.ab))], [], [alog, dtb], [(LANES, LANES)], 1, tmn)
        o0, st_a0, iv_a0 = gdn_fwd(qa, ka, va, gb, h_gdn, 0, None, nm("gdn_f0"))
        oa, st_a1, iv_a1 = gdn_fwd(qa, ka, va, gb, h_gdn, 1, o0, nm("gdn_f1"))
        (oan,) = rowwise(nm("gdn_out"), f_gdn_out, [Row(oa, LANES), Row(proj, LANES, cb(lay.z))], [], [gain_a],
                         [(d, LANES)], h_gdn, tmn, BF16)
        ya = unflat(matmul(flat(oan), p["w_up_a"][l], name=nm("mm_up_a")))
        (qkr,) = rowwise(nm("rope"), rope_f, [Row(proj, 2 * LANES, cb(lay.qb) // 2)], [Tab(cos), Tab(sin)], [],
                         [(2 * d, 2 * LANES)], 2 * h_ret, tmn)
        r0, st_b0 = ret_fwd(qkr, proj, cb(lay.vb), logits[0], h_ret, 0, None, nm("ret_f0"))
        ob, st_b1 = ret_fwd(qkr, proj, cb(lay.vb), logits[1], h_ret, 1, r0, nm("ret_f1"))
        (obn,) = rowwise(nm("ret_out"), f_ret_out, [Row(ob, 4 * LANES), Row(proj, 4 * LANES, cb(lay.gb) // 4)], [], [],
                         [(2 * d, 4 * LANES)], h_ret, tmw, BF16)
        yb = unflat(matmul(flat(obn), p["w_up_b"][l], name=nm("mm_up_b")))
        (mg,) = rowwise(nm("merge"), f_merge,
                        [Row(proj, LANES, cb(lay.ga)), Row(proj, LANES, cb(lay.gbt)), Row(ya, LANES), Row(yb, LANES)],
                        [], [], [(d, LANES)], cb(d), tmn, BF16)
        h_mid = unflat(matmul(flat(mg), p["w_out"][l], add=flat(h), name=nm("mm_out")))
        (hn2,) = rowwise(nm("rms_ffn"), f_rms, [Row(h_mid, d)], [], [g_ffn], [(d, d)], 1, tm, BF16)
        ffp = unflat(matmul(flat(hn2), p["w_ffn_in"][l], name=nm("mm_ffn_in")))
        (act,) = rowwise(nm("swiglu"), f_swiglu, [Row(ffp, 2 * ff)], [], [], [(ff, ff)], 1, tmw, BF16)
        h = unflat(matmul(flat(act), p["w_ffn_out"][l], add=flat(h_mid), name=nm("mm_ffn_out")))
        s.update(hn=hn, proj=proj, qa=qa, ka=ka, va=va, gb=gb, st_a=(st_a0, st_a1), iv_a=(iv_a0, iv_a1), oa=oa,
                 oan=oan, ya=ya, qkr=qkr, st_b=(st_b0, st_b1), ob=ob, obn=obn, yb=yb, mg=mg, h_mid=h_mid, hn2=hn2,
                 ffp=ffp, act=act, logits=logits, alog=alog, dtb=dtb, gain_a=gain_a, cw=cw, g_mix=g_mix, g_ffn=g_ffn)
        saved.append(s)

    dh, d_final, loss_row = loss_head(h, p["norm_final"][None], tgt, tm)

    grads = {k: [None] * depth for k in ("norm_mix", "w_in", "conv_w", "gdn_a_log", "gdn_dt_bias", "gdn_norm",
                                          "ret_decay_logit", "w_up_a", "w_up_b", "w_out", "norm_ffn", "w_ffn_in",
                                          "w_ffn_out")}
    for l in reversed(range(depth)):
        s = saved[l]
        nm = lambda k: f"l{l}_{k}"
        proj = s["proj"]
        dhf = flat(dh)
        grads["w_ffn_out"][l] = matmul(flat(s["act"]), dhf, ta=True, name=nm("mmg_ffn_out"))
        dact = unflat(matmul(dhf, p["w_ffn_out"][l], tb=True, name=nm("mmb_ffn_out")))
        (dffp,), _ = rowwise_vjp(nm("swiglu_b"), f_swiglu, [Row(s["ffp"], 2 * ff)], [], [], [dact], 1, tmw, narrow=(0,))
        grads["w_ffn_in"][l] = matmul(flat(s["hn2"]), flat(dffp), ta=True, name=nm("mmg_ffn_in"))
        dhn2 = unflat(matmul(flat(dffp), p["w_ffn_in"][l], tb=True, name=nm("mmb_ffn_in")))
        (dh_mid,), (dg_ffn,) = rowwise_vjp(nm("rms_ffn_b"), f_rms, [Row(s["h_mid"], d)], [], [s["g_ffn"]], [dhn2], 1, tmw,
                                           adds={0: dh})
        grads["norm_ffn"][l] = dg_ffn[0]
        dmf = flat(dh_mid)
        grads["w_out"][l] = matmul(flat(s["mg"]), dmf, ta=True, name=nm("mmg_out"))
        dmg = unflat(matmul(dmf, p["w_out"][l], tb=True, name=nm("mmb_out")))
        (dga, dgbt, dya, dyb), _ = rowwise_vjp(
            nm("merge_b"), f_merge,
            [Row(proj, LANES, cb(lay.ga)), Row(proj, LANES, cb(lay.gbt)), Row(s["ya"], LANES), Row(s["yb"], LANES)],
            [], [], [dmg], cb(d), tmn, narrow=(0, 1, 2, 3))
        grads["w_up_b"][l] = matmul(flat(s["obn"]), flat(dyb), ta=True, name=nm("mmg_up_b"))
        dobn = unflat(matmul(flat(dyb), p["w_up_b"][l], tb=True, name=nm("mmb_up_b")))
        (dob, dg_b), _ = rowwise_vjp(nm("ret_out_b"), f_ret_out,
                                     [Row(s["ob"], 4 * LANES), Row(proj, 4 * LANES, cb(lay.gb) // 4)], [], [], [dobn],
                                     h_ret, tmw, narrow=(1,))
        r1 = ret_bwd(s["qkr"], proj, cb(lay.vb), s["logits"][1], s["st_b"][1], dob, h_ret, 1, None, nm("ret_b1"))
        r0 = ret_bwd(s["qkr"], proj, cb(lay.vb), s["logits"][0], s["st_b"][0], dob, h_ret, 0, r1[:2], nm("ret_b0"),
                     dv_dtype=BF16)
        (dqk,) = rowwise(nm("rope_b"), rope_b, [Row(r0[0], 2 * LANES)], [Tab(cos), Tab(sin)], [],
                         [(2 * d, 2 * LANES)], 2 * h_ret, tmn, BF16)
        dv_b = r0[1]
        grads["ret_decay_logit"][l] = jnp.stack([r0[2][0, :h_ret], r1[2][0, :h_ret]])
        grads["w_up_a"][l] = matmul(flat(s["oan"]), flat(dya), ta=True, name=nm("mmg_up_a"))
        doan = unflat(matmul(flat(dya), p["w_up_a"][l], tb=True, name=nm("mmb_up_a")))
        (doa, dz), (dgain_a,) = rowwise_vjp(nm("gdn_out_b"), f_gdn_out,
                                            [Row(s["oa"], LANES), Row(proj, LANES, cb(lay.z))], [], [s["gain_a"]],
                                            [doan], h_gdn, tmn, narrow=(1,))
        grads["gdn_norm"][l] = dgain_a[0]
        a1 = gdn_bwd(s["qa"], s["ka"], s["va"], s["gb"], s["st_a"][1], s["iv_a"][1], doa, h_gdn, 1, None, nm("gdn_b1"))
        a0 = gdn_bwd(s["qa"], s["ka"], s["va"], s["gb"], s["st_a"][0], s["iv_a"][0], doa, h_gdn, 0, a1, nm("gdn_b0"))
        (dab,), (dalog, ddtb) = rowwise_vjp(nm("gb_b"), f_gb, [Row(proj, LANES, cb(lay.ab))], [], [s["alog"], s["dtb"]],
                                            [a0[3]], 1, tmn, narrow=(0,))
        grads["gdn_a_log"][l] = dalog[0, :2 * h_gdn].reshape(2, h_gdn)
        grads["gdn_dt_bias"][l] = ddtb[0, :2 * h_gdn].reshape(2, h_gdn)
        cw = s["cw"]
        dxq, dwq = conv_bwd(proj, 0, cw[:, :d], a0[0], True, nm("conv_q_b"))
        dxk, dwk = conv_bwd(proj, cb(d), cw[:, d:2 * d], a0[1], True, nm("conv_k_b"))
        dxv, dwv = conv_bwd(proj, cb(2 * d), cw[:, 2 * d:], a0[2], False, nm("conv_v_b"))
        grads["conv_w"][l] = jnp.concatenate([dwq, dwk, dwv], axis=1)
        dproj = jnp.concatenate([dxq, dxk, dxv, dz, dqk, dv_b, dg_b, dga, dgbt, dab,
                                 jnp.zeros((bsz, lp, lay.np - lay.used), BF16)], axis=-1)
        grads["w_in"][l] = matmul(flat(s["hn"]), flat(dproj), ta=True, name=nm("mmg_in"))
        dhn = unflat(matmul(flat(dproj), p["w_in"][l], tb=True, name=nm("mmb_in")))
        (dh,), (dg_mix,) = rowwise_vjp(nm("rms_mix_b"), f_rms, [Row(s["h_in"], d)], [], [s["g_mix"]], [dhn], 1, tmw,
                                       adds={0: dh_mid})
        grads["norm_mix"][l] = dg_mix[0]

    out = {k: jnp.stack(v) for k, v in grads.items()}
    out["norm_final"] = d_final[0]
    grad_x = dh[:, CHUNK:]
    grad_meta = jnp.sum(dh[:, N_PAD:CHUNK], axis=0)
    return loss_row, grad_x, grad_meta, out


BIG = ("w_in", "w_up_a", "w_up_b", "w_out", "w_ffn_in", "w_ffn_out")
BIG_AXES = (2, 1, 1, 1, 2, 1)
SMALL_SHARDED = ("meta_tokens", "conv_w")
SMALL_AXES = (1, 2)
REPLICATED = ("norm_mix", "gdn_a_log", "gdn_dt_bias", "gdn_norm", "ret_decay_logit", "norm_ffn", "norm_final")
WEIGHTS = ("meta_tokens", "norm_mix", "w_in", "conv_w", "gdn_a_log", "gdn_dt_bias", "gdn_norm", "ret_decay_logit",
           "w_up_a", "w_up_b", "w_out", "norm_ffn", "w_ffn_in", "w_ffn_out", "norm_final")


def kernel(x, meta_tokens, norm_mix, w_in, conv_w, gdn_a_log, gdn_dt_bias, gdn_norm, ret_decay_logit, w_up_a, w_up_b, w_out, norm_ffn, w_ffn_in, w_ffn_out, norm_final, loss_target, m_meta_tokens, m_norm_mix, m_w_in, m_conv_w, m_gdn_a_log, m_gdn_dt_bias, m_gdn_norm, m_ret_decay_logit, m_w_up_a, m_w_up_b, m_w_out, m_norm_ffn, m_w_ffn_in, m_w_ffn_out, m_norm_final, v_meta_tokens, v_norm_mix, v_w_in, v_conv_w, v_gdn_a_log, v_gdn_dt_bias, v_gdn_norm, v_ret_decay_logit, v_w_up_a, v_w_up_b, v_w_out, v_norm_ffn, v_w_ffn_in, v_w_ffn_out, v_norm_final):
    w = dict(meta_tokens=meta_tokens, norm_mix=norm_mix, w_in=w_in, conv_w=conv_w, gdn_a_log=gdn_a_log,
             gdn_dt_bias=gdn_dt_bias, gdn_norm=gdn_norm, ret_decay_logit=ret_decay_logit, w_up_a=w_up_a,
             w_up_b=w_up_b, w_out=w_out, norm_ffn=norm_ffn, w_ffn_in=w_ffn_in, w_ffn_out=w_ffn_out,
             norm_final=norm_final)
    m = dict(meta_tokens=m_meta_tokens, norm_mix=m_norm_mix, w_in=m_w_in, conv_w=m_conv_w, gdn_a_log=m_gdn_a_log,
             gdn_dt_bias=m_gdn_dt_bias, gdn_norm=m_gdn_norm, ret_decay_logit=m_ret_decay_logit, w_up_a=m_w_up_a,
             w_up_b=m_w_up_b, w_out=m_w_out, norm_ffn=m_norm_ffn, w_ffn_in=m_w_ffn_in, w_ffn_out=m_w_ffn_out,
             norm_final=m_norm_final)
    v = dict(meta_tokens=v_meta_tokens, norm_mix=v_norm_mix, w_in=v_w_in, conv_w=v_conv_w, gdn_a_log=v_gdn_a_log,
             gdn_dt_bias=v_gdn_dt_bias, gdn_norm=v_gdn_norm, ret_decay_logit=v_ret_decay_logit, w_up_a=v_w_up_a,
             w_up_b=v_w_up_b, w_out=v_w_out, norm_ffn=v_norm_ffn, w_ffn_in=v_w_ffn_in, w_ffn_out=v_w_ffn_out,
             norm_final=v_norm_final)
    d = x.shape[-1]
    lay = Layout(d)
    sharded = BIG + SMALL_SHARDED
    sharded_axes = BIG_AXES + SMALL_AXES

    big_shapes = [w[k].shape for k in BIG]
    gathered = all_gather_hbm(_pack([w[k].astype(BF16) for k in BIG], 16), "gather_weights")
    full = dict(zip(BIG, _gathered_full(gathered, big_shapes, BIG_AXES)))
    small_shapes = [w[k].shape for k in SMALL_SHARDED]
    gathered_s = all_gather_small(_pack([w[k] for k in SMALL_SHARDED], 8), "gather_small")
    full.update(zip(SMALL_SHARDED, _gathered_full(gathered_s, small_shapes, SMALL_AXES)))
    p = {k: w[k] for k in REPLICATED}
    p.update({k: full[k] for k in BIG + ("conv_w",)})
    p["w_in"] = jnp.stack([lay.relayout_w_in(full["w_in"][l]) for l in range(full["w_in"].shape[0])])

    loss_row, grad_x, grad_meta, g = local_step(x, loss_target, full["meta_tokens"], p)
    g["meta_tokens"] = grad_meta
    g["w_in"] = jnp.stack([lay.unlayout_w_in(g["w_in"][l]) for l in range(g["w_in"].shape[0])])

    pieces = all_to_all_hbm(_pieces_by_dest([g[k].astype(BF16) for k in sharded], sharded_axes, 1024),
                            "exchange_grads")
    local_shapes = [w[k].shape for k in sharded]
    res = adamw(_pack([w[k] for k in sharded], 1024), _pack([m[k] for k in sharded], 1024),
                _pack([v[k] for k in sharded], 1024), pieces, "adamw_sharded")
    outs = {kind: dict(zip(sharded, _unpack(r, local_shapes))) for kind, r in zip(("g", "d", "m", "v"), res)}

    rep_shapes = [w[k].shape for k in REPLICATED]
    part = _pack([g[k] for k in REPLICATED] + [loss_row[0, :1]], 8)
    parts = all_gather_small(part, "gather_replicated")
    pad1 = lambda a: _pack([a[k] for k in REPLICATED] + [jnp.zeros((1,), F32)], 8)
    res_r = adamw(pad1(w), pad1(m), pad1(v), parts, "adamw_replicated")
    for kind, r in zip(("g", "d", "m", "v"), res_r):
        outs[kind].update(zip(REPLICATED, _unpack(r, rep_shapes)))
    loss = res_r[0].reshape(-1)[sum(math.prod(s) for s in rep_shapes)]

    return (loss, grad_x, *[outs["g"][k] for k in WEIGHTS], *[outs["d"][k] for k in WEIGHTS],
            *[outs["m"][k] for k in WEIGHTS], *[outs["v"][k] for k in WEIGHTS])
```

```python
import functools
import math

import jax
import jax.numpy as jnp
from jax import lax
from jax.experimental import pallas as pl
from jax.experimental.pallas import tpu as pltpu

F32 = jnp.float32
BF16 = jnp.bfloat16
HIGHEST = lax.Precision.HIGHEST

LANES = 128
CHUNK = 64
N_META = 16
N_PAD = CHUNK - N_META
CONV_K = 5
EPS = 1e-6
ROPE_BASE = 10000.0
N_DEV = 8
VMEM_LIMIT = 56 * 1024 * 1024
MATMUL_VMEM = 40 * 1024 * 1024

ADAM_LR, ADAM_B1, ADAM_B2, ADAM_EPS, ADAM_WD, ADAM_STEP = 0.001, 0.9, 0.999, 1e-08, 0.01, 10


def _tile(n, cap, mult):
    if n <= cap:
        return n
    best = None
    for t in range(mult, cap + 1, mult):
        if n % t == 0:
            best = t
    assert best is not None, (n, cap, mult)
    return best


def _params(sem):
    return pltpu.CompilerParams(dimension_semantics=sem, vmem_limit_bytes=VMEM_LIMIT)


def _raw_dot(a, b, ca, cb, exact):
    dims = (((ca,), (cb,)), ((), ()))
    a_hi, b_hi = a.astype(BF16), b.astype(BF16)
    out = lax.dot_general(a_hi, b_hi, dims, preferred_element_type=F32)
    if exact:
        a_lo = (a - a_hi.astype(F32)).astype(BF16)
        b_lo = (b - b_hi.astype(F32)).astype(BF16)
        out = out + lax.dot_general(a_hi, b_lo, dims, preferred_element_type=F32)
        out = out + lax.dot_general(a_lo, b_hi, dims, preferred_element_type=F32)
    return out


@functools.partial(jax.custom_vjp, nondiff_argnums=(2, 3, 4))
def _dot(a, b, ca, cb, exact):
    return _raw_dot(a, b, ca, cb, exact)


def _dot_fwd(a, b, ca, cb, exact):
    return _raw_dot(a, b, ca, cb, exact), (a, b)


def _dot_bwd(ca, cb, exact, res, g):
    a, b = res
    if ca == 1:
        da = _raw_dot(g, b, 1, 1 if cb == 0 else 0, exact)
    else:
        da = _raw_dot(b, g, 1 if cb == 0 else 0, 1, exact)
    if cb == 0:
        db = _raw_dot(a, g, 0 if ca == 1 else 1, 0, exact)
    else:
        db = _raw_dot(g, a, 0, 0 if ca == 1 else 1, exact)
    return da, db


_dot.defvjp(_dot_fwd, _dot_bwd)


@functools.partial(jax.custom_vjp, nondiff_argnums=(1, 2))
def _split(x, n, axis):
    return lax.slice_in_dim(x, 0, n, axis=axis), lax.slice_in_dim(x, n, x.shape[axis], axis=axis)


_split.defvjp(lambda x, n, axis: (_split(x, n, axis), None),
              lambda n, axis, _, g: (jnp.concatenate([g[0], g[1]], axis=axis),))


def _bdot(a, b, ca=1, cb=0):
    return _dot(a, b, ca, cb, False)


def _xdot(a, b, ca=1, cb=0):
    return _dot(a, b, ca, cb, True)


def matmul(a, b, *, ta=False, tb=False, add=None, name):
    m, k = (a.shape[1], a.shape[0]) if ta else a.shape
    k2, n = (b.shape[1], b.shape[0]) if tb else b.shape
    assert k == k2, (a.shape, b.shape, ta, tb)
    has_add = add is not None
    tm_cap, tn_cap = 2080, 1408
    while True:
        tm = _tile(m, tm_cap, 128 if ta else 16)
        tn = _tile(n, tn_cap, 128)
        tk = _tile(k, 1664, 128 if (not ta or tb) else 16)
        nk = k // tk
        need = 2 * (tm * tk * a.dtype.itemsize + tk * tn * b.dtype.itemsize + tm * tn * 4 * (2 if has_add else 1))
        need += tm * tn * 4 if nk > 1 else 0
        if need <= MATMUL_VMEM or (tm_cap <= 256 and tn_cap <= 256):
            break
        if tm_cap >= tn_cap:
            tm_cap //= 2
        else:
            tn_cap //= 2
    ca, cb = (0 if ta else 1), (1 if tb else 0)
    a_spec = pl.BlockSpec((tk, tm), lambda i, j, kk: (kk, i)) if ta else pl.BlockSpec((tm, tk), lambda i, j, kk: (i, kk))
    b_spec = pl.BlockSpec((tn, tk), lambda i, j, kk: (j, kk)) if tb else pl.BlockSpec((tk, tn), lambda i, j, kk: (kk, j))
    o_spec = pl.BlockSpec((tm, tn), lambda i, j, kk: (i, j))

    def body(*refs):
        a_ref, b_ref = refs[0], refs[1]
        add_ref = refs[2] if has_add else None
        o_ref = refs[3] if has_add else refs[2]
        def part():
            return _raw_dot(a_ref[...], b_ref[...], ca, cb, False)

        if nk == 1:
            o_ref[...] = part() + add_ref[...] if has_add else part()
            return
        acc_ref = refs[-1]
        kk = pl.program_id(2)

        @pl.when(kk == 0)
        def _():
            acc_ref[...] = part()

        @pl.when(jnp.logical_and(kk > 0, kk < nk - 1))
        def _():
            acc_ref[...] += part()

        @pl.when(kk == nk - 1)
        def _():
            o_ref[...] = acc_ref[...] + part() + add_ref[...] if has_add else acc_ref[...] + part()

    ins = [a, b] + ([add] if has_add else [])
    in_specs = [a_spec, b_spec] + ([o_spec] if has_add else [])
    return pl.pallas_call(
        body, name=name, grid=(m // tm, n // tn, nk), in_specs=in_specs, out_specs=o_spec,
        out_shape=jax.ShapeDtypeStruct((m, n), F32),
        scratch_shapes=[pltpu.VMEM((tm, tn), F32)] if nk > 1 else [],
        compiler_params=_params(("parallel", "parallel", "arbitrary")),
    )(*ins)


class Row:
    def __init__(self, arr, bc, off=0, per_head=True, diff=True):
        self.arr, self.bc, self.off, self.per_head, self.diff = arr, bc, off, per_head, diff


class Tab:
    def __init__(self, arr):
        self.arr = arr


def _row_specs(rows, tabs, pars, tm):
    specs = []
    for r in rows:
        specs.append(pl.BlockSpec((1, tm, r.bc), functools.partial(
            lambda b, i, h, off, ph: (b, i, off + (h if ph else 0)), off=r.off, ph=r.per_head)))
    for t in tabs:
        specs.append(pl.BlockSpec((tm, t.arr.shape[1]), lambda b, i, h: (i, 0)))
    for p in pars:
        specs.append(pl.BlockSpec(p.shape, lambda b, i, h: (0, 0)))
    return specs


def rowwise(name, fn, rows, tabs, pars, outs, nh, tm, out_dtype=F32):
    bsz, lp = rows[0].arr.shape[:2]
    nr, nt, npar = len(rows), len(tabs), len(pars)

    def body(*refs):
        t0 = pl.program_id(1) * tm
        ins = [refs[k][0] for k in range(nr)] + [refs[nr + k][...] for k in range(nt + npar)]
        res = fn(t0, *ins)
        for o_ref, o in zip(refs[nr + nt + npar:], res):
            o_ref[0] = o.astype(o_ref.dtype)

    return pl.pallas_call(
        body, name=name, grid=(bsz, lp // tm, nh),
        in_specs=_row_specs(rows, tabs, pars, tm),
        out_specs=[pl.BlockSpec((1, tm, bc), lambda b, i, h: (b, i, h)) for _, bc in outs],
        out_shape=[jax.ShapeDtypeStruct((bsz, lp, c), out_dtype) for c, _ in outs],
        compiler_params=_params(("parallel", "parallel", "parallel")),
    )(*[r.arr for r in rows], *[t.arr for t in tabs], *pars)


def rowwise_vjp(name, fn, rows, tabs, pars, couts, nh, tm, adds=None, narrow=()):
    bsz, lp = rows[0].arr.shape[:2]
    nr, nt, npar, nco = len(rows), len(tabs), len(pars), len(couts)
    adds = adds or {}
    add_keys = sorted(adds)
    diff_idx = [k for k, r in enumerate(rows) if r.diff]
    for k in diff_idx:
        assert rows[k].per_head or nh == 1

    def body(*refs):
        b, i, h = pl.program_id(0), pl.program_id(1), pl.program_id(2)
        t0 = i * tm
        pos = 0
        row_v = [refs[k][0] for k in range(nr)]
        pos += nr
        tab_v = [refs[pos + k][...] for k in range(nt)]
        pos += nt
        par_v = [refs[pos + k][...] for k in range(npar)]
        pos += npar
        co_v = [refs[pos + k][0] for k in range(nco)]
        pos += nco
        add_v = {key: refs[pos + k][0] for k, key in enumerate(add_keys)}
        pos += len(add_keys)
        drow_refs = refs[pos:pos + len(diff_idx)]
        dpar_refs = refs[pos + len(diff_idx):]

        def f(dvals, pvals):
            full = list(row_v)
            for k, v in zip(diff_idx, dvals):
                full[k] = v
            return tuple(fn(t0, *full, *tab_v, *pvals))

        _, pull = jax.vjp(f, [row_v[k] for k in diff_idx], par_v)
        d_rows, d_pars = pull(tuple(co_v))
        for ref, k, d in zip(drow_refs, diff_idx, d_rows):
            ref[0] = (d + add_v[k] if k in add_v else d).astype(ref.dtype)
        first = jnp.logical_and(jnp.logical_and(b == 0, i == 0), h == 0)
        for ref, d in zip(dpar_refs, d_pars):
            @pl.when(first)
            def _(ref=ref, d=d):
                ref[...] = d

            @pl.when(jnp.logical_not(first))
            def _(ref=ref, d=d):
                ref[...] += d

    out_block = lambda bc: pl.BlockSpec((1, tm, bc), lambda b, i, h: (b, i, h))
    in_specs = _row_specs(rows, tabs, pars, tm)
    in_specs += [out_block(c.shape[2] // nh) for c in couts]
    in_specs += [out_block(rows[k].bc) for k in add_keys]
    out_specs = [out_block(rows[k].bc) for k in diff_idx]
    out_specs += [pl.BlockSpec(p.shape, lambda b, i, h: (0, 0)) for p in pars]
    out_shape = [jax.ShapeDtypeStruct((bsz, lp, nh * rows[k].bc), BF16 if k in narrow else F32) for k in diff_idx]
    out_shape += [jax.ShapeDtypeStruct(p.shape, F32) for p in pars]
    res = pl.pallas_call(
        body, name=name, grid=(bsz, lp // tm, nh), in_specs=in_specs, out_specs=out_specs, out_shape=out_shape,
        compiler_params=_params(("arbitrary", "arbitrary", "arbitrary")),
    )(*[r.arr for r in rows], *[t.arr for t in tabs], *pars, *couts, *[adds[k] for k in add_keys])
    return res[:len(diff_idx)], res[len(diff_idx):]


def _real_rows(t0, tm):
    return (t0 + lax.broadcasted_iota(jnp.int32, (tm, 1), 0)) >= N_PAD


def f_rms(t0, x, gain):
    return (x * lax.rsqrt(jnp.mean(x * x, axis=-1, keepdims=True) + EPS) * gain,)


def make_f_gb(heads):
    def f_gb(t0, ab, alog, dtb):
        lane = lax.broadcasted_iota(jnp.int32, ab.shape, 1)
        g = -jnp.exp(alog) * jax.nn.softplus(ab + dtb)
        beta = jax.nn.sigmoid(ab)
        out = jnp.where(lane < 2 * heads, g, jnp.where(lane < 4 * heads, beta, 0.0))
        return (jnp.where(_real_rows(t0, ab.shape[0]), out, 0.0),)
    return f_gb


def f_gdn_out(t0, o, z, gain):
    on = o * lax.rsqrt(jnp.mean(o * o, axis=-1, keepdims=True) + EPS)
    return (on * gain * jax.nn.silu(z),)


def f_ret_out(t0, o, g):
    on = o * lax.rsqrt(jnp.mean(o * o, axis=-1, keepdims=True) + EPS)
    return (on * jax.nn.silu(g),)


def f_merge(t0, ga, gb, ya, yb):
    return (jax.nn.sigmoid(ga) * ya + jax.nn.sigmoid(gb) * yb,)


def f_swiglu(t0, x):
    gate, up = _split(x, x.shape[1] // 2, 1)
    return (jax.nn.silu(gate) * up,)


def make_f_rope(sign):
    def f_rope(t0, x, cos, sin):
        half = x.shape[1] // 2
        x1, x2 = x[:, :half], x[:, half:]
        s = sin * sign
        return (jnp.concatenate([x1 * cos - x2 * s, x1 * s + x2 * cos], axis=1),)
    return f_rope


def loss_head(h, gain, target, tm):
    bsz, lp, d = h.shape

    def body(h_ref, g_ref, t_ref, dh_ref, dg_ref, loss_ref):
        b, i = pl.program_id(0), pl.program_id(1)
        rows = (i * tm + lax.broadcasted_iota(jnp.int32, (tm, 1), 0)) >= CHUNK
        tgt = t_ref[0]

        def f(x, gain_v):
            y = f_rms(0, x, gain_v)[0]
            err = jnp.where(rows, y - tgt, 0.0)
            return 0.5 * jnp.sum(jnp.mean(err * err, axis=-1, keepdims=True), keepdims=True)

        val, pull = jax.vjp(f, h_ref[0], g_ref[...])
        dh, dg = pull(jnp.ones((1, 1), F32))
        dh_ref[0] = dh
        first = jnp.logical_and(b == 0, i == 0)
        val_row = jnp.broadcast_to(val, (1, LANES))

        @pl.when(first)
        def _():
            dg_ref[...] = dg
            loss_ref[...] = val_row

        @pl.when(jnp.logical_not(first))
        def _():
            dg_ref[...] += dg
            loss_ref[...] += val_row

    blk = pl.BlockSpec((1, tm, d), lambda b, i: (b, i, 0))
    return pl.pallas_call(
        body, name="loss_head", grid=(bsz, lp // tm),
        in_specs=[blk, pl.BlockSpec((1, d), lambda b, i: (0, 0)), blk],
        out_specs=[blk, pl.BlockSpec((1, d), lambda b, i: (0, 0)), pl.BlockSpec((1, LANES), lambda b, i: (0, 0))],
        out_shape=[jax.ShapeDtypeStruct((bsz, lp, d), F32), jax.ShapeDtypeStruct((1, d), F32),
                   jax.ShapeDtypeStruct((1, LANES), F32)],
        compiler_params=_params(("arbitrary", "arbitrary")),
    )(h, gain, target)


def _conv_pre(x, w_ref):
    lp = x.shape[0]
    acc = w_ref[2:3, :] * x
    for k in (0, 1, 3, 4):
        acc = acc + w_ref[k:k + 1, :] * pltpu.roll(x, (2 - k) % lp, 0)
    return acc


def conv_fwd(proj, off, w, l2, name):
    bsz, lp, _ = proj.shape
    d = w.shape[1]

    def body(x_ref, w_ref, o_ref):
        x = x_ref[0]
        s = jnp.where(_real_rows(0, lp), jax.nn.silu(_conv_pre(x, w_ref)), 0.0)
        if l2:
            s = s * lax.rsqrt(jnp.sum(s * s, axis=-1, keepdims=True) + EPS)
        o_ref[0] = s

    return pl.pallas_call(
        body, name=name, grid=(d // LANES, bsz),
        in_specs=[pl.BlockSpec((1, lp, LANES), lambda j, b: (b, 0, off + j)),
                  pl.BlockSpec((CONV_K, LANES), lambda j, b: (0, j))],
        out_specs=pl.BlockSpec((1, lp, LANES), lambda j, b: (b, 0, j)),
        out_shape=jax.ShapeDtypeStruct((bsz, lp, d), F32),
        compiler_params=_params(("parallel", "parallel")),
    )(proj, w)


def conv_bwd(proj, off, w, dy, l2, name):
    bsz, lp, _ = proj.shape
    d = w.shape[1]

    def body(x_ref, w_ref, dy_ref, dx_ref, dw_ref):
        b = pl.program_id(1)
        x, g = x_ref[0], dy_ref[0]
        real = _real_rows(0, lp)
        c = _conv_pre(x, w_ref)
        sg = jax.nn.sigmoid(c)
        s = jnp.where(real, c * sg, 0.0)
        if l2:
            r = lax.rsqrt(jnp.sum(s * s, axis=-1, keepdims=True) + EPS)
            g = r * g - s * (r * r * r) * jnp.sum(g * s, axis=-1, keepdims=True)
        dc = jnp.where(real, g * (sg * (1.0 + c * (1.0 - sg))), 0.0)
        dx = w_ref[2:3, :] * dc
        for k in (0, 1, 3, 4):
            dx = dx + w_ref[k:k + 1, :] * pltpu.roll(dc, (k - 2) % lp, 0)
        dx_ref[0] = jnp.where(real, dx, 0.0).astype(dx_ref.dtype)
        tap_row = lax.broadcasted_iota(jnp.int32, (CONV_K, LANES), 0)
        dw = jnp.zeros((CONV_K, LANES), F32)
        for k in range(CONV_K):
            xs = x if k == 2 else pltpu.roll(x, (2 - k) % lp, 0)
            dw = dw + jnp.where(tap_row == k, jnp.sum(dc * xs, axis=0, keepdims=True), 0.0)

        @pl.when(b == 0)
        def _():
            dw_ref[...] = dw

        @pl.when(b > 0)
        def _():
            dw_ref[...] += dw

    blk = pl.BlockSpec((1, lp, LANES), lambda j, b: (b, 0, j))
    return pl.pallas_call(
        body, name=name, grid=(d // LANES, bsz),
        in_specs=[pl.BlockSpec((1, lp, LANES), lambda j, b: (b, 0, off + j)),
                  pl.BlockSpec((CONV_K, LANES), lambda j, b: (0, j)), blk],
        out_specs=[blk, pl.BlockSpec((CONV_K, LANES), lambda j, b: (0, j))],
        out_shape=[jax.ShapeDtypeStruct((bsz, lp, d), BF16), jax.ShapeDtypeStruct((CONV_K, d), F32)],
        compiler_params=_params(("parallel", "arbitrary")),
    )(proj, w, dy)


def _tri_masks(rev):
    ii = lax.broadcasted_iota(jnp.int32, (CHUNK, CHUNK), 0)
    jj = lax.broadcasted_iota(jnp.int32, (CHUNK, CHUNK), 1)
    return ((ii <= jj), (ii < jj)) if rev else ((ii >= jj), (ii > jj))


def _lane_pick(block, lane):
    sel = lax.broadcasted_iota(jnp.int32, block.shape, 1) == lane
    return jnp.sum(jnp.where(sel, block, 0.0), axis=1, keepdims=True)


def _cumsum_impl(x, rev):
    n = x.shape[0]
    row = lax.broadcasted_iota(jnp.int32, x.shape, 0)
    step = 1
    while step < n:
        if rev:
            x = x + jnp.where(row < n - step, pltpu.roll(x, n - step, 0), 0.0)
        else:
            x = x + jnp.where(row >= step, pltpu.roll(x, step, 0), 0.0)
        step *= 2
    return x


@functools.partial(jax.custom_vjp, nondiff_argnums=(1,))
def _cumsum_rows(x, rev):
    return _cumsum_impl(x, rev)


_cumsum_rows.defvjp(lambda x, rev: (_cumsum_impl(x, rev), None),
                    lambda rev, _, g: (_cumsum_impl(g, not rev),))


def _unit_inv_impl(m):
    n = m.shape[0]
    eye = (lax.broadcasted_iota(jnp.int32, (n, n), 0) == lax.broadcasted_iota(jnp.int32, (n, n), 1)).astype(F32)
    p = -m
    inv = eye + p
    step = 2
    while step < n:
        p = _raw_dot(p, p, 1, 0, True)
        inv = inv + _raw_dot(inv, p, 1, 0, True)
        step *= 2
    return inv


@jax.custom_vjp
def _unit_inv(m):
    return _unit_inv_impl(m)


def _unit_inv_fwd(m):
    inv = _unit_inv_impl(m)
    return inv, inv


def _unit_inv_bwd(inv, g):
    return (-_raw_dot(_raw_dot(inv, g, 0, 0, True), inv, 1, 1, True),)


_unit_inv.defvjp(_unit_inv_fwd, _unit_inv_bwd)


GROUP = 2


def _unit_inv_all_impl(ms):
    n = ms[0].shape[0]
    width = GROUP * n
    dims = (((1,), (0,)), ((), ()))
    lane = lax.broadcasted_iota(jnp.int32, (n, width), 1)
    row = lax.broadcasted_iota(jnp.int32, (n, width), 0)
    blocks = [jnp.logical_and(lane >= t * n, lane < (t + 1) * n) for t in range(GROUP)]

    def halves(x):
        hi = x.astype(BF16)
        return hi, (x - hi.astype(F32)).astype(BF16)

    def xdot(a, b):
        diag = jnp.concatenate([jnp.where(blk, b, 0.0) for blk in blocks], axis=0)
        a_hi, a_lo = halves(a)
        d_hi, d_lo = halves(diag)
        out = lax.dot_general(a_hi, d_hi, dims, preferred_element_type=F32)
        out = out + lax.dot_general(a_hi, d_lo, dims, preferred_element_type=F32)
        return out + lax.dot_general(a_lo, d_hi, dims, preferred_element_type=F32)

    eye = (row == lane % n).astype(F32)
    groups = range(len(ms) // GROUP)
    p = [-jnp.concatenate(ms[t * GROUP:(t + 1) * GROUP], axis=1) for t in groups]
    inv = [eye + p[t] for t in groups]
    step = 2
    while step < n:
        p = [xdot(p[t], p[t]) for t in groups]
        inv = [inv[t] + xdot(inv[t], p[t]) for t in groups]
        step *= 2
    return [inv[t][:, u * n:(u + 1) * n] for t in groups for u in range(GROUP)]


@jax.custom_vjp
def _unit_inv_all(ms):
    return _unit_inv_all_impl(ms)


def _unit_inv_all_fwd(ms):
    out = _unit_inv_all_impl(ms)
    return out, out


_unit_inv_all.defvjp(_unit_inv_all_fwd, lambda invs, g: ([_unit_inv_bwd(i, gi)[0] for i, gi in zip(invs, g)],))


@jax.custom_vjp
def _unit_inv_known(m, inv):
    return inv


_unit_inv_known.defvjp(lambda m, inv: (inv, inv),
                       lambda inv, g: (_unit_inv_bwd(inv, g)[0], jnp.zeros_like(inv)))


def _gdn_chunk(qs, ks, vs, gb, ss, g_lanes, b_lanes, rev, invs=None, want_inv=False):
    nh = len(qs)
    dk = qs[0].shape[1]
    incl, strict = _tri_masks(rev)
    hs = range(nh)
    g = [_lane_pick(gb, l) for l in g_lanes]
    beta = [_lane_pick(gb, l) for l in b_lanes]
    qs = [q * (dk ** -0.5) for q in qs]
    gc_sq = [_cumsum_rows(jnp.broadcast_to(g[h], (CHUNK, CHUNK)), rev) for h in hs]
    gc = [_cumsum_rows(jnp.broadcast_to(g[h], (CHUNK, dk)), rev) for h in hs]
    g_last = [jnp.sum(g[h], axis=0, keepdims=True) for h in hs]
    decay = [jnp.where(incl, jnp.exp(jnp.where(incl, gc_sq[h] - gc_sq[h].T, 0.0)), 0.0) for h in hs]
    kb = [ks[h] * beta[h] for h in hs]
    kk = [_split(_bdot(jnp.concatenate([kb[h], qs[h]], axis=0), ks[h], 1, 1), CHUNK, 0) for h in hs]
    m = [jnp.where(strict, kk[h][0] * decay[h], 0.0) for h in hs]
    qk = [kk[h][1] * decay[h] for h in hs]
    if invs is not None:
        inv = [_unit_inv_known(m[h], invs[h]) for h in hs]
    elif nh % GROUP == 0:
        inv = _unit_inv_all(m)
    else:
        inv = [_unit_inv(m[h]) for h in hs]
    e_gc = [jnp.exp(gc[h]) for h in hs]
    uw = [_split(_bdot(inv[h], jnp.concatenate([vs[h] * beta[h], kb[h] * e_gc[h]], axis=1)), vs[h].shape[1], 1)
          for h in hs]
    u = [uw[h][0] for h in hs]
    w = [uw[h][1] for h in hs]
    q_dec = [qs[h] * e_gc[h] for h in hs]
    k_dec = [ks[h] * jnp.exp(g_last[h] - gc[h]) for h in hs]
    ws = [_split(_bdot(jnp.concatenate([w[h], q_dec[h]], axis=0), ss[h]), CHUNK, 0) for h in hs]
    v_new = [u[h] - ws[h][0] for h in hs]
    o = [ws[h][1] + _bdot(qk[h], v_new[h]) for h in hs]
    s_new = [ss[h] * jnp.exp(g_last[h]) + _bdot(k_dec[h], v_new[h], 0, 0) for h in hs]
    return (o, s_new, inv) if want_inv else (o, s_new)


def gdn_fwd(q, k, v, gb, heads, direction, o_prev, name):
    bsz, lp, d = q.shape
    nc = lp // CHUNK
    rev = direction == 1
    cm = (lambda n: nc - 1 - n) if rev else (lambda n: n)
    has_prev = o_prev is not None

    def body(*refs):
        q_ref, k_ref, v_ref, gb_ref = refs[:4]
        prev_ref = refs[4] if has_prev else None
        o_ref, st_ref, inv_ref, s_ref = refs[-4:]
        n = pl.program_id(1)

        @pl.when(n == 0)
        def _():
            s_ref[...] = jnp.zeros_like(s_ref)

        sls = [slice(h * LANES, (h + 1) * LANES) for h in range(heads)]
        ss = [s_ref[h] for h in range(heads)]
        for h in range(heads):
            st_ref[0, 0, h] = ss[h]
        os_, s_new, inv = _gdn_chunk([q_ref[0, :, sl] for sl in sls], [k_ref[0, :, sl] for sl in sls],
                                     [v_ref[0, :, sl] for sl in sls], gb_ref[0], ss,
                                     [direction * heads + h for h in range(heads)],
                                     [2 * heads + direction * heads + h for h in range(heads)], rev, want_inv=True)
        for h, sl in enumerate(sls):
            s_ref[h] = s_new[h]
            inv_ref[0, 0, h] = inv[h]
            o_ref[0, :, sl] = os_[h] + prev_ref[0, :, sl] if has_prev else os_[h]

    blk = pl.BlockSpec((1, CHUNK, d), lambda b, n: (b, cm(n), 0))
    gblk = pl.BlockSpec((1, CHUNK, LANES), lambda b, n: (b, cm(n), 0))
    st_blk = pl.BlockSpec((1, 1, heads, LANES, LANES), lambda b, n: (b, cm(n), 0, 0, 0))
    inv_blk = pl.BlockSpec((1, 1, heads, CHUNK, CHUNK), lambda b, n: (b, cm(n), 0, 0, 0))
    return pl.pallas_call(
        body, name=name, grid=(bsz, nc),
        in_specs=[blk, blk, blk, gblk] + ([blk] if has_prev else []),
        out_specs=[blk, st_blk, inv_blk],
        out_shape=[jax.ShapeDtypeStruct((bsz, lp, d), F32),
                   jax.ShapeDtypeStruct((bsz, nc, heads, LANES, LANES), F32),
                   jax.ShapeDtypeStruct((bsz, nc, heads, CHUNK, CHUNK), F32)],
        scratch_shapes=[pltpu.VMEM((heads, LANES, LANES), F32)],
        compiler_params=_params(("parallel", "arbitrary")),
    )(q, k, v, gb, *([o_prev] if has_prev else []))


def gdn_bwd(q, k, v, gb, states, invs, do, heads, direction, prev, name):
    bsz, lp, d = q.shape
    nc = lp // CHUNK
    rev = direction == 1
    cm = (lambda n: n) if rev else (lambda n: nc - 1 - n)
    has_prev = prev is not None

    def body(*refs):
        q_ref, k_ref, v_ref, gb_ref, st_ref, inv_ref, do_ref = refs[:7]
        prev_refs = refs[7:11] if has_prev else None
        dq_ref, dk_ref, dv_ref, dgb_ref, ds_ref = refs[-5:]
        n = pl.program_id(1)

        @pl.when(n == 0)
        def _():
            ds_ref[...] = jnp.zeros_like(ds_ref)

        sls = [slice(h * LANES, (h + 1) * LANES) for h in range(heads)]
        f = functools.partial(_gdn_chunk, g_lanes=[direction * heads + h for h in range(heads)],
                              b_lanes=[2 * heads + direction * heads + h for h in range(heads)], rev=rev,
                              invs=[inv_ref[0, 0, h] for h in range(heads)])
        _, pull = jax.vjp(f, [q_ref[0, :, sl] for sl in sls], [k_ref[0, :, sl] for sl in sls],
                          [v_ref[0, :, sl] for sl in sls], gb_ref[0], [st_ref[0, 0, h] for h in range(heads)])
        dq, dk, dv, dgb, ds = pull(([do_ref[0, :, sl] for sl in sls], [ds_ref[h] for h in range(heads)]))
        for h, sl in enumerate(sls):
            ds_ref[h] = ds[h]
            if has_prev:
                dq[h], dk[h], dv[h] = (dq[h] + prev_refs[0][0, :, sl], dk[h] + prev_refs[1][0, :, sl],
                                       dv[h] + prev_refs[2][0, :, sl])
            dq_ref[0, :, sl] = dq[h]
            dk_ref[0, :, sl] = dk[h]
            dv_ref[0, :, sl] = dv[h]
        dgb_ref[0] = dgb + prev_refs[3][0] if has_prev else dgb

    blk = pl.BlockSpec((1, CHUNK, d), lambda b, n: (b, cm(n), 0))
    gblk = pl.BlockSpec((1, CHUNK, LANES), lambda b, n: (b, cm(n), 0))
    st_blk = pl.BlockSpec((1, 1, heads, LANES, LANES), lambda b, n: (b, cm(n), 0, 0, 0))
    inv_blk = pl.BlockSpec((1, 1, heads, CHUNK, CHUNK), lambda b, n: (b, cm(n), 0, 0, 0))
    big = jax.ShapeDtypeStruct((bsz, lp, d), F32)
    return pl.pallas_call(
        body, name=name, grid=(bsz, nc),
        in_specs=[blk, blk, blk, gblk, st_blk, inv_blk, blk] + ([blk, blk, blk, gblk] if has_prev else []),
        out_specs=[blk, blk, blk, gblk],
        out_shape=[big, big, big, jax.ShapeDtypeStruct((bsz, lp, LANES), F32)],
        scratch_shapes=[pltpu.VMEM((heads, LANES, LANES), F32)],
        compiler_params=_params(("parallel", "arbitrary")),
    )(q, k, v, gb, states, invs, do, *(list(prev) if has_prev else []))


def _ret_chunk(q, k, v, r, logit, lane, rev):
    dk = q.shape[1]
    lg = jax.nn.log_sigmoid(_lane_pick(logit, lane))
    k = k * (dk ** -0.5)
    ii = lax.broadcasted_iota(jnp.int32, (CHUNK, CHUNK), 0)
    jj = lax.broadcasted_iota(jnp.int32, (CHUNK, CHUNK), 1)
    pos = lax.broadcasted_iota(jnp.int32, (CHUNK, 1), 0)
    if rev:
        incl, rel = ii <= jj, (jj - ii)
        seen = (CHUNK - 1 - pos)
    else:
        incl, rel = ii >= jj, (ii - jj)
        seen = pos
    relf = jnp.where(incl, rel, 0).astype(F32)
    seenf = seen.astype(F32)
    intra = jnp.where(incl, jnp.exp(relf * lg), 0.0)
    qk = _bdot(q, k, 1, 1) * intra
    q_dec = q * jnp.exp(lg * (seenf + 1.0))
    k_dec = k * jnp.exp(lg * (CHUNK - 1.0 - seenf))
    o = _bdot(q_dec, r) + _bdot(qk, v)
    r_new = r * jnp.exp(lg * CHUNK) + _bdot(k_dec, v, 0, 0)
    return o, r_new


def ret_fwd(qk, v_arr, v_off, logit, heads, direction, o_prev, name):
    bsz, lp, d2 = qk.shape
    d = d2 // 2
    dkh, dvh = d // heads, 2 * d // heads
    nc = lp // CHUNK
    rev = direction == 1
    cm = (lambda n: nc - 1 - n) if rev else (lambda n: n)
    has_prev = o_prev is not None
    v_cb = v_off * LANES // (2 * d)
    assert v_cb * 2 * d == v_off * LANES

    def body(*refs):
        q_ref, k_ref, v_ref, lg_ref = refs[:4]
        prev_ref = refs[4] if has_prev else None
        o_ref, st_ref, r_ref = refs[-3], refs[-2], refs[-1]
        n = pl.program_id(1)

        @pl.when(n == 0)
        def _():
            r_ref[...] = jnp.zeros_like(r_ref)

        lgv = lg_ref[...]
        for h in range(heads):
            ks, vs = slice(h * dkh, (h + 1) * dkh), slice(h * dvh, (h + 1) * dvh)
            r = r_ref[h]
            st_ref[0, 0, h] = r
            o, r_new = _ret_chunk(q_ref[0, :, ks], k_ref[0, :, ks], v_ref[0, :, vs], r, lgv, h, rev)
            r_ref[h] = r_new
            o_ref[0, :, vs] = o + prev_ref[0, :, vs] if has_prev else o

    qblk = pl.BlockSpec((1, CHUNK, d), lambda b, n: (b, cm(n), 0))
    kblk = pl.BlockSpec((1, CHUNK, d), lambda b, n: (b, cm(n), 1))
    vblk = pl.BlockSpec((1, CHUNK, 2 * d), lambda b, n: (b, cm(n), v_cb))
    oblk = pl.BlockSpec((1, CHUNK, 2 * d), lambda b, n: (b, cm(n), 0))
    st_blk = pl.BlockSpec((1, 1, heads, dkh, dvh), lambda b, n: (b, cm(n), 0, 0, 0))
    return pl.pallas_call(
        body, name=name, grid=(bsz, nc),
        in_specs=[qblk, kblk, vblk, pl.BlockSpec((1, LANES), lambda b, n: (0, 0))] + ([oblk] if has_prev else []),
        out_specs=[oblk, st_blk],
        out_shape=[jax.ShapeDtypeStruct((bsz, lp, 2 * d), F32),
                   jax.ShapeDtypeStruct((bsz, nc, heads, dkh, dvh), F32)],
        scratch_shapes=[pltpu.VMEM((heads, dkh, dvh), F32)],
        compiler_params=_params(("parallel", "arbitrary")),
    )(qk, qk, v_arr, logit, *([o_prev] if has_prev else []))


def ret_bwd(qk, v_arr, v_off, logit, states, do, heads, direction, prev, name, dv_dtype=F32):
    bsz, lp, d2 = qk.shape
    d = d2 // 2
    dkh, dvh = d // heads, 2 * d // heads
    nc = lp // CHUNK
    rev = direction == 1
    cm = (lambda n: n) if rev else (lambda n: nc - 1 - n)
    has_prev = prev is not None
    v_cb = v_off * LANES // (2 * d)

    def body(*refs):
        q_ref, k_ref, v_ref, lg_ref, st_ref, do_ref = refs[:6]
        prev_refs = refs[6:8] if has_prev else None
        dqk_ref, dv_ref, dlg_ref, dr_ref = refs[-4:]
        b, n = pl.program_id(0), pl.program_id(1)

        @pl.when(n == 0)
        def _():
            dr_ref[...] = jnp.zeros_like(dr_ref)

        lgv = lg_ref[...]
        dlg = jnp.zeros((1, LANES), F32)
        for h in range(heads):
            ks, vs = slice(h * dkh, (h + 1) * dkh), slice(h * dvh, (h + 1) * dvh)
            f = functools.partial(_ret_chunk, lane=h, rev=rev)
            _, pull = jax.vjp(f, q_ref[0, :, ks], k_ref[0, :, ks], v_ref[0, :, vs], st_ref[0, 0, h], lgv)
            dq, dk, dv, dr, dlg_h = pull((do_ref[0, :, vs], dr_ref[h]))
            dr_ref[h] = dr
            dlg = dlg + dlg_h
            kks = slice(d + h * dkh, d + (h + 1) * dkh)
            if has_prev:
                dq, dk, dv = dq + prev_refs[0][0, :, ks], dk + prev_refs[0][0, :, kks], dv + prev_refs[1][0, :, vs]
            dqk_ref[0, :, ks] = dq
            dqk_ref[0, :, kks] = dk
            dv_ref[0, :, vs] = dv.astype(dv_ref.dtype)
        first = jnp.logical_and(b == 0, n == 0)

        @pl.when(first)
        def _():
            dlg_ref[...] = dlg

        @pl.when(jnp.logical_not(first))
        def _():
            dlg_ref[...] += dlg

    qblk = pl.BlockSpec((1, CHUNK, d), lambda b, n: (b, cm(n), 0))
    kblk = pl.BlockSpec((1, CHUNK, d), lambda b, n: (b, cm(n), 1))
    vblk = pl.BlockSpec((1, CHUNK, 2 * d), lambda b, n: (b, cm(n), v_cb))
    oblk = pl.BlockSpec((1, CHUNK, 2 * d), lambda b, n: (b, cm(n), 0))
    lblk = pl.BlockSpec((1, LANES), lambda b, n: (0, 0))
    st_blk = pl.BlockSpec((1, 1, heads, dkh, dvh), lambda b, n: (b, cm(n), 0, 0, 0))
    return pl.pallas_call(
        body, name=name, grid=(bsz, nc),
        in_specs=[qblk, kblk, vblk, lblk, st_blk, oblk] + ([oblk, oblk] if has_prev else []),
        out_specs=[oblk, oblk, lblk],
        out_shape=[jax.ShapeDtypeStruct((bsz, lp, 2 * d), F32), jax.ShapeDtypeStruct((bsz, lp, 2 * d), dv_dtype),
                   jax.ShapeDtypeStruct((1, LANES), F32)],
        scratch_shapes=[pltpu.VMEM((heads, dkh, dvh), F32)],
        compiler_params=_params(("arbitrary", "arbitrary")),
    )(qk, qk, v_arr, logit, states, do, *(list(prev) if has_prev else []))


def _flip(v, bit):
    return 1 - v if bit else v


def _peer(x, y, c, off):
    return (_flip(x, off & 4), _flip(y, off & 2), _flip(c, off & 1))


def _slot(ref, axis, idx):
    return ref.at[(slice(None),) * axis + (idx,)]


def _slotted_shape(shape, axis):
    return tuple(shape[:axis]) + (N_DEV,) + tuple(shape[axis:])


def all_gather_hbm(blocks, axes, name):
    n = len(blocks)

    def body(*refs):
        x_refs, out_refs = refs[:n], refs[n:2 * n]
        send_sems, recv_sems, local_sems = refs[2 * n:]
        x, y, c = lax.axis_index("x"), lax.axis_index("y"), lax.axis_index("c")
        me, sibling = (x, y, c), (x, y, 1 - c)
        chips = [(1 - x, y), (x, 1 - y), (1 - x, 1 - y)]

        def slot(k, px, py, pc):
            return _slot(out_refs[k], axes[k], 4 * px + 2 * py + pc)

        def copy(k, j, block_of, to, src=None):
            return pltpu.make_async_remote_copy(
                src_ref=slot(k, *block_of) if src is None else src, dst_ref=slot(k, *block_of),
                send_sem=send_sems.at[7 * k + j], recv_sem=recv_sems.at[7 * k + j], device_id=to,
                device_id_type=pl.DeviceIdType.MESH)

        ks = range(n)
        mine = [pltpu.make_async_copy(x_refs[k], slot(k, *me), local_sems.at[k]) for k in ks]
        for cp in mine:
            cp.start()
        first = [copy(k, 0, me, sibling, src=x_refs[k]) for k in ks]
        first += [copy(k, 1 + j, me, (*chip, c), src=x_refs[k]) for k in ks for j, chip in enumerate(chips)]
        for cp in first:
            cp.start()
        passed = []
        for j, chip in enumerate(chips):
            for k in ks:
                copy(k, 1 + j, (*chip, c), me).wait_recv()
                passed.append(copy(k, 4 + j, (*chip, c), sibling))
                passed[-1].start()
        for k in ks:
            copy(k, 0, sibling, me).wait_recv()
        for j, chip in enumerate(chips):
            for k in ks:
                copy(k, 4 + j, (*chip, 1 - c), me).wait_recv()
        for cp in first + passed:
            cp.wait_send()
        for cp in mine:
            cp.wait()

    hbm = pl.BlockSpec(memory_space=pl.ANY)
    return pl.pallas_call(
        body, name=name,
        out_shape=[jax.ShapeDtypeStruct(_slotted_shape(b.shape, ax), b.dtype) for b, ax in zip(blocks, axes)],
        in_specs=[hbm] * n, out_specs=[hbm] * n,
        scratch_shapes=[pltpu.SemaphoreType.DMA((7 * n,)), pltpu.SemaphoreType.DMA((7 * n,)),
                        pltpu.SemaphoreType.DMA((n,))],
    )(*blocks)


def all_gather_small(block, name):
    r, lanes = block.shape

    def body(x_ref, out_ref, send_sems, recv_sems):
        x, y, c = lax.axis_index("x"), lax.axis_index("y"), lax.axis_index("c")
        me = 4 * x + 2 * y + c
        out_ref[me] = x_ref[...]
        copies = []
        for off in range(1, N_DEV):
            copies.append(pltpu.make_async_remote_copy(
                src_ref=x_ref, dst_ref=out_ref.at[me], send_sem=send_sems.at[off - 1], recv_sem=recv_sems.at[off - 1],
                device_id=_peer(x, y, c, off), device_id_type=pl.DeviceIdType.MESH))
        for cp in copies:
            cp.start()
        for cp in copies:
            cp.wait()

    return pl.pallas_call(
        body, name=name, out_shape=jax.ShapeDtypeStruct((N_DEV, r, lanes), block.dtype),
        in_specs=[pl.BlockSpec(memory_space=pltpu.VMEM)], out_specs=pl.BlockSpec(memory_space=pltpu.VMEM),
        scratch_shapes=[pltpu.SemaphoreType.DMA((7,)), pltpu.SemaphoreType.DMA((7,))],
    )(block)


def all_to_all_hbm(pieces, axes, name):
    n = len(pieces)

    def body(*refs):
        x_refs, out_refs = refs[:n], refs[n:2 * n]
        send_sems, recv_sems, local_sems = refs[2 * n:]
        x, y, c = lax.axis_index("x"), lax.axis_index("y"), lax.axis_index("c")
        me = 4 * x + 2 * y + c
        mine = [pltpu.make_async_copy(_slot(x_refs[k], axes[k], me), _slot(out_refs[k], axes[k], me), local_sems.at[k])
                for k in range(n)]
        for cp in mine:
            cp.start()
        copies = []
        for off in range(1, N_DEV):
            px, py, pc = _peer(x, y, c, off)
            for k in range(n):
                copies.append(pltpu.make_async_remote_copy(
                    src_ref=_slot(x_refs[k], axes[k], 4 * px + 2 * py + pc), dst_ref=_slot(out_refs[k], axes[k], me),
                    send_sem=send_sems.at[7 * k + off - 1], recv_sem=recv_sems.at[7 * k + off - 1],
                    device_id=(px, py, pc), device_id_type=pl.DeviceIdType.MESH))
        for cp in copies:
            cp.start()
        for cp in copies:
            cp.wait()
        for cp in mine:
            cp.wait()

    hbm = pl.BlockSpec(memory_space=pl.ANY)
    return pl.pallas_call(
        body, name=name, out_shape=[jax.ShapeDtypeStruct(p.shape, p.dtype) for p in pieces],
        in_specs=[hbm] * n, out_specs=[hbm] * n,
        scratch_shapes=[pltpu.SemaphoreType.DMA((7 * n,)), pltpu.SemaphoreType.DMA((7 * n,)),
                        pltpu.SemaphoreType.DMA((n,))],
    )(*pieces)


def _adamw_update(w, m, v, pieces):
    g = pieces[0].astype(F32)
    for piece in pieces[1:]:
        g = g + piece.astype(F32)
    mn = ADAM_B1 * m + (1.0 - ADAM_B1) * g
    vn = ADAM_B2 * v + (1.0 - ADAM_B2) * (g * g)
    m_hat = mn / (1.0 - ADAM_B1 ** ADAM_STEP)
    v_hat = vn / (1.0 - ADAM_B2 ** ADAM_STEP)
    return g, -ADAM_LR * (m_hat / (jnp.sqrt(v_hat) + ADAM_EPS) + ADAM_WD * w), mn, vn


def adamw_blocks(w, m, v, g8, slot_axis, name):
    nl, r, c = w.shape
    tr = _tile(r, 128, 16)

    def body(w_ref, m_ref, v_ref, g_ref, g_out, d_out, m_out, v_out):
        pieces = [g_ref[s, 0] if slot_axis == 0 else g_ref[0, s] for s in range(N_DEV)]
        for ref, val in zip((g_out, d_out, m_out, v_out), _adamw_update(w_ref[0], m_ref[0], v_ref[0], pieces)):
            ref[0] = val

    blk = pl.BlockSpec((1, tr, c), lambda l, i: (l, i, 0))
    gblk = (pl.BlockSpec((N_DEV, 1, tr, c), lambda l, i: (0, l, i, 0)) if slot_axis == 0
            else pl.BlockSpec((1, N_DEV, tr, c), lambda l, i: (l, 0, i, 0)))
    shp = jax.ShapeDtypeStruct(w.shape, F32)
    return pl.pallas_call(
        body, name=name, grid=(nl, r // tr), in_specs=[blk, blk, blk, gblk],
        out_specs=[blk, blk, blk, blk], out_shape=[shp, shp, shp, shp],
        compiler_params=_params(("parallel", "parallel")),
    )(w, m, v, g8)


def adamw(w, m, v, g8, name):
    r = w.shape[0]
    tm = _tile(r, 1024, 8)

    def body(w_ref, m_ref, v_ref, g_ref, g_out, d_out, m_out, v_out):
        res = _adamw_update(w_ref[...], m_ref[...], v_ref[...], [g_ref[s] for s in range(N_DEV)])
        for ref, val in zip((g_out, d_out, m_out, v_out), res):
            ref[...] = val

    blk = pl.BlockSpec((tm, LANES), lambda i: (i, 0))
    shp = jax.ShapeDtypeStruct((r, LANES), F32)
    return pl.pallas_call(
        body, name=name, grid=(r // tm,),
        in_specs=[blk, blk, blk, pl.BlockSpec((N_DEV, tm, LANES), lambda i: (0, i, 0))],
        out_specs=[blk, blk, blk, blk], out_shape=[shp, shp, shp, shp],
        compiler_params=_params(("parallel",)),
    )(w, m, v, g8)


def _pack(blocks, rows_mult):
    flat = jnp.concatenate([b.reshape(-1) for b in blocks])
    unit = rows_mult * LANES
    total = -(-flat.shape[0] // unit) * unit
    return jnp.pad(flat, (0, total - flat.shape[0])).reshape(-1, LANES)


def _unpack(packed, shapes):
    flat = packed.reshape(-1)
    out, pos = [], 0
    for s in shapes:
        n = math.prod(s)
        out.append(flat[pos:pos + n].reshape(s))
        pos += n
    return out


def _gathered_full(gathered, shapes, axes):
    per_dev = [_unpack(gathered[d], shapes) for d in range(N_DEV)]
    return [jnp.concatenate([per_dev[d][k] for d in range(N_DEV)], axis=axes[k]) for k in range(len(shapes))]


def _pieces_by_dest(fulls, axes, rows_mult):
    packs = []
    for d in range(N_DEV):
        blocks = []
        for f, ax in zip(fulls, axes):
            n = f.shape[ax] // N_DEV
            blocks.append(lax.slice_in_dim(f, d * n, (d + 1) * n, axis=ax))
        packs.append(_pack(blocks, rows_mult))
    return jnp.stack(packs)


class Layout:
    def __init__(self, d):
        self.d = d
        self.h = d // 128
        self.hr = d // 256
        self.z = 3 * d
        self.qb = 4 * d
        self.vb = 6 * d
        self.gb = 8 * d
        self.ga = 10 * d
        self.gbt = 11 * d
        self.ab = 12 * d
        self.used = 12 * d + LANES
        self.np = -(-self.used // 512) * 512

    def relayout_w_in(self, w):
        d, h4 = self.d, 4 * self.h
        return jnp.concatenate([w[:, :4 * d], w[:, 4 * d + h4:], w[:, 4 * d:4 * d + h4],
                                jnp.zeros((d, self.np - 12 * d - h4), w.dtype)], axis=1)

    def unlayout_w_in(self, w):
        d, h4 = self.d, 4 * self.h
        return jnp.concatenate([w[:, :4 * d], w[:, 12 * d:12 * d + h4], w[:, 4 * d:12 * d]], axis=1)


def _lane_row(vec):
    return jnp.pad(vec.reshape(-1), (0, LANES - vec.size)).reshape(1, LANES)


def _rope_tables(lp, half):
    inv = ROPE_BASE ** (-jnp.arange(half, dtype=F32) / half)
    pos = (jnp.arange(lp) - N_PAD).astype(F32)
    ang = pos[:, None] * inv[None, :]
    return jnp.cos(ang), jnp.sin(ang)


def local_step(x, target, meta, p):
    bsz, seq, d = x.shape
    lay = Layout(d)
    h_gdn, h_ret = lay.h, lay.hr
    lp = seq + CHUNK
    t_all = bsz * lp
    depth = p["w_up_a"].shape[0]
    ff = p["w_ffn_out"].shape[1]
    tm = _tile(lp, 512, 16)
    tmw = _tile(lp, 256, 16)
    tmn = _tile(lp, 1040, 16)
    tmr = _tile(lp, 832, 16)
    cb = lambda cols: cols // LANES
    flat = lambda a: a.reshape(t_all, a.shape[-1])
    unflat = lambda a: a.reshape(bsz, lp, a.shape[-1])
    cos, sin = _rope_tables(lp, LANES)
    f_gb = make_f_gb(h_gdn)
    rope_f, rope_b = make_f_rope(1.0), make_f_rope(-1.0)

    head = jnp.concatenate([jnp.zeros((N_PAD, d), F32), meta], axis=0)
    h = jnp.concatenate([jnp.broadcast_to(head[None], (bsz, CHUNK, d)), x], axis=1)
    tgt = jnp.pad(target, ((0, 0), (CHUNK, 0), (0, 0)))

    saved = []
    for l in range(depth):
        s = {"h_in": h}
        nm = lambda k: f"l{l}_{k}"
        g_mix, g_ffn = p["norm_mix"][l][None], p["norm_ffn"][l][None]
        alog, dtb = _lane_row(p["gdn_a_log"][l]), _lane_row(p["gdn_dt_bias"][l])
        gain_a = p["gdn_norm"][l][None]
        logits = [_lane_row(p["ret_decay_logit"][l][0]), _lane_row(p["ret_decay_logit"][l][1])]
        cw = p["conv_w"][l]
        (hn,) = rowwise(nm("rms_mix"), f_rms, [Row(h, d)], [], [g_mix], [(d, d)], 1, tm, BF16)
        proj = unflat(matmul(flat(hn), p["w_in"][l], name=nm("mm_in")))
        qa = conv_fwd(proj, 0, cw[:, :d], True, nm("conv_q"))
        ka = conv_fwd(proj, cb(d), cw[:, d:2 * d], True, nm("conv_k"))
        va = conv_fwd(proj, cb(2 * d), cw[:, 2 * d:], False, nm("conv_v"))
        (gb,) = rowwise(nm("gb"), f_gb, [Row(proj, LANES, cb(lay.ab))], [], [alog, dtb], [(LANES, LANES)], 1, tmn)
        o0, st_a0, iv_a0 = gdn_fwd(qa, ka, va, gb, h_gdn, 0, None, nm("gdn_f0"))
        oa, st_a1, iv_a1 = gdn_fwd(qa, ka, va, gb, h_gdn, 1, o0, nm("gdn_f1"))
        (oan,) = rowwise(nm("gdn_out"), f_gdn_out, [Row(oa, LANES), Row(proj, LANES, cb(lay.z))], [], [gain_a],
                         [(d, LANES)], h_gdn, tmn, BF16)
        ya = unflat(matmul(flat(oan), p["w_up_a"][l], name=nm("mm_up_a")))
        (qkr,) = rowwise(nm("rope"), rope_f, [Row(proj, 2 * LANES, cb(lay.qb) // 2)], [Tab(cos), Tab(sin)], [],
                         [(2 * d, 2 * LANES)], 2 * h_ret, tmn)
        r0, st_b0 = ret_fwd(qkr, proj, cb(lay.vb), logits[0], h_ret, 0, None, nm("ret_f0"))
        ob, st_b1 = ret_fwd(qkr, proj, cb(lay.vb), logits[1], h_ret, 1, r0, nm("ret_f1"))
        (obn,) = rowwise(nm("ret_out"), f_ret_out, [Row(ob, 4 * LANES), Row(proj, 4 * LANES, cb(lay.gb) // 4)], [], [],
                         [(2 * d, 4 * LANES)], h_ret, tmr, BF16)
        yb = unflat(matmul(flat(obn), p["w_up_b"][l], name=nm("mm_up_b")))
        (mg,) = rowwise(nm("merge"), f_merge,
                        [Row(proj, LANES, cb(lay.ga)), Row(proj, LANES, cb(lay.gbt)), Row(ya, LANES), Row(yb, LANES)],
                        [], [], [(d, LANES)], cb(d), tmn, BF16)
        h_mid = unflat(matmul(flat(mg), p["w_out"][l], add=flat(h), name=nm("mm_out")))
        (hn2,) = rowwise(nm("rms_ffn"), f_rms, [Row(h_mid, d)], [], [g_ffn], [(d, d)], 1, tm, BF16)
        ffp = unflat(matmul(flat(hn2), p["w_ffn_in"][l], name=nm("mm_ffn_in")))
        (act,) = rowwise(nm("swiglu"), f_swiglu, [Row(ffp, 2 * ff)], [], [], [(ff, ff)], 1, tmw, BF16)
        h = unflat(matmul(flat(act), p["w_ffn_out"][l], add=flat(h_mid), name=nm("mm_ffn_out")))
        s.update(hn=hn, proj=proj, qa=qa, ka=ka, va=va, gb=gb, st_a=(st_a0, st_a1), iv_a=(iv_a0, iv_a1), oa=oa,
                 oan=oan, ya=ya, qkr=qkr, st_b=(st_b0, st_b1), ob=ob, obn=obn, yb=yb, mg=mg, h_mid=h_mid, hn2=hn2,
                 ffp=ffp, act=act, logits=logits, alog=alog, dtb=dtb, gain_a=gain_a, cw=cw, g_mix=g_mix, g_ffn=g_ffn)
        saved.append(s)

    dh, d_final, loss_row = loss_head(h, p["norm_final"][None], tgt, tm)

    grads = {k: [None] * depth for k in ("norm_mix", "w_in", "conv_w", "gdn_a_log", "gdn_dt_bias", "gdn_norm",
                                          "ret_decay_logit", "w_up_a", "w_up_b", "w_out", "norm_ffn", "w_ffn_in",
                                          "w_ffn_out")}
    for l in reversed(range(depth)):
        s = saved[l]
        nm = lambda k: f"l{l}_{k}"
        proj = s["proj"]
        dhf = flat(dh)
        grads["w_ffn_out"][l] = matmul(flat(s["act"]), dhf, ta=True, name=nm("mmg_ffn_out"))
        dact = unflat(matmul(dhf, p["w_ffn_out"][l], tb=True, name=nm("mmb_ffn_out")))
        (dffp,), _ = rowwise_vjp(nm("swiglu_b"), f_swiglu, [Row(s["ffp"], 2 * ff)], [], [], [dact], 1, tmw, narrow=(0,))
        grads["w_ffn_in"][l] = matmul(flat(s["hn2"]), flat(dffp), ta=True, name=nm("mmg_ffn_in"))
        dhn2 = unflat(matmul(flat(dffp), p["w_ffn_in"][l], tb=True, name=nm("mmb_ffn_in")))
        (dh_mid,), (dg_ffn,) = rowwise_vjp(nm("rms_ffn_b"), f_rms, [Row(s["h_mid"], d)], [], [s["g_ffn"]], [dhn2], 1, tmw,
                                           adds={0: dh})
        grads["norm_ffn"][l] = dg_ffn[0]
        dmf = flat(dh_mid)
        grads["w_out"][l] = matmul(flat(s["mg"]), dmf, ta=True, name=nm("mmg_out"))
        dmg = unflat(matmul(dmf, p["w_out"][l], tb=True, name=nm("mmb_out")))
        (dga, dgbt, dya, dyb), _ = rowwise_vjp(
            nm("merge_b"), f_merge,
            [Row(proj, LANES, cb(lay.ga)), Row(proj, LANES, cb(lay.gbt)), Row(s["ya"], LANES), Row(s["yb"], LANES)],
            [], [], [dmg], cb(d), tmn, narrow=(0, 1, 2, 3))
        grads["w_up_b"][l] = matmul(flat(s["obn"]), flat(dyb), ta=True, name=nm("mmg_up_b"))
        dobn = unflat(matmul(flat(dyb), p["w_up_b"][l], tb=True, name=nm("mmb_up_b")))
        (dob, dg_b), _ = rowwise_vjp(nm("ret_out_b"), f_ret_out,
                                     [Row(s["ob"], 4 * LANES), Row(proj, 4 * LANES, cb(lay.gb) // 4)], [], [], [dobn],
                                     h_ret, tmr, narrow=(1,))
        r1 = ret_bwd(s["qkr"], proj, cb(lay.vb), s["logits"][1], s["st_b"][1], dob, h_ret, 1, None, nm("ret_b1"))
        r0 = ret_bwd(s["qkr"], proj, cb(lay.vb), s["logits"][0], s["st_b"][0], dob, h_ret, 0, r1[:2], nm("ret_b0"),
                     dv_dtype=BF16)
        (dqk,) = rowwise(nm("rope_b"), rope_b, [Row(r0[0], 2 * LANES)], [Tab(cos), Tab(sin)], [],
                         [(2 * d, 2 * LANES)], 2 * h_ret, tmn, BF16)
        dv_b = r0[1]
        grads["ret_decay_logit"][l] = jnp.stack([r0[2][0, :h_ret], r1[2][0, :h_ret]])
        grads["w_up_a"][l] = matmul(flat(s["oan"]), flat(dya), ta=True, name=nm("mmg_up_a"))
        doan = unflat(matmul(flat(dya), p["w_up_a"][l], tb=True, name=nm("mmb_up_a")))
        (doa, dz), (dgain_a,) = rowwise_vjp(nm("gdn_out_b"), f_gdn_out,
                                            [Row(s["oa"], LANES), Row(proj, LANES, cb(lay.z))], [], [s["gain_a"]],
                                            [doan], h_gdn, tmn, narrow=(1,))
        grads["gdn_norm"][l] = dgain_a[0]
        a1 = gdn_bwd(s["qa"], s["ka"], s["va"], s["gb"], s["st_a"][1], s["iv_a"][1], doa, h_gdn, 1, None, nm("gdn_b1"))
        a0 = gdn_bwd(s["qa"], s["ka"], s["va"], s["gb"], s["st_a"][0], s["iv_a"][0], doa, h_gdn, 0, a1, nm("gdn_b0"))
        (dab,), (dalog, ddtb) = rowwise_vjp(nm("gb_b"), f_gb, [Row(proj, LANES, cb(lay.ab))], [], [s["alog"], s["dtb"]],
                                            [a0[3]], 1, tmn, narrow=(0,))
        grads["gdn_a_log"][l] = dalog[0, :2 * h_gdn].reshape(2, h_gdn)
        grads["gdn_dt_bias"][l] = ddtb[0, :2 * h_gdn].reshape(2, h_gdn)
        cw = s["cw"]
        dxq, dwq = conv_bwd(proj, 0, cw[:, :d], a0[0], True, nm("conv_q_b"))
        dxk, dwk = conv_bwd(proj, cb(d), cw[:, d:2 * d], a0[1], True, nm("conv_k_b"))
        dxv, dwv = conv_bwd(proj, cb(2 * d), cw[:, 2 * d:], a0[2], False, nm("conv_v_b"))
        grads["conv_w"][l] = jnp.concatenate([dwq, dwk, dwv], axis=1)
        dproj = jnp.concatenate([dxq, dxk, dxv, dz, dqk, dv_b, dg_b, dga, dgbt, dab,
                                 jnp.zeros((bsz, lp, lay.np - lay.used), BF16)], axis=-1)
        grads["w_in"][l] = matmul(flat(s["hn"]), flat(dproj), ta=True, name=nm("mmg_in"))
        dhn = unflat(matmul(flat(dproj), p["w_in"][l], tb=True, name=nm("mmb_in")))
        (dh,), (dg_mix,) = rowwise_vjp(nm("rms_mix_b"), f_rms, [Row(s["h_in"], d)], [], [s["g_mix"]], [dhn], 1, tmw,
                                       adds={0: dh_mid})
        grads["norm_mix"][l] = dg_mix[0]

    out = {k: jnp.stack(v) for k, v in grads.items()}
    out["norm_final"] = d_final[0]
    grad_x = dh[:, CHUNK:]
    grad_meta = jnp.sum(dh[:, N_PAD:CHUNK], axis=0)
    return loss_row, grad_x, grad_meta, out


BIG = ("w_in", "w_up_a", "w_up_b", "w_out", "w_ffn_in", "w_ffn_out")
BIG_AXES = (2, 1, 1, 1, 2, 1)
SMALL_SHARDED = ("meta_tokens", "conv_w")
SMALL_AXES = (1, 2)
REPLICATED = ("norm_mix", "gdn_a_log", "gdn_dt_bias", "gdn_norm", "ret_decay_logit", "norm_ffn", "norm_final")
WEIGHTS = ("meta_tokens", "norm_mix", "w_in", "conv_w", "gdn_a_log", "gdn_dt_bias", "gdn_norm", "ret_decay_logit",
           "w_up_a", "w_up_b", "w_out", "norm_ffn", "w_ffn_in", "w_ffn_out", "norm_final")


def kernel(x, meta_tokens, norm_mix, w_in, conv_w, gdn_a_log, gdn_dt_bias, gdn_norm, ret_decay_logit, w_up_a, w_up_b, w_out, norm_ffn, w_ffn_in, w_ffn_out, norm_final, loss_target, m_meta_tokens, m_norm_mix, m_w_in, m_conv_w, m_gdn_a_log, m_gdn_dt_bias, m_gdn_norm, m_ret_decay_logit, m_w_up_a, m_w_up_b, m_w_out, m_norm_ffn, m_w_ffn_in, m_w_ffn_out, m_norm_final, v_meta_tokens, v_norm_mix, v_w_in, v_conv_w, v_gdn_a_log, v_gdn_dt_bias, v_gdn_norm, v_ret_decay_logit, v_w_up_a, v_w_up_b, v_w_out, v_norm_ffn, v_w_ffn_in, v_w_ffn_out, v_norm_final):
    w = dict(meta_tokens=meta_tokens, norm_mix=norm_mix, w_in=w_in, conv_w=conv_w, gdn_a_log=gdn_a_log,
             gdn_dt_bias=gdn_dt_bias, gdn_norm=gdn_norm, ret_decay_logit=ret_decay_logit, w_up_a=w_up_a,
             w_up_b=w_up_b, w_out=w_out, norm_ffn=norm_ffn, w_ffn_in=w_ffn_in, w_ffn_out=w_ffn_out,
             norm_final=norm_final)
    m = dict(meta_tokens=m_meta_tokens, norm_mix=m_norm_mix, w_in=m_w_in, conv_w=m_conv_w, gdn_a_log=m_gdn_a_log,
             gdn_dt_bias=m_gdn_dt_bias, gdn_norm=m_gdn_norm, ret_decay_logit=m_ret_decay_logit, w_up_a=m_w_up_a,
             w_up_b=m_w_up_b, w_out=m_w_out, norm_ffn=m_norm_ffn, w_ffn_in=m_w_ffn_in, w_ffn_out=m_w_ffn_out,
             norm_final=m_norm_final)
    v = dict(meta_tokens=v_meta_tokens, norm_mix=v_norm_mix, w_in=v_w_in, conv_w=v_conv_w, gdn_a_log=v_gdn_a_log,
             gdn_dt_bias=v_gdn_dt_bias, gdn_norm=v_gdn_norm, ret_decay_logit=v_ret_decay_logit, w_up_a=v_w_up_a,
             w_up_b=v_w_up_b, w_out=v_w_out, norm_ffn=v_norm_ffn, w_ffn_in=v_w_ffn_in, w_ffn_out=v_w_ffn_out,
             norm_final=v_norm_final)
    d = x.shape[-1]
    lay = Layout(d)
    slot_axes = [1 if ax == 1 else 0 for ax in BIG_AXES]

    gathered = all_gather_hbm([w[k].astype(BF16) for k in BIG], slot_axes, "gather_weights")
    full = {}
    for k, gk, sa in zip(BIG, gathered, slot_axes):
        if sa == 1:
            full[k] = gk.reshape(gk.shape[0], -1, gk.shape[3])
        else:
            full[k] = jnp.concatenate([gk[s] for s in range(N_DEV)], axis=2)
    small_shapes = [w[k].shape for k in SMALL_SHARDED]
    gathered_s = all_gather_small(_pack([w[k] for k in SMALL_SHARDED], 8), "gather_small")
    full.update(zip(SMALL_SHARDED, _gathered_full(gathered_s, small_shapes, SMALL_AXES)))
    p = {k: w[k] for k in REPLICATED}
    p.update({k: full[k] for k in BIG + ("conv_w",)})
    p["w_in"] = jnp.stack([lay.relayout_w_in(full["w_in"][l]) for l in range(full["w_in"].shape[0])])

    loss_row, grad_x, grad_meta, g = local_step(x, loss_target, full["meta_tokens"], p)
    g["meta_tokens"] = grad_meta
    g["w_in"] = jnp.stack([lay.unlayout_w_in(g["w_in"][l]) for l in range(g["w_in"].shape[0])])

    pieces = []
    for k, sa in zip(BIG, slot_axes):
        nl, rows, cols = g[k].shape
        if sa == 1:
            pieces.append(g[k].astype(BF16).reshape(nl, N_DEV, rows // N_DEV, cols))
        else:
            cs = cols // N_DEV
            pieces.append(jnp.stack([g[k][:, :, s * cs:(s + 1) * cs] for s in range(N_DEV)]).astype(BF16))
    small_pieces = _pieces_by_dest([g[k] for k in SMALL_SHARDED], SMALL_AXES, 8)
    received = all_to_all_hbm(pieces + [small_pieces], slot_axes + [0], "exchange_grads")
    outs = {kind: {} for kind in ("g", "d", "m", "v")}
    for k, sa, got in zip(BIG, slot_axes, received):
        for kind, r in zip(("g", "d", "m", "v"), adamw_blocks(w[k], m[k], v[k], got, sa, "adamw_" + k)):
            outs[kind][k] = r
    res = adamw(_pack([w[k] for k in SMALL_SHARDED], 8), _pack([m[k] for k in SMALL_SHARDED], 8),
                _pack([v[k] for k in SMALL_SHARDED], 8), received[-1], "adamw_small")
    for kind, r in zip(("g", "d", "m", "v"), res):
        outs[kind].update(zip(SMALL_SHARDED, _unpack(r, small_shapes)))

    rep_shapes = [w[k].shape for k in REPLICATED]
    part = _pack([g[k] for k in REPLICATED] + [loss_row[0, :1]], 8)
    parts = all_gather_small(part, "gather_replicated")
    pad1 = lambda a: _pack([a[k] for k in REPLICATED] + [jnp.zeros((1,), F32)], 8)
    res_r = adamw(pad1(w), pad1(m), pad1(v), parts, "adamw_replicated")
    for kind, r in zip(("g", "d", "m", "v"), res_r):
        outs[kind].update(zip(REPLICATED, _unpack(r, rep_shapes)))
    loss = res_r[0].reshape(-1)[sum(math.prod(s) for s in rep_shapes)]

    return (loss, grad_x, *[outs["g"][k] for k in WEIGHTS], *[outs["d"][k] for k in WEIGHTS],
            *[outs["m"][k] for k in WEIGHTS], *[outs["v"][k] for k in WEIGHTS])
```

```python
import functools
import math

import jax
import jax.numpy as jnp
from jax import lax
from jax.experimental import pallas as pl
from jax.experimental.pallas import tpu as pltpu

F32 = jnp.float32
BF16 = jnp.bfloat16
HIGHEST = lax.Precision.HIGHEST

LANES = 128
CHUNK = 64
N_META = 16
N_PAD = CHUNK - N_META
CONV_K = 5
EPS = 1e-6
ROPE_BASE = 10000.0
N_DEV = 8
VMEM_LIMIT = 56 * 1024 * 1024
MATMUL_VMEM = 40 * 1024 * 1024

ADAM_LR, ADAM_B1, ADAM_B2, ADAM_EPS, ADAM_WD, ADAM_STEP = 0.001, 0.9, 0.999, 1e-08, 0.01, 10


def _tile(n, cap, mult):
    if n <= cap:
        return n
    best = None
    for t in range(mult, cap + 1, mult):
        if n % t == 0:
            best = t
    assert best is not None, (n, cap, mult)
    return best


def _params(sem):
    return pltpu.CompilerParams(dimension_semantics=sem, vmem_limit_bytes=VMEM_LIMIT)


def _raw_dot(a, b, ca, cb, exact):
    dims = (((ca,), (cb,)), ((), ()))
    a_hi, b_hi = a.astype(BF16), b.astype(BF16)
    out = lax.dot_general(a_hi, b_hi, dims, preferred_element_type=F32)
    if exact:
        a_lo = (a - a_hi.astype(F32)).astype(BF16)
        b_lo = (b - b_hi.astype(F32)).astype(BF16)
        out = out + lax.dot_general(a_hi, b_lo, dims, preferred_element_type=F32)
        out = out + lax.dot_general(a_lo, b_hi, dims, preferred_element_type=F32)
    return out


@functools.partial(jax.custom_vjp, nondiff_argnums=(2, 3, 4))
def _dot(a, b, ca, cb, exact):
    return _raw_dot(a, b, ca, cb, exact)


def _dot_fwd(a, b, ca, cb, exact):
    return _raw_dot(a, b, ca, cb, exact), (a, b)


def _dot_bwd(ca, cb, exact, res, g):
    a, b = res
    if ca == 1:
        da = _raw_dot(g, b, 1, 1 if cb == 0 else 0, exact)
    else:
        da = _raw_dot(b, g, 1 if cb == 0 else 0, 1, exact)
    if cb == 0:
        db = _raw_dot(a, g, 0 if ca == 1 else 1, 0, exact)
    else:
        db = _raw_dot(g, a, 0, 0 if ca == 1 else 1, exact)
    return da, db


_dot.defvjp(_dot_fwd, _dot_bwd)


@functools.partial(jax.custom_vjp, nondiff_argnums=(1, 2))
def _split(x, n, axis):
    return lax.slice_in_dim(x, 0, n, axis=axis), lax.slice_in_dim(x, n, x.shape[axis], axis=axis)


_split.defvjp(lambda x, n, axis: (_split(x, n, axis), None),
              lambda n, axis, _, g: (jnp.concatenate([g[0], g[1]], axis=axis),))


def _bdot(a, b, ca=1, cb=0):
    return _dot(a, b, ca, cb, False)


def _xdot(a, b, ca=1, cb=0):
    return _dot(a, b, ca, cb, True)


def matmul(a, b, *, ta=False, tb=False, add=None, name):
    m, k = (a.shape[1], a.shape[0]) if ta else a.shape
    k2, n = (b.shape[1], b.shape[0]) if tb else b.shape
    assert k == k2, (a.shape, b.shape, ta, tb)
    has_add = add is not None
    tm_cap, tn_cap = 2080, 1408
    while True:
        tm = _tile(m, tm_cap, 128 if ta else 16)
        tn = _tile(n, tn_cap, 128)
        tk = _tile(k, 1664, 128 if (not ta or tb) else 16)
        nk = k // tk
        need = 2 * (tm * tk * a.dtype.itemsize + tk * tn * b.dtype.itemsize + tm * tn * 4 * (2 if has_add else 1))
        need += tm * tn * 4 if nk > 1 else 0
        if need <= MATMUL_VMEM or (tm_cap <= 256 and tn_cap <= 256):
            break
        if tm_cap >= tn_cap:
            tm_cap //= 2
        else:
            tn_cap //= 2
    ca, cb = (0 if ta else 1), (1 if tb else 0)
    a_spec = pl.BlockSpec((tk, tm), lambda i, j, kk: (kk, i)) if ta else pl.BlockSpec((tm, tk), lambda i, j, kk: (i, kk))
    b_spec = pl.BlockSpec((tn, tk), lambda i, j, kk: (j, kk)) if tb else pl.BlockSpec((tk, tn), lambda i, j, kk: (kk, j))
    o_spec = pl.BlockSpec((tm, tn), lambda i, j, kk: (i, j))

    def body(*refs):
        a_ref, b_ref = refs[0], refs[1]
        add_ref = refs[2] if has_add else None
        o_ref = refs[3] if has_add else refs[2]
        def part():
            return _raw_dot(a_ref[...], b_ref[...], ca, cb, False)

        if nk == 1:
            o_ref[...] = part() + add_ref[...] if has_add else part()
            return
        acc_ref = refs[-1]
        kk = pl.program_id(2)

        @pl.when(kk == 0)
        def _():
            acc_ref[...] = part()

        @pl.when(jnp.logical_and(kk > 0, kk < nk - 1))
        def _():
            acc_ref[...] += part()

        @pl.when(kk == nk - 1)
        def _():
            o_ref[...] = acc_ref[...] + part() + add_ref[...] if has_add else acc_ref[...] + part()

    ins = [a, b] + ([add] if has_add else [])
    in_specs = [a_spec, b_spec] + ([o_spec] if has_add else [])
    return pl.pallas_call(
        body, name=name, grid=(m // tm, n // tn, nk), in_specs=in_specs, out_specs=o_spec,
        out_shape=jax.ShapeDtypeStruct((m, n), F32),
        scratch_shapes=[pltpu.VMEM((tm, tn), F32)] if nk > 1 else [],
        compiler_params=_params(("parallel", "parallel", "arbitrary")),
    )(*ins)


class Row:
    def __init__(self, arr, bc, off=0, per_head=True, diff=True):
        self.arr, self.bc, self.off, self.per_head, self.diff = arr, bc, off, per_head, diff


class Tab:
    def __init__(self, arr):
        self.arr = arr


def _row_specs(rows, tabs, pars, tm):
    specs = []
    for r in rows:
        specs.append(pl.BlockSpec((1, tm, r.bc), functools.partial(
            lambda b, i, h, off, ph: (b, i, off + (h if ph else 0)), off=r.off, ph=r.per_head)))
    for t in tabs:
        specs.append(pl.BlockSpec((tm, t.arr.shape[1]), lambda b, i, h: (i, 0)))
    for p in pars:
        specs.append(pl.BlockSpec(p.shape, lambda b, i, h: (0, 0)))
    return specs


def rowwise(name, fn, rows, tabs, pars, outs, nh, tm, out_dtype=F32):
    bsz, lp = rows[0].arr.shape[:2]
    nr, nt, npar = len(rows), len(tabs), len(pars)

    def body(*refs):
        t0 = pl.program_id(1) * tm
        ins = [refs[k][0] for k in range(nr)] + [refs[nr + k][...] for k in range(nt + npar)]
        res = fn(t0, *ins)
        for o_ref, o in zip(refs[nr + nt + npar:], res):
            o_ref[0] = o.astype(o_ref.dtype)

    return pl.pallas_call(
        body, name=name, grid=(bsz, lp // tm, nh),
        in_specs=_row_specs(rows, tabs, pars, tm),
        out_specs=[pl.BlockSpec((1, tm, bc), lambda b, i, h: (b, i, h)) for _, bc in outs],
        out_shape=[jax.ShapeDtypeStruct((bsz, lp, c), out_dtype) for c, _ in outs],
        compiler_params=_params(("parallel", "parallel", "parallel")),
    )(*[r.arr for r in rows], *[t.arr for t in tabs], *pars)


def rowwise_vjp(name, fn, rows, tabs, pars, couts, nh, tm, adds=None, narrow=()):
    bsz, lp = rows[0].arr.shape[:2]
    nr, nt, npar, nco = len(rows), len(tabs), len(pars), len(couts)
    adds = adds or {}
    add_keys = sorted(adds)
    diff_idx = [k for k, r in enumerate(rows) if r.diff]
    for k in diff_idx:
        assert rows[k].per_head or nh == 1

    def body(*refs):
        b, i, h = pl.program_id(0), pl.program_id(1), pl.program_id(2)
        t0 = i * tm
        pos = 0
        row_v = [refs[k][0] for k in range(nr)]
        pos += nr
        tab_v = [refs[pos + k][...] for k in range(nt)]
        pos += nt
        par_v = [refs[pos + k][...] for k in range(npar)]
        pos += npar
        co_v = [refs[pos + k][0] for k in range(nco)]
        pos += nco
        add_v = {key: refs[pos + k][0] for k, key in enumerate(add_keys)}
        pos += len(add_keys)
        drow_refs = refs[pos:pos + len(diff_idx)]
        dpar_refs = refs[pos + len(diff_idx):]

        def f(dvals, pvals):
            full = list(row_v)
            for k, v in zip(diff_idx, dvals):
                full[k] = v
            return tuple(fn(t0, *full, *tab_v, *pvals))

        _, pull = jax.vjp(f, [row_v[k] for k in diff_idx], par_v)
        d_rows, d_pars = pull(tuple(co_v))
        for ref, k, d in zip(drow_refs, diff_idx, d_rows):
            ref[0] = (d + add_v[k] if k in add_v else d).astype(ref.dtype)
        first = jnp.logical_and(jnp.logical_and(b == 0, i == 0), h == 0)
        for ref, d in zip(dpar_refs, d_pars):
            @pl.when(first)
            def _(ref=ref, d=d):
                ref[...] = d

            @pl.when(jnp.logical_not(first))
            def _(ref=ref, d=d):
                ref[...] += d

    out_block = lambda bc: pl.BlockSpec((1, tm, bc), lambda b, i, h: (b, i, h))
    in_specs = _row_specs(rows, tabs, pars, tm)
    in_specs += [out_block(c.shape[2] // nh) for c in couts]
    in_specs += [out_block(rows[k].bc) for k in add_keys]
    out_specs = [out_block(rows[k].bc) for k in diff_idx]
    out_specs += [pl.BlockSpec(p.shape, lambda b, i, h: (0, 0)) for p in pars]
    out_shape = [jax.ShapeDtypeStruct((bsz, lp, nh * rows[k].bc), BF16 if k in narrow else F32) for k in diff_idx]
    out_shape += [jax.ShapeDtypeStruct(p.shape, F32) for p in pars]
    res = pl.pallas_call(
        body, name=name, grid=(bsz, lp // tm, nh), in_specs=in_specs, out_specs=out_specs, out_shape=out_shape,
        compiler_params=_params(("arbitrary", "arbitrary", "arbitrary")),
    )(*[r.arr for r in rows], *[t.arr for t in tabs], *pars, *couts, *[adds[k] for k in add_keys])
    return res[:len(diff_idx)], res[len(diff_idx):]


def _real_rows(t0, tm):
    return (t0 + lax.broadcasted_iota(jnp.int32, (tm, 1), 0)) >= N_PAD


def f_rms(t0, x, gain):
    return (x * lax.rsqrt(jnp.mean(x * x, axis=-1, keepdims=True) + EPS) * gain,)


def make_f_gb(heads):
    def f_gb(t0, ab, alog, dtb):
        lane = lax.broadcasted_iota(jnp.int32, ab.shape, 1)
        g = -jnp.exp(alog) * jax.nn.softplus(ab + dtb)
        beta = jax.nn.sigmoid(ab)
        out = jnp.where(lane < 2 * heads, g, jnp.where(lane < 4 * heads, beta, 0.0))
        return (jnp.where(_real_rows(t0, ab.shape[0]), out, 0.0),)
    return f_gb


def f_gdn_out(t0, o, z, gain):
    on = o * lax.rsqrt(jnp.mean(o * o, axis=-1, keepdims=True) + EPS)
    return (on * gain * jax.nn.silu(z),)


def f_ret_out(t0, o, g):
    on = o * lax.rsqrt(jnp.mean(o * o, axis=-1, keepdims=True) + EPS)
    return (on * jax.nn.silu(g),)


def f_merge(t0, ga, gb, ya, yb):
    return (jax.nn.sigmoid(ga) * ya + jax.nn.sigmoid(gb) * yb,)


def f_swiglu(t0, x):
    gate, up = _split(x, x.shape[1] // 2, 1)
    return (jax.nn.silu(gate) * up,)


def make_f_rope(sign):
    def f_rope(t0, x, cos, sin):
        half = x.shape[1] // 2
        x1, x2 = x[:, :half], x[:, half:]
        s = sin * sign
        return (jnp.concatenate([x1 * cos - x2 * s, x1 * s + x2 * cos], axis=1),)
    return f_rope


def loss_head(h, gain, target, tm):
    bsz, lp, d = h.shape

    def body(h_ref, g_ref, t_ref, dh_ref, dg_ref, loss_ref):
        b, i = pl.program_id(0), pl.program_id(1)
        rows = (i * tm + lax.broadcasted_iota(jnp.int32, (tm, 1), 0)) >= CHUNK
        tgt = t_ref[0]

        def f(x, gain_v):
            y = f_rms(0, x, gain_v)[0]
            err = jnp.where(rows, y - tgt, 0.0)
            return 0.5 * jnp.sum(jnp.mean(err * err, axis=-1, keepdims=True), keepdims=True)

        val, pull = jax.vjp(f, h_ref[0], g_ref[...])
        dh, dg = pull(jnp.ones((1, 1), F32))
        dh_ref[0] = dh
        first = jnp.logical_and(b == 0, i == 0)
        val_row = jnp.broadcast_to(val, (1, LANES))

        @pl.when(first)
        def _():
            dg_ref[...] = dg
            loss_ref[...] = val_row

        @pl.when(jnp.logical_not(first))
        def _():
            dg_ref[...] += dg
            loss_ref[...] += val_row

    blk = pl.BlockSpec((1, tm, d), lambda b, i: (b, i, 0))
    return pl.pallas_call(
        body, name="loss_head", grid=(bsz, lp // tm),
        in_specs=[blk, pl.BlockSpec((1, d), lambda b, i: (0, 0)), blk],
        out_specs=[blk, pl.BlockSpec((1, d), lambda b, i: (0, 0)), pl.BlockSpec((1, LANES), lambda b, i: (0, 0))],
        out_shape=[jax.ShapeDtypeStruct((bsz, lp, d), F32), jax.ShapeDtypeStruct((1, d), F32),
                   jax.ShapeDtypeStruct((1, LANES), F32)],
        compiler_params=_params(("arbitrary", "arbitrary")),
    )(h, gain, target)


def _conv_pre(x, w_ref):
    lp = x.shape[0]
    acc = w_ref[2:3, :] * x
    for k in (0, 1, 3, 4):
        acc = acc + w_ref[k:k + 1, :] * pltpu.roll(x, (2 - k) % lp, 0)
    return acc


def conv_fwd(proj, off, w, l2, name):
    bsz, lp, _ = proj.shape
    d = w.shape[1]

    def body(x_ref, w_ref, o_ref):
        x = x_ref[0]
        s = jnp.where(_real_rows(0, lp), jax.nn.silu(_conv_pre(x, w_ref)), 0.0)
        if l2:
            s = s * lax.rsqrt(jnp.sum(s * s, axis=-1, keepdims=True) + EPS)
        o_ref[0] = s

    return pl.pallas_call(
        body, name=name, grid=(d // LANES, bsz),
        in_specs=[pl.BlockSpec((1, lp, LANES), lambda j, b: (b, 0, off + j)),
                  pl.BlockSpec((CONV_K, LANES), lambda j, b: (0, j))],
        out_specs=pl.BlockSpec((1, lp, LANES), lambda j, b: (b, 0, j)),
        out_shape=jax.ShapeDtypeStruct((bsz, lp, d), F32),
        compiler_params=_params(("parallel", "parallel")),
    )(proj, w)


def conv_bwd(proj, off, w, dy, l2, name):
    bsz, lp, _ = proj.shape
    d = w.shape[1]

    def body(x_ref, w_ref, dy_ref, dx_ref, dw_ref):
        b = pl.program_id(1)
        x, g = x_ref[0], dy_ref[0]
        real = _real_rows(0, lp)
        c = _conv_pre(x, w_ref)
        sg = jax.nn.sigmoid(c)
        s = jnp.where(real, c * sg, 0.0)
        if l2:
            r = lax.rsqrt(jnp.sum(s * s, axis=-1, keepdims=True) + EPS)
            g = r * g - s * (r * r * r) * jnp.sum(g * s, axis=-1, keepdims=True)
        dc = jnp.where(real, g * (sg * (1.0 + c * (1.0 - sg))), 0.0)
        dx = w_ref[2:3, :] * dc
        for k in (0, 1, 3, 4):
            dx = dx + w_ref[k:k + 1, :] * pltpu.roll(dc, (k - 2) % lp, 0)
        dx_ref[0] = jnp.where(real, dx, 0.0).astype(dx_ref.dtype)
        tap_row = lax.broadcasted_iota(jnp.int32, (CONV_K, LANES), 0)
        dw = jnp.zeros((CONV_K, LANES), F32)
        for k in range(CONV_K):
            xs = x if k == 2 else pltpu.roll(x, (2 - k) % lp, 0)
            dw = dw + jnp.where(tap_row == k, jnp.sum(dc * xs, axis=0, keepdims=True), 0.0)

        @pl.when(b == 0)
        def _():
            dw_ref[...] = dw

        @pl.when(b > 0)
        def _():
            dw_ref[...] += dw

    blk = pl.BlockSpec((1, lp, LANES), lambda j, b: (b, 0, j))
    return pl.pallas_call(
        body, name=name, grid=(d // LANES, bsz),
        in_specs=[pl.BlockSpec((1, lp, LANES), lambda j, b: (b, 0, off + j)),
                  pl.BlockSpec((CONV_K, LANES), lambda j, b: (0, j)), blk],
        out_specs=[blk, pl.BlockSpec((CONV_K, LANES), lambda j, b: (0, j))],
        out_shape=[jax.ShapeDtypeStruct((bsz, lp, d), BF16), jax.ShapeDtypeStruct((CONV_K, d), F32)],
        compiler_params=_params(("parallel", "arbitrary")),
    )(proj, w, dy)


def _tri_masks(rev):
    ii = lax.broadcasted_iota(jnp.int32, (CHUNK, CHUNK), 0)
    jj = lax.broadcasted_iota(jnp.int32, (CHUNK, CHUNK), 1)
    return ((ii <= jj), (ii < jj)) if rev else ((ii >= jj), (ii > jj))


def _lane_pick(block, lane):
    sel = lax.broadcasted_iota(jnp.int32, block.shape, 1) == lane
    return jnp.sum(jnp.where(sel, block, 0.0), axis=1, keepdims=True)


def _cumsum_impl(x, rev):
    n = x.shape[0]
    row = lax.broadcasted_iota(jnp.int32, x.shape, 0)
    step = 1
    while step < n:
        if rev:
            x = x + jnp.where(row < n - step, pltpu.roll(x, n - step, 0), 0.0)
        else:
            x = x + jnp.where(row >= step, pltpu.roll(x, step, 0), 0.0)
        step *= 2
    return x


@functools.partial(jax.custom_vjp, nondiff_argnums=(1,))
def _cumsum_rows(x, rev):
    return _cumsum_impl(x, rev)


_cumsum_rows.defvjp(lambda x, rev: (_cumsum_impl(x, rev), None),
                    lambda rev, _, g: (_cumsum_impl(g, not rev),))


def _unit_inv_impl(m):
    n = m.shape[0]
    eye = (lax.broadcasted_iota(jnp.int32, (n, n), 0) == lax.broadcasted_iota(jnp.int32, (n, n), 1)).astype(F32)
    p = -m
    inv = eye + p
    step = 2
    while step < n:
        p = _raw_dot(p, p, 1, 0, True)
        inv = inv + _raw_dot(inv, p, 1, 0, True)
        step *= 2
    return inv


@jax.custom_vjp
def _unit_inv(m):
    return _unit_inv_impl(m)


def _unit_inv_fwd(m):
    inv = _unit_inv_impl(m)
    return inv, inv


def _unit_inv_bwd(inv, g):
    return (-_raw_dot(_raw_dot(inv, g, 0, 0, True), inv, 1, 1, True),)


_unit_inv.defvjp(_unit_inv_fwd, _unit_inv_bwd)


GROUP = 2


def _unit_inv_all_impl(ms):
    n = ms[0].shape[0]
    width = GROUP * n
    dims = (((1,), (0,)), ((), ()))
    lane = lax.broadcasted_iota(jnp.int32, (n, width), 1)
    row = lax.broadcasted_iota(jnp.int32, (n, width), 0)
    blocks = [jnp.logical_and(lane >= t * n, lane < (t + 1) * n) for t in range(GROUP)]

    def halves(x):
        hi = x.astype(BF16)
        return hi, (x - hi.astype(F32)).astype(BF16)

    def xdot(a, b):
        diag = jnp.concatenate([jnp.where(blk, b, 0.0) for blk in blocks], axis=0)
        a_hi, a_lo = halves(a)
        d_hi, d_lo = halves(diag)
        out = lax.dot_general(a_hi, d_hi, dims, preferred_element_type=F32)
        out = out + lax.dot_general(a_hi, d_lo, dims, preferred_element_type=F32)
        return out + lax.dot_general(a_lo, d_hi, dims, preferred_element_type=F32)

    eye = (row == lane % n).astype(F32)
    groups = range(len(ms) // GROUP)
    p = [-jnp.concatenate(ms[t * GROUP:(t + 1) * GROUP], axis=1) for t in groups]
    inv = [eye + p[t] for t in groups]
    step = 2
    while step < n:
        p = [xdot(p[t], p[t]) for t in groups]
        inv = [inv[t] + xdot(inv[t], p[t]) for t in groups]
        step *= 2
    return [inv[t][:, u * n:(u + 1) * n] for t in groups for u in range(GROUP)]


@jax.custom_vjp
def _unit_inv_all(ms):
    return _unit_inv_all_impl(ms)


def _unit_inv_all_fwd(ms):
    out = _unit_inv_all_impl(ms)
    return out, out


_unit_inv_all.defvjp(_unit_inv_all_fwd, lambda invs, g: ([_unit_inv_bwd(i, gi)[0] for i, gi in zip(invs, g)],))


@jax.custom_vjp
def _unit_inv_known(m, inv):
    return inv


_unit_inv_known.defvjp(lambda m, inv: (inv, inv),
                       lambda inv, g: (_unit_inv_bwd(inv, g)[0], jnp.zeros_like(inv)))


def _gdn_chunk(qs, ks, vs, gb, ss, g_lanes, b_lanes, rev, invs=None, want_inv=False):
    nh = len(qs)
    dk = qs[0].shape[1]
    incl, strict = _tri_masks(rev)
    hs = range(nh)
    g = [_lane_pick(gb, l) for l in g_lanes]
    beta = [_lane_pick(gb, l) for l in b_lanes]
    qs = [q * (dk ** -0.5) for q in qs]
    gc_sq = [_cumsum_rows(jnp.broadcast_to(g[h], (CHUNK, CHUNK)), rev) for h in hs]
    gc = [_cumsum_rows(jnp.broadcast_to(g[h], (CHUNK, dk)), rev) for h in hs]
    g_last = [jnp.sum(g[h], axis=0, keepdims=True) for h in hs]
    decay = [jnp.where(incl, jnp.exp(jnp.where(incl, gc_sq[h] - gc_sq[h].T, 0.0)), 0.0) for h in hs]
    kb = [ks[h] * beta[h] for h in hs]
    kk = [_split(_bdot(jnp.concatenate([kb[h], qs[h]], axis=0), ks[h], 1, 1), CHUNK, 0) for h in hs]
    m = [jnp.where(strict, kk[h][0] * decay[h], 0.0) for h in hs]
    qk = [kk[h][1] * decay[h] for h in hs]
    if invs is not None:
        inv = [_unit_inv_known(m[h], invs[h]) for h in hs]
    elif nh % GROUP == 0:
        inv = _unit_inv_all(m)
    else:
        inv = [_unit_inv(m[h]) for h in hs]
    e_gc = [jnp.exp(gc[h]) for h in hs]
    uw = [_split(_bdot(inv[h], jnp.concatenate([vs[h] * beta[h], kb[h] * e_gc[h]], axis=1)), vs[h].shape[1], 1)
          for h in hs]
    u = [uw[h][0] for h in hs]
    w = [uw[h][1] for h in hs]
    q_dec = [qs[h] * e_gc[h] for h in hs]
    k_dec = [ks[h] * jnp.exp(g_last[h] - gc[h]) for h in hs]
    ws = [_split(_bdot(jnp.concatenate([w[h], q_dec[h]], axis=0), ss[h]), CHUNK, 0) for h in hs]
    v_new = [u[h] - ws[h][0] for h in hs]
    o = [ws[h][1] + _bdot(qk[h], v_new[h]) for h in hs]
    s_new = [ss[h] * jnp.exp(g_last[h]) + _bdot(k_dec[h], v_new[h], 0, 0) for h in hs]
    return (o, s_new, inv) if want_inv else (o, s_new)


def gdn_fwd(q, k, v, gb, heads, direction, o_prev, name):
    bsz, lp, d = q.shape
    nc = lp // CHUNK
    rev = direction == 1
    cm = (lambda n: nc - 1 - n) if rev else (lambda n: n)
    has_prev = o_prev is not None

    def body(*refs):
        q_ref, k_ref, v_ref, gb_ref = refs[:4]
        prev_ref = refs[4] if has_prev else None
        o_ref, st_ref, inv_ref, s_ref = refs[-4:]
        n = pl.program_id(1)

        @pl.when(n == 0)
        def _():
            s_ref[...] = jnp.zeros_like(s_ref)

        sls = [slice(h * LANES, (h + 1) * LANES) for h in range(heads)]
        ss = [s_ref[h] for h in range(heads)]
        for h in range(heads):
            st_ref[0, 0, h] = ss[h]
        os_, s_new, inv = _gdn_chunk([q_ref[0, :, sl] for sl in sls], [k_ref[0, :, sl] for sl in sls],
                                     [v_ref[0, :, sl] for sl in sls], gb_ref[0], ss,
                                     [direction * heads + h for h in range(heads)],
                                     [2 * heads + direction * heads + h for h in range(heads)], rev, want_inv=True)
        for h, sl in enumerate(sls):
            s_ref[h] = s_new[h]
            inv_ref[0, 0, h] = inv[h]
            o_ref[0, :, sl] = os_[h] + prev_ref[0, :, sl] if has_prev else os_[h]

    blk = pl.BlockSpec((1, CHUNK, d), lambda b, n: (b, cm(n), 0))
    gblk = pl.BlockSpec((1, CHUNK, LANES), lambda b, n: (b, cm(n), 0))
    st_blk = pl.BlockSpec((1, 1, heads, LANES, LANES), lambda b, n: (b, cm(n), 0, 0, 0))
    inv_blk = pl.BlockSpec((1, 1, heads, CHUNK, CHUNK), lambda b, n: (b, cm(n), 0, 0, 0))
    return pl.pallas_call(
        body, name=name, grid=(bsz, nc),
        in_specs=[blk, blk, blk, gblk] + ([blk] if has_prev else []),
        out_specs=[blk, st_blk, inv_blk],
        out_shape=[jax.ShapeDtypeStruct((bsz, lp, d), F32),
                   jax.ShapeDtypeStruct((bsz, nc, heads, LANES, LANES), F32),
                   jax.ShapeDtypeStruct((bsz, nc, heads, CHUNK, CHUNK), F32)],
        scratch_shapes=[pltpu.VMEM((heads, LANES, LANES), F32)],
        compiler_params=_params(("parallel", "arbitrary")),
    )(q, k, v, gb, *([o_prev] if has_prev else []))


def gdn_bwd(q, k, v, gb, states, invs, do, heads, direction, prev, name):
    bsz, lp, d = q.shape
    nc = lp // CHUNK
    rev = direction == 1
    cm = (lambda n: n) if rev else (lambda n: nc - 1 - n)
    has_prev = prev is not None

    def body(*refs):
        q_ref, k_ref, v_ref, gb_ref, st_ref, inv_ref, do_ref = refs[:7]
        prev_refs = refs[7:11] if has_prev else None
        dq_ref, dk_ref, dv_ref, dgb_ref, ds_ref = refs[-5:]
        n = pl.program_id(1)

        @pl.when(n == 0)
        def _():
            ds_ref[...] = jnp.zeros_like(ds_ref)

        sls = [slice(h * LANES, (h + 1) * LANES) for h in range(heads)]
        f = functools.partial(_gdn_chunk, g_lanes=[direction * heads + h for h in range(heads)],
                              b_lanes=[2 * heads + direction * heads + h for h in range(heads)], rev=rev,
                              invs=[inv_ref[0, 0, h] for h in range(heads)])
        _, pull = jax.vjp(f, [q_ref[0, :, sl] for sl in sls], [k_ref[0, :, sl] for sl in sls],
                          [v_ref[0, :, sl] for sl in sls], gb_ref[0], [st_ref[0, 0, h] for h in range(heads)])
        dq, dk, dv, dgb, ds = pull(([do_ref[0, :, sl] for sl in sls], [ds_ref[h] for h in range(heads)]))
        for h, sl in enumerate(sls):
            ds_ref[h] = ds[h]
            if has_prev:
                dq[h], dk[h], dv[h] = (dq[h] + prev_refs[0][0, :, sl], dk[h] + prev_refs[1][0, :, sl],
                                       dv[h] + prev_refs[2][0, :, sl])
            dq_ref[0, :, sl] = dq[h]
            dk_ref[0, :, sl] = dk[h]
            dv_ref[0, :, sl] = dv[h]
        dgb_ref[0] = dgb + prev_refs[3][0] if has_prev else dgb

    blk = pl.BlockSpec((1, CHUNK, d), lambda b, n: (b, cm(n), 0))
    gblk = pl.BlockSpec((1, CHUNK, LANES), lambda b, n: (b, cm(n), 0))
    st_blk = pl.BlockSpec((1, 1, heads, LANES, LANES), lambda b, n: (b, cm(n), 0, 0, 0))
    inv_blk = pl.BlockSpec((1, 1, heads, CHUNK, CHUNK), lambda b, n: (b, cm(n), 0, 0, 0))
    big = jax.ShapeDtypeStruct((bsz, lp, d), F32)
    return pl.pallas_call(
        body, name=name, grid=(bsz, nc),
        in_specs=[blk, blk, blk, gblk, st_blk, inv_blk, blk] + ([blk, blk, blk, gblk] if has_prev else []),
        out_specs=[blk, blk, blk, gblk],
        out_shape=[big, big, big, jax.ShapeDtypeStruct((bsz, lp, LANES), F32)],
        scratch_shapes=[pltpu.VMEM((heads, LANES, LANES), F32)],
        compiler_params=_params(("parallel", "arbitrary")),
    )(q, k, v, gb, states, invs, do, *(list(prev) if has_prev else []))


def _ret_chunk(q, k, v, r, logit, lane, rev):
    dk = q.shape[1]
    lg = jax.nn.log_sigmoid(_lane_pick(logit, lane))
    k = k * (dk ** -0.5)
    ii = lax.broadcasted_iota(jnp.int32, (CHUNK, CHUNK), 0)
    jj = lax.broadcasted_iota(jnp.int32, (CHUNK, CHUNK), 1)
    pos = lax.broadcasted_iota(jnp.int32, (CHUNK, 1), 0)
    if rev:
        incl, rel = ii <= jj, (jj - ii)
        seen = (CHUNK - 1 - pos)
    else:
        incl, rel = ii >= jj, (ii - jj)
        seen = pos
    relf = jnp.where(incl, rel, 0).astype(F32)
    seenf = seen.astype(F32)
    intra = jnp.where(incl, jnp.exp(relf * lg), 0.0)
    qk = _bdot(q, k, 1, 1) * intra
    q_dec = q * jnp.exp(lg * (seenf + 1.0))
    k_dec = k * jnp.exp(lg * (CHUNK - 1.0 - seenf))
    o = _bdot(q_dec, r) + _bdot(qk, v)
    r_new = r * jnp.exp(lg * CHUNK) + _bdot(k_dec, v, 0, 0)
    return o, r_new


def ret_fwd(qk, v_arr, v_off, logit, heads, direction, o_prev, name):
    bsz, lp, d2 = qk.shape
    d = d2 // 2
    dkh, dvh = d // heads, 2 * d // heads
    nc = lp // CHUNK
    rev = direction == 1
    cm = (lambda n: nc - 1 - n) if rev else (lambda n: n)
    has_prev = o_prev is not None
    v_cb = v_off * LANES // (2 * d)
    assert v_cb * 2 * d == v_off * LANES

    def body(*refs):
        q_ref, k_ref, v_ref, lg_ref = refs[:4]
        prev_ref = refs[4] if has_prev else None
        o_ref, st_ref, r_ref = refs[-3], refs[-2], refs[-1]
        n = pl.program_id(1)

        @pl.when(n == 0)
        def _():
            r_ref[...] = jnp.zeros_like(r_ref)

        lgv = lg_ref[...]
        for h in range(heads):
            ks, vs = slice(h * dkh, (h + 1) * dkh), slice(h * dvh, (h + 1) * dvh)
            r = r_ref[h]
            st_ref[0, 0, h] = r.astype(st_ref.dtype)
            o, r_new = _ret_chunk(q_ref[0, :, ks], k_ref[0, :, ks], v_ref[0, :, vs], r, lgv, h, rev)
            r_ref[h] = r_new
            o_ref[0, :, vs] = o + prev_ref[0, :, vs] if has_prev else o

    qblk = pl.BlockSpec((1, CHUNK, d), lambda b, n: (b, cm(n), 0))
    kblk = pl.BlockSpec((1, CHUNK, d), lambda b, n: (b, cm(n), 1))
    vblk = pl.BlockSpec((1, CHUNK, 2 * d), lambda b, n: (b, cm(n), v_cb))
    oblk = pl.BlockSpec((1, CHUNK, 2 * d), lambda b, n: (b, cm(n), 0))
    st_blk = pl.BlockSpec((1, 1, heads, dkh, dvh), lambda b, n: (b, cm(n), 0, 0, 0))
    return pl.pallas_call(
        body, name=name, grid=(bsz, nc),
        in_specs=[qblk, kblk, vblk, pl.BlockSpec((1, LANES), lambda b, n: (0, 0))] + ([oblk] if has_prev else []),
        out_specs=[oblk, st_blk],
        out_shape=[jax.ShapeDtypeStruct((bsz, lp, 2 * d), F32),
                   jax.ShapeDtypeStruct((bsz, nc, heads, dkh, dvh), BF16)],
        scratch_shapes=[pltpu.VMEM((heads, dkh, dvh), F32)],
        compiler_params=_params(("parallel", "arbitrary")),
    )(qk, qk, v_arr, logit, *([o_prev] if has_prev else []))


def ret_bwd(qk, v_arr, v_off, logit, states, do, heads, direction, prev, name, dv_dtype=F32):
    bsz, lp, d2 = qk.shape
    d = d2 // 2
    dkh, dvh = d // heads, 2 * d // heads
    nc = lp // CHUNK
    rev = direction == 1
    cm = (lambda n: n) if rev else (lambda n: nc - 1 - n)
    has_prev = prev is not None
    v_cb = v_off * LANES // (2 * d)

    def body(*refs):
        q_ref, k_ref, v_ref, lg_ref, st_ref, do_ref = refs[:6]
        prev_refs = refs[6:8] if has_prev else None
        dqk_ref, dv_ref, dlg_ref, dr_ref = refs[-4:]
        b, n = pl.program_id(0), pl.program_id(1)

        @pl.when(n == 0)
        def _():
            dr_ref[...] = jnp.zeros_like(dr_ref)

        lgv = lg_ref[...]
        dlg = jnp.zeros((1, LANES), F32)
        for h in range(heads):
            ks, vs = slice(h * dkh, (h + 1) * dkh), slice(h * dvh, (h + 1) * dvh)
            f = functools.partial(_ret_chunk, lane=h, rev=rev)
            _, pull = jax.vjp(f, q_ref[0, :, ks], k_ref[0, :, ks], v_ref[0, :, vs], st_ref[0, 0, h].astype(F32), lgv)
            dq, dk, dv, dr, dlg_h = pull((do_ref[0, :, vs], dr_ref[h]))
            dr_ref[h] = dr
            dlg = dlg + dlg_h
            kks = slice(d + h * dkh, d + (h + 1) * dkh)
            if has_prev:
                dq, dk, dv = dq + prev_refs[0][0, :, ks], dk + prev_refs[0][0, :, kks], dv + prev_refs[1][0, :, vs]
            dqk_ref[0, :, ks] = dq
            dqk_ref[0, :, kks] = dk
            dv_ref[0, :, vs] = dv.astype(dv_ref.dtype)
        first = jnp.logical_and(b == 0, n == 0)

        @pl.when(first)
        def _():
            dlg_ref[...] = dlg

        @pl.when(jnp.logical_not(first))
        def _():
            dlg_ref[...] += dlg

    qblk = pl.BlockSpec((1, CHUNK, d), lambda b, n: (b, cm(n), 0))
    kblk = pl.BlockSpec((1, CHUNK, d), lambda b, n: (b, cm(n), 1))
    vblk = pl.BlockSpec((1, CHUNK, 2 * d), lambda b, n: (b, cm(n), v_cb))
    oblk = pl.BlockSpec((1, CHUNK, 2 * d), lambda b, n: (b, cm(n), 0))
    lblk = pl.BlockSpec((1, LANES), lambda b, n: (0, 0))
    st_blk = pl.BlockSpec((1, 1, heads, dkh, dvh), lambda b, n: (b, cm(n), 0, 0, 0))
    return pl.pallas_call(
        body, name=name, grid=(bsz, nc),
        in_specs=[qblk, kblk, vblk, lblk, st_blk, oblk] + ([oblk, oblk] if has_prev else []),
        out_specs=[oblk, oblk, lblk],
        out_shape=[jax.ShapeDtypeStruct((bsz, lp, 2 * d), F32), jax.ShapeDtypeStruct((bsz, lp, 2 * d), dv_dtype),
                   jax.ShapeDtypeStruct((1, LANES), F32)],
        scratch_shapes=[pltpu.VMEM((heads, dkh, dvh), F32)],
        compiler_params=_params(("arbitrary", "arbitrary")),
    )(qk, qk, v_arr, logit, states, do, *(list(prev) if has_prev else []))


def _flip(v, bit):
    return 1 - v if bit else v


def _peer(x, y, c, off):
    return (_flip(x, off & 4), _flip(y, off & 2), _flip(c, off & 1))


def _slot(ref, axis, idx):
    return ref.at[(slice(None),) * axis + (idx,)]


def _slotted_shape(shape, axis):
    return tuple(shape[:axis]) + (N_DEV,) + tuple(shape[axis:])


def all_gather_hbm(blocks, axes, name):
    n = len(blocks)

    def body(*refs):
        x_refs, out_refs = refs[:n], refs[n:2 * n]
        send_sems, recv_sems, local_sems = refs[2 * n:]
        x, y, c = lax.axis_index("x"), lax.axis_index("y"), lax.axis_index("c")
        me, sibling = (x, y, c), (x, y, 1 - c)
        chips = [(1 - x, y), (x, 1 - y), (1 - x, 1 - y)]

        def slot(k, px, py, pc):
            return _slot(out_refs[k], axes[k], 4 * px + 2 * py + pc)

        def copy(k, j, block_of, to, src=None):
            return pltpu.make_async_remote_copy(
                src_ref=slot(k, *block_of) if src is None else src, dst_ref=slot(k, *block_of),
                send_sem=send_sems.at[7 * k + j], recv_sem=recv_sems.at[7 * k + j], device_id=to,
                device_id_type=pl.DeviceIdType.MESH)

        ks = range(n)
        mine = [pltpu.make_async_copy(x_refs[k], slot(k, *me), local_sems.at[k]) for k in ks]
        for cp in mine:
            cp.start()
        first = [copy(k, 0, me, sibling, src=x_refs[k]) for k in ks]
        first += [copy(k, 1 + j, me, (*chip, c), src=x_refs[k]) for k in ks for j, chip in enumerate(chips)]
        for cp in first:
            cp.start()
        passed = []
        for j, chip in enumerate(chips):
            for k in ks:
                copy(k, 1 + j, (*chip, c), me).wait_recv()
                passed.append(copy(k, 4 + j, (*chip, c), sibling))
                passed[-1].start()
        for k in ks:
            copy(k, 0, sibling, me).wait_recv()
        for j, chip in enumerate(chips):
            for k in ks:
                copy(k, 4 + j, (*chip, 1 - c), me).wait_recv()
        for cp in first + passed:
            cp.wait_send()
        for cp in mine:
            cp.wait()

    hbm = pl.BlockSpec(memory_space=pl.ANY)
    return pl.pallas_call(
        body, name=name,
        out_shape=[jax.ShapeDtypeStruct(_slotted_shape(b.shape, ax), b.dtype) for b, ax in zip(blocks, axes)],
        in_specs=[hbm] * n, out_specs=[hbm] * n,
        scratch_shapes=[pltpu.SemaphoreType.DMA((7 * n,)), pltpu.SemaphoreType.DMA((7 * n,)),
                        pltpu.SemaphoreType.DMA((n,))],
    )(*blocks)


def all_gather_small(block, name):
    r, lanes = block.shape

    def body(x_ref, out_ref, send_sems, recv_sems):
        x, y, c = lax.axis_index("x"), lax.axis_index("y"), lax.axis_index("c")
        me = 4 * x + 2 * y + c
        out_ref[me] = x_ref[...]
        copies = []
        for off in range(1, N_DEV):
            copies.append(pltpu.make_async_remote_copy(
                src_ref=x_ref, dst_ref=out_ref.at[me], send_sem=send_sems.at[off - 1], recv_sem=recv_sems.at[off - 1],
                device_id=_peer(x, y, c, off), device_id_type=pl.DeviceIdType.MESH))
        for cp in copies:
            cp.start()
        for cp in copies:
            cp.wait()

    return pl.pallas_call(
        body, name=name, out_shape=jax.ShapeDtypeStruct((N_DEV, r, lanes), block.dtype),
        in_specs=[pl.BlockSpec(memory_space=pltpu.VMEM)], out_specs=pl.BlockSpec(memory_space=pltpu.VMEM),
        scratch_shapes=[pltpu.SemaphoreType.DMA((7,)), pltpu.SemaphoreType.DMA((7,))],
    )(block)


def all_to_all_hbm(pieces, axes, name):
    n = len(pieces)

    def body(*refs):
        x_refs, out_refs = refs[:n], refs[n:2 * n]
        send_sems, recv_sems, local_sems = refs[2 * n:]
        x, y, c = lax.axis_index("x"), lax.axis_index("y"), lax.axis_index("c")
        me = 4 * x + 2 * y + c
        mine = [pltpu.make_async_copy(_slot(x_refs[k], axes[k], me), _slot(out_refs[k], axes[k], me), local_sems.at[k])
                for k in range(n)]
        for cp in mine:
            cp.start()
        copies = []
        for off in range(1, N_DEV):
            px, py, pc = _peer(x, y, c, off)
            for k in range(n):
                copies.append(pltpu.make_async_remote_copy(
                    src_ref=_slot(x_refs[k], axes[k], 4 * px + 2 * py + pc), dst_ref=_slot(out_refs[k], axes[k], me),
                    send_sem=send_sems.at[7 * k + off - 1], recv_sem=recv_sems.at[7 * k + off - 1],
                    device_id=(px, py, pc), device_id_type=pl.DeviceIdType.MESH))
        for cp in copies:
            cp.start()
        for cp in copies:
            cp.wait()
        for cp in mine:
            cp.wait()

    hbm = pl.BlockSpec(memory_space=pl.ANY)
    return pl.pallas_call(
        body, name=name, out_shape=[jax.ShapeDtypeStruct(p.shape, p.dtype) for p in pieces],
        in_specs=[hbm] * n, out_specs=[hbm] * n,
        scratch_shapes=[pltpu.SemaphoreType.DMA((7 * n,)), pltpu.SemaphoreType.DMA((7 * n,)),
                        pltpu.SemaphoreType.DMA((n,))],
    )(*pieces)


N_CHIP = 4


def _reslotted(arr, axis, slots):
    return jax.ShapeDtypeStruct(tuple(arr.shape[:axis]) + (slots,) + tuple(arr.shape[axis + 1:]), arr.dtype)


def sibling_exchange(pieces, axes, name):
    n = len(pieces)

    def body(*refs):
        x_refs, mine_refs, sib_refs = refs[:n], refs[n:2 * n], refs[2 * n:3 * n]
        send_sems, recv_sems, local_sems = refs[3 * n:]
        x, y, c = lax.axis_index("x"), lax.axis_index("y"), lax.axis_index("c")
        local, remote = [], []
        for k in range(n):
            for j in range(N_CHIP):
                chip = 4 * (j // 2) + 2 * (j % 2)
                local.append(pltpu.make_async_copy(_slot(x_refs[k], axes[k], chip + c), _slot(mine_refs[k], axes[k], j),
                                                   local_sems.at[N_CHIP * k + j]))
                remote.append(pltpu.make_async_remote_copy(
                    src_ref=_slot(x_refs[k], axes[k], chip + 1 - c), dst_ref=_slot(sib_refs[k], axes[k], j),
                    send_sem=send_sems.at[N_CHIP * k + j], recv_sem=recv_sems.at[N_CHIP * k + j],
                    device_id=(x, y, 1 - c), device_id_type=pl.DeviceIdType.MESH))
        for cp in local + remote:
            cp.start()
        for cp in remote:
            cp.wait()
        for cp in local:
            cp.wait()

    hbm = pl.BlockSpec(memory_space=pl.ANY)
    shapes = [_reslotted(p, ax, N_CHIP) for p, ax in zip(pieces, axes)]
    res = pl.pallas_call(
        body, name=name, out_shape=shapes + shapes, in_specs=[hbm] * n, out_specs=[hbm] * (2 * n),
        scratch_shapes=[pltpu.SemaphoreType.DMA((N_CHIP * n,)), pltpu.SemaphoreType.DMA((N_CHIP * n,)),
                        pltpu.SemaphoreType.DMA((N_CHIP * n,))],
    )(*pieces)
    return res[:n], res[n:]


def chip_exchange(partials, axes, name):
    n = len(partials)

    def body(*refs):
        p_refs, out_refs = refs[:n], refs[n:2 * n]
        send_sems, recv_sems, local_sems = refs[2 * n:]
        x, y, c = lax.axis_index("x"), lax.axis_index("y"), lax.axis_index("c")
        my_chip = 2 * x + y
        local, remote = [], []
        for k in range(n):
            local.append(pltpu.make_async_copy(_slot(p_refs[k], axes[k], my_chip), _slot(out_refs[k], axes[k], my_chip),
                                               local_sems.at[k]))
            for off in range(1, N_CHIP):
                tx, ty = _flip(x, off & 2), _flip(y, off & 1)
                remote.append(pltpu.make_async_remote_copy(
                    src_ref=_slot(p_refs[k], axes[k], 2 * tx + ty), dst_ref=_slot(out_refs[k], axes[k], my_chip),
                    send_sem=send_sems.at[3 * k + off - 1], recv_sem=recv_sems.at[3 * k + off - 1],
                    device_id=(tx, ty, c), device_id_type=pl.DeviceIdType.MESH))
        for cp in local + remote:
            cp.start()
        for cp in remote:
            cp.wait()
        for cp in local:
            cp.wait()

    hbm = pl.BlockSpec(memory_space=pl.ANY)
    return pl.pallas_call(
        body, name=name, out_shape=[jax.ShapeDtypeStruct(p.shape, p.dtype) for p in partials],
        in_specs=[hbm] * n, out_specs=[hbm] * n,
        scratch_shapes=[pltpu.SemaphoreType.DMA((3 * n,)), pltpu.SemaphoreType.DMA((3 * n,)),
                        pltpu.SemaphoreType.DMA((n,))],
    )(*partials)


def _adamw_update(w, m, v, pieces):
    g = pieces[0].astype(F32)
    for piece in pieces[1:]:
        g = g + piece.astype(F32)
    mn = ADAM_B1 * m + (1.0 - ADAM_B1) * g
    vn = ADAM_B2 * v + (1.0 - ADAM_B2) * (g * g)
    m_hat = mn / (1.0 - ADAM_B1 ** ADAM_STEP)
    v_hat = vn / (1.0 - ADAM_B2 ** ADAM_STEP)
    return g, -ADAM_LR * (m_hat / (jnp.sqrt(v_hat) + ADAM_EPS) + ADAM_WD * w), mn, vn


def adamw_blocks(w, m, v, parts, slot_axis, name):
    nl, r, c = w.shape
    ns = parts.shape[slot_axis]
    tr = _tile(r, 128, 16)

    def body(w_ref, m_ref, v_ref, g_ref, g_out, d_out, m_out, v_out):
        pieces = [g_ref[s, 0] if slot_axis == 0 else g_ref[0, s] for s in range(ns)]
        for ref, val in zip((g_out, d_out, m_out, v_out), _adamw_update(w_ref[0], m_ref[0], v_ref[0], pieces)):
            ref[0] = val

    blk = pl.BlockSpec((1, tr, c), lambda l, i: (l, i, 0))
    gblk = (pl.BlockSpec((ns, 1, tr, c), lambda l, i: (0, l, i, 0)) if slot_axis == 0
            else pl.BlockSpec((1, ns, tr, c), lambda l, i: (l, 0, i, 0)))
    shp = jax.ShapeDtypeStruct(w.shape, F32)
    return pl.pallas_call(
        body, name=name, grid=(nl, r // tr), in_specs=[blk, blk, blk, gblk],
        out_specs=[blk, blk, blk, blk], out_shape=[shp, shp, shp, shp],
        compiler_params=_params(("parallel", "parallel")),
    )(w, m, v, parts)


def partial_sums(a, b, slot_axis, name):
    s0, s1, r, c = a.shape
    tr = _tile(r, 128, 16)

    def body(a_ref, b_ref, o_ref):
        o_ref[...] = (a_ref[...].astype(F32) + b_ref[...].astype(F32)).astype(o_ref.dtype)

    blk = (pl.BlockSpec((s0, 1, tr, c), lambda l, i: (0, l, i, 0)) if slot_axis == 0
           else pl.BlockSpec((1, s1, tr, c), lambda l, i: (l, 0, i, 0)))
    return pl.pallas_call(
        body, name=name, grid=(s1 if slot_axis == 0 else s0, r // tr), in_specs=[blk, blk], out_specs=blk,
        out_shape=jax.ShapeDtypeStruct(a.shape, BF16), compiler_params=_params(("parallel", "parallel")),
    )(a, b)


def adamw(w, m, v, g8, name):
    r = w.shape[0]
    tm = _tile(r, 1024, 8)

    def body(w_ref, m_ref, v_ref, g_ref, g_out, d_out, m_out, v_out):
        res = _adamw_update(w_ref[...], m_ref[...], v_ref[...], [g_ref[s] for s in range(N_DEV)])
        for ref, val in zip((g_out, d_out, m_out, v_out), res):
            ref[...] = val

    blk = pl.BlockSpec((tm, LANES), lambda i: (i, 0))
    shp = jax.ShapeDtypeStruct((r, LANES), F32)
    return pl.pallas_call(
        body, name=name, grid=(r // tm,),
        in_specs=[blk, blk, blk, pl.BlockSpec((N_DEV, tm, LANES), lambda i: (0, i, 0))],
        out_specs=[blk, blk, blk, blk], out_shape=[shp, shp, shp, shp],
        compiler_params=_params(("parallel",)),
    )(w, m, v, g8)


def _pack(blocks, rows_mult):
    flat = jnp.concatenate([b.reshape(-1) for b in blocks])
    unit = rows_mult * LANES
    total = -(-flat.shape[0] // unit) * unit
    return jnp.pad(flat, (0, total - flat.shape[0])).reshape(-1, LANES)


def _unpack(packed, shapes):
    flat = packed.reshape(-1)
    out, pos = [], 0
    for s in shapes:
        n = math.prod(s)
        out.append(flat[pos:pos + n].reshape(s))
        pos += n
    return out


def _gathered_full(gathered, shapes, axes):
    per_dev = [_unpack(gathered[d], shapes) for d in range(N_DEV)]
    return [jnp.concatenate([per_dev[d][k] for d in range(N_DEV)], axis=axes[k]) for k in range(len(shapes))]


def _pieces_by_dest(fulls, axes, rows_mult):
    packs = []
    for d in range(N_DEV):
        blocks = []
        for f, ax in zip(fulls, axes):
            n = f.shape[ax] // N_DEV
            blocks.append(lax.slice_in_dim(f, d * n, (d + 1) * n, axis=ax))
        packs.append(_pack(blocks, rows_mult))
    return jnp.stack(packs)


class Layout:
    def __init__(self, d):
        self.d = d
        self.h = d // 128
        self.hr = d // 256
        self.z = 3 * d
        self.qb = 4 * d
        self.vb = 6 * d
        self.gb = 8 * d
        self.ga = 10 * d
        self.gbt = 11 * d
        self.ab = 12 * d
        self.used = 12 * d + LANES
        self.np = -(-self.used // 512) * 512

    def relayout_w_in(self, w):
        d, h4 = self.d, 4 * self.h
        return jnp.concatenate([w[:, :4 * d], w[:, 4 * d + h4:], w[:, 4 * d:4 * d + h4],
                                jnp.zeros((d, self.np - 12 * d - h4), w.dtype)], axis=1)

    def unlayout_w_in(self, w):
        d, h4 = self.d, 4 * self.h
        return jnp.concatenate([w[:, :4 * d], w[:, 12 * d:12 * d + h4], w[:, 4 * d:12 * d]], axis=1)


def _lane_row(vec):
    return jnp.pad(vec.reshape(-1), (0, LANES - vec.size)).reshape(1, LANES)


def _rope_tables(lp, half):
    inv = ROPE_BASE ** (-jnp.arange(half, dtype=F32) / half)
    pos = (jnp.arange(lp) - N_PAD).astype(F32)
    ang = pos[:, None] * inv[None, :]
    return jnp.cos(ang), jnp.sin(ang)


def local_step(x, target, meta, p):
    bsz, seq, d = x.shape
    lay = Layout(d)
    h_gdn, h_ret = lay.h, lay.hr
    lp = seq + CHUNK
    t_all = bsz * lp
    depth = p["w_up_a"].shape[0]
    ff = p["w_ffn_out"].shape[1]
    tm = _tile(lp, 512, 16)
    tmw = _tile(lp, 256, 16)
    tmn = _tile(lp, 1040, 16)
    tmr = _tile(lp, 832, 16)
    cb = lambda cols: cols // LANES
    flat = lambda a: a.reshape(t_all, a.shape[-1])
    unflat = lambda a: a.reshape(bsz, lp, a.shape[-1])
    cos, sin = _rope_tables(lp, LANES)
    f_gb = make_f_gb(h_gdn)
    rope_f, rope_b = make_f_rope(1.0), make_f_rope(-1.0)

    head = jnp.concatenate([jnp.zeros((N_PAD, d), F32), meta], axis=0)
    h = jnp.concatenate([jnp.broadcast_to(head[None], (bsz, CHUNK, d)), x], axis=1)
    tgt = jnp.pad(target, ((0, 0), (CHUNK, 0), (0, 0)))

    saved = []
    for l in range(depth):
        s = {"h_in": h}
        nm = lambda k: f"l{l}_{k}"
        g_mix, g_ffn = p["norm_mix"][l][None], p["norm_ffn"][l][None]
        alog, dtb = _lane_row(p["gdn_a_log"][l]), _lane_row(p["gdn_dt_bias"][l])
        gain_a = p["gdn_norm"][l][None]
        logits = [_lane_row(p["ret_decay_logit"][l][0]), _lane_row(p["ret_decay_logit"][l][1])]
        cw = p["conv_w"][l]
        (hn,) = rowwise(nm("rms_mix"), f_rms, [Row(h, d)], [], [g_mix], [(d, d)], 1, tm, BF16)
        proj = unflat(matmul(flat(hn), p["w_in"][l], name=nm("mm_in")))
        qa = conv_fwd(proj, 0, cw[:, :d], True, nm("conv_q"))
        ka = conv_fwd(proj, cb(d), cw[:, d:2 * d], True, nm("conv_k"))
        va = conv_fwd(proj, cb(2 * d), cw[:, 2 * d:], False, nm("conv_v"))
        (gb,) = rowwise(nm("gb"), f_gb, [Row(proj, LANES, cb(lay.ab))], [], [alog, dtb], [(LANES, LANES)], 1, tmn)
        o0, st_a0, iv_a0 = gdn_fwd(qa, ka, va, gb, h_gdn, 0, None, nm("gdn_f0"))
        oa, st_a1, iv_a1 = gdn_fwd(qa, ka, va, gb, h_gdn, 1, o0, nm("gdn_f1"))
        (oan,) = rowwise(nm("gdn_out"), f_gdn_out, [Row(oa, LANES), Row(proj, LANES, cb(lay.z))], [], [gain_a],
                         [(d, LANES)], h_gdn, tmn, BF16)
        ya = unflat(matmul(flat(oan), p["w_up_a"][l], name=nm("mm_up_a")))
        (qkr,) = rowwise(nm("rope"), rope_f, [Row(proj, 2 * LANES, cb(lay.qb) // 2)], [Tab(cos), Tab(sin)], [],
                         [(2 * d, 2 * LANES)], 2 * h_ret, tmn)
        r0, st_b0 = ret_fwd(qkr, proj, cb(lay.vb), logits[0], h_ret, 0, None, nm("ret_f0"))
        ob, st_b1 = ret_fwd(qkr, proj, cb(lay.vb), logits[1], h_ret, 1, r0, nm("ret_f1"))
        (obn,) = rowwise(nm("ret_out"), f_ret_out, [Row(ob, 4 * LANES), Row(proj, 4 * LANES, cb(lay.gb) // 4)], [], [],
                         [(2 * d, 4 * LANES)], h_ret, tmr, BF16)
        yb = unflat(matmul(flat(obn), p["w_up_b"][l], name=nm("mm_up_b")))
        (mg,) = rowwise(nm("merge"), f_merge,
                        [Row(proj, LANES, cb(lay.ga)), Row(proj, LANES, cb(lay.gbt)), Row(ya, LANES), Row(yb, LANES)],
                        [], [], [(d, LANES)], cb(d), tmn, BF16)
        h_mid = unflat(matmul(flat(mg), p["w_out"][l], add=flat(h), name=nm("mm_out")))
        (hn2,) = rowwise(nm("rms_ffn"), f_rms, [Row(h_mid, d)], [], [g_ffn], [(d, d)], 1, tm, BF16)
        ffp = unflat(matmul(flat(hn2), p["w_ffn_in"][l], name=nm("mm_ffn_in")))
        (act,) = rowwise(nm("swiglu"), f_swiglu, [Row(ffp, 2 * ff)], [], [], [(ff, ff)], 1, tmw, BF16)
        h = unflat(matmul(flat(act), p["w_ffn_out"][l], add=flat(h_mid), name=nm("mm_ffn_out")))
        s.update(hn=hn, proj=proj, qa=qa, ka=ka, va=va, gb=gb, st_a=(st_a0, st_a1), iv_a=(iv_a0, iv_a1), oa=oa,
                 oan=oan, ya=ya, qkr=qkr, st_b=(st_b0, st_b1), ob=ob, obn=obn, yb=yb, mg=mg, h_mid=h_mid, hn2=hn2,
                 ffp=ffp, act=act, logits=logits, alog=alog, dtb=dtb, gain_a=gain_a, cw=cw, g_mix=g_mix, g_ffn=g_ffn)
        saved.append(s)

    dh, d_final, loss_row = loss_head(h, p["norm_final"][None], tgt, tm)

    grads = {k: [None] * depth for k in ("norm_mix", "w_in", "conv_w", "gdn_a_log", "gdn_dt_bias", "gdn_norm",
                                          "ret_decay_logit", "w_up_a", "w_up_b", "w_out", "norm_ffn", "w_ffn_in",
                                          "w_ffn_out")}
    for l in reversed(range(depth)):
        s = saved[l]
        nm = lambda k: f"l{l}_{k}"
        proj = s["proj"]
        dhf = flat(dh)
        grads["w_ffn_out"][l] = matmul(flat(s["act"]), dhf, ta=True, name=nm("mmg_ffn_out"))
        dact = unflat(matmul(dhf, p["w_ffn_out"][l], tb=True, name=nm("mmb_ffn_out")))
        (dffp,), _ = rowwise_vjp(nm("swiglu_b"), f_swiglu, [Row(s["ffp"], 2 * ff)], [], [], [dact], 1, tmw, narrow=(0,))
        grads["w_ffn_in"][l] = matmul(flat(s["hn2"]), flat(dffp), ta=True, name=nm("mmg_ffn_in"))
        dhn2 = unflat(matmul(flat(dffp), p["w_ffn_in"][l], tb=True, name=nm("mmb_ffn_in")))
        (dh_mid,), (dg_ffn,) = rowwise_vjp(nm("rms_ffn_b"), f_rms, [Row(s["h_mid"], d)], [], [s["g_ffn"]], [dhn2], 1, tmw,
                                           adds={0: dh})
        grads["norm_ffn"][l] = dg_ffn[0]
        dmf = flat(dh_mid)
        grads["w_out"][l] = matmul(flat(s["mg"]), dmf, ta=True, name=nm("mmg_out"))
        dmg = unflat(matmul(dmf, p["w_out"][l], tb=True, name=nm("mmb_out")))
        (dga, dgbt, dya, dyb), _ = rowwise_vjp(
            nm("merge_b"), f_merge,
            [Row(proj, LANES, cb(lay.ga)), Row(proj, LANES, cb(lay.gbt)), Row(s["ya"], LANES), Row(s["yb"], LANES)],
            [], [], [dmg], cb(d), tmn, narrow=(0, 1, 2, 3))
        grads["w_up_b"][l] = matmul(flat(s["obn"]), flat(dyb), ta=True, name=nm("mmg_up_b"))
        dobn = unflat(matmul(flat(dyb), p["w_up_b"][l], tb=True, name=nm("mmb_up_b")))
        (dob, dg_b), _ = rowwise_vjp(nm("ret_out_b"), f_ret_out,
                                     [Row(s["ob"], 4 * LANES), Row(proj, 4 * LANES, cb(lay.gb) // 4)], [], [], [dobn],
                                     h_ret, tmr, narrow=(1,))
        r1 = ret_bwd(s["qkr"], proj, cb(lay.vb), s["logits"][1], s["st_b"][1], dob, h_ret, 1, None, nm("ret_b1"))
        r0 = ret_bwd(s["qkr"], proj, cb(lay.vb), s["logits"][0], s["st_b"][0], dob, h_ret, 0, r1[:2], nm("ret_b0"),
                     dv_dtype=BF16)
        (dqk,) = rowwise(nm("rope_b"), rope_b, [Row(r0[0], 2 * LANES)], [Tab(cos), Tab(sin)], [],
                         [(2 * d, 2 * LANES)], 2 * h_ret, tmn, BF16)
        dv_b = r0[1]
        grads["ret_decay_logit"][l] = jnp.stack([r0[2][0, :h_ret], r1[2][0, :h_ret]])
        grads["w_up_a"][l] = matmul(flat(s["oan"]), flat(dya), ta=True, name=nm("mmg_up_a"))
        doan = unflat(matmul(flat(dya), p["w_up_a"][l], tb=True, name=nm("mmb_up_a")))
        (doa, dz), (dgain_a,) = rowwise_vjp(nm("gdn_out_b"), f_gdn_out,
                                            [Row(s["oa"], LANES), Row(proj, LANES, cb(lay.z))], [], [s["gain_a"]],
                                            [doan], h_gdn, tmn, narrow=(1,))
        grads["gdn_norm"][l] = dgain_a[0]
        a1 = gdn_bwd(s["qa"], s["ka"], s["va"], s["gb"], s["st_a"][1], s["iv_a"][1], doa, h_gdn, 1, None, nm("gdn_b1"))
        a0 = gdn_bwd(s["qa"], s["ka"], s["va"], s["gb"], s["st_a"][0], s["iv_a"][0], doa, h_gdn, 0, a1, nm("gdn_b0"))
        (dab,), (dalog, ddtb) = rowwise_vjp(nm("gb_b"), f_gb, [Row(proj, LANES, cb(lay.ab))], [], [s["alog"], s["dtb"]],
                                            [a0[3]], 1, tmn, narrow=(0,))
        grads["gdn_a_log"][l] = dalog[0, :2 * h_gdn].reshape(2, h_gdn)
        grads["gdn_dt_bias"][l] = ddtb[0, :2 * h_gdn].reshape(2, h_gdn)
        cw = s["cw"]
        dxq, dwq = conv_bwd(proj, 0, cw[:, :d], a0[0], True, nm("conv_q_b"))
        dxk, dwk = conv_bwd(proj, cb(d), cw[:, d:2 * d], a0[1], True, nm("conv_k_b"))
        dxv, dwv = conv_bwd(proj, cb(2 * d), cw[:, 2 * d:], a0[2], False, nm("conv_v_b"))
        grads["conv_w"][l] = jnp.concatenate([dwq, dwk, dwv], axis=1)
        dproj = jnp.concatenate([dxq, dxk, dxv, dz, dqk, dv_b, dg_b, dga, dgbt, dab,
                                 jnp.zeros((bsz, lp, lay.np - lay.used), BF16)], axis=-1)
        grads["w_in"][l] = matmul(flat(s["hn"]), flat(dproj), ta=True, name=nm("mmg_in"))
        dhn = unflat(matmul(flat(dproj), p["w_in"][l], tb=True, name=nm("mmb_in")))
        (dh,), (dg_mix,) = rowwise_vjp(nm("rms_mix_b"), f_rms, [Row(s["h_in"], d)], [], [s["g_mix"]], [dhn], 1, tmw,
                                       adds={0: dh_mid})
        grads["norm_mix"][l] = dg_mix[0]

    out = {k: jnp.stack(v) for k, v in grads.items()}
    out["norm_final"] = d_final[0]
    grad_x = dh[:, CHUNK:]
    grad_meta = jnp.sum(dh[:, N_PAD:CHUNK], axis=0)
    return loss_row, grad_x, grad_meta, out


BIG = ("w_in", "w_up_a", "w_up_b", "w_out", "w_ffn_in", "w_ffn_out")
BIG_AXES = (2, 1, 1, 1, 2, 1)
SMALL_SHARDED = ("meta_tokens", "conv_w")
SMALL_AXES = (1, 2)
REPLICATED = ("norm_mix", "gdn_a_log", "gdn_dt_bias", "gdn_norm", "ret_decay_logit", "norm_ffn", "norm_final")
WEIGHTS = ("meta_tokens", "norm_mix", "w_in", "conv_w", "gdn_a_log", "gdn_dt_bias", "gdn_norm", "ret_decay_logit",
           "w_up_a", "w_up_b", "w_out", "norm_ffn", "w_ffn_in", "w_ffn_out", "norm_final")


def kernel(x, meta_tokens, norm_mix, w_in, conv_w, gdn_a_log, gdn_dt_bias, gdn_norm, ret_decay_logit, w_up_a, w_up_b, w_out, norm_ffn, w_ffn_in, w_ffn_out, norm_final, loss_target, m_meta_tokens, m_norm_mix, m_w_in, m_conv_w, m_gdn_a_log, m_gdn_dt_bias, m_gdn_norm, m_ret_decay_logit, m_w_up_a, m_w_up_b, m_w_out, m_norm_ffn, m_w_ffn_in, m_w_ffn_out, m_norm_final, v_meta_tokens, v_norm_mix, v_w_in, v_conv_w, v_gdn_a_log, v_gdn_dt_bias, v_gdn_norm, v_ret_decay_logit, v_w_up_a, v_w_up_b, v_w_out, v_norm_ffn, v_w_ffn_in, v_w_ffn_out, v_norm_final):
    w = dict(meta_tokens=meta_tokens, norm_mix=norm_mix, w_in=w_in, conv_w=conv_w, gdn_a_log=gdn_a_log,
             gdn_dt_bias=gdn_dt_bias, gdn_norm=gdn_norm, ret_decay_logit=ret_decay_logit, w_up_a=w_up_a,
             w_up_b=w_up_b, w_out=w_out, norm_ffn=norm_ffn, w_ffn_in=w_ffn_in, w_ffn_out=w_ffn_out,
             norm_final=norm_final)
    m = dict(meta_tokens=m_meta_tokens, norm_mix=m_norm_mix, w_in=m_w_in, conv_w=m_conv_w, gdn_a_log=m_gdn_a_log,
             gdn_dt_bias=m_gdn_dt_bias, gdn_norm=m_gdn_norm, ret_decay_logit=m_ret_decay_logit, w_up_a=m_w_up_a,
             w_up_b=m_w_up_b, w_out=m_w_out, norm_ffn=m_norm_ffn, w_ffn_in=m_w_ffn_in, w_ffn_out=m_w_ffn_out,
             norm_final=m_norm_final)
    v = dict(meta_tokens=v_meta_tokens, norm_mix=v_norm_mix, w_in=v_w_in, conv_w=v_conv_w, gdn_a_log=v_gdn_a_log,
             gdn_dt_bias=v_gdn_dt_bias, gdn_norm=v_gdn_norm, ret_decay_logit=v_ret_decay_logit, w_up_a=v_w_up_a,
             w_up_b=v_w_up_b, w_out=v_w_out, norm_ffn=v_norm_ffn, w_ffn_in=v_w_ffn_in, w_ffn_out=v_w_ffn_out,
             norm_final=v_norm_final)
    d = x.shape[-1]
    lay = Layout(d)
    slot_axes = [1 if ax == 1 else 0 for ax in BIG_AXES]

    gathered = all_gather_hbm([w[k].astype(BF16) for k in BIG], slot_axes, "gather_weights")
    full = {}
    for k, gk, sa in zip(BIG, gathered, slot_axes):
        if sa == 1:
            full[k] = gk.reshape(gk.shape[0], -1, gk.shape[3])
        else:
            full[k] = jnp.concatenate([gk[s] for s in range(N_DEV)], axis=2)
    small_shapes = [w[k].shape for k in SMALL_SHARDED]
    gathered_s = all_gather_small(_pack([w[k] for k in SMALL_SHARDED], 8), "gather_small")
    full.update(zip(SMALL_SHARDED, _gathered_full(gathered_s, small_shapes, SMALL_AXES)))
    p = {k: w[k] for k in REPLICATED}
    p.update({k: full[k] for k in BIG + ("conv_w",)})
    p["w_in"] = jnp.stack([lay.relayout_w_in(full["w_in"][l]) for l in range(full["w_in"].shape[0])])

    loss_row, grad_x, grad_meta, g = local_step(x, loss_target, full["meta_tokens"], p)
    g["meta_tokens"] = grad_meta
    g["w_in"] = jnp.stack([lay.unlayout_w_in(g["w_in"][l]) for l in range(g["w_in"].shape[0])])

    pieces = []
    for k, sa in zip(BIG, slot_axes):
        nl, rows, cols = g[k].shape
        if sa == 1:
            pieces.append(g[k].astype(BF16).reshape(nl, N_DEV, rows // N_DEV, cols))
        else:
            cs = cols // N_DEV
            pieces.append(jnp.stack([g[k][:, :, s * cs:(s + 1) * cs] for s in range(N_DEV)]).astype(BF16))
    mine, theirs = sibling_exchange(pieces, slot_axes, "exchange_sibling")
    partials = [partial_sums(a, b, sa, "chip_sum_" + k) for k, a, b, sa in zip(BIG, mine, theirs, slot_axes)]
    landed = chip_exchange(partials, slot_axes, "exchange_chips")
    outs = {kind: {} for kind in ("g", "d", "m", "v")}
    for k, sa, got in zip(BIG, slot_axes, landed):
        for kind, r in zip(("g", "d", "m", "v"), adamw_blocks(w[k], m[k], v[k], got, sa, "adamw_" + k)):
            outs[kind][k] = r
    small_pieces = _pieces_by_dest([g[k] for k in SMALL_SHARDED], SMALL_AXES, 8)
    received_small = all_to_all_hbm([small_pieces], [0], "exchange_small")[0]
    res = adamw(_pack([w[k] for k in SMALL_SHARDED], 8), _pack([m[k] for k in SMALL_SHARDED], 8),
                _pack([v[k] for k in SMALL_SHARDED], 8), received_small, "adamw_small")
    for kind, r in zip(("g", "d", "m", "v"), res):
        outs[kind].update(zip(SMALL_SHARDED, _unpack(r, small_shapes)))

    rep_shapes = [w[k].shape for k in REPLICATED]
    part = _pack([g[k] for k in REPLICATED] + [loss_row[0, :1]], 8)
    parts = all_gather_small(part, "gather_replicated")
    pad1 = lambda a: _pack([a[k] for k in REPLICATED] + [jnp.zeros((1,), F32)], 8)
    res_r = adamw(pad1(w), pad1(m), pad1(v), parts, "adamw_replicated")
    for kind, r in zip(("g", "d", "m", "v"), res_r):
        outs[kind].update(zip(REPLICATED, _unpack(r, rep_shapes)))
    loss = res_r[0].reshape(-1)[sum(math.prod(s) for s in rep_shapes)]

    return (loss, grad_x, *[outs["g"][k] for k in WEIGHTS], *[outs["d"][k] for k in WEIGHTS],
            *[outs["m"][k] for k in WEIGHTS], *[outs["v"][k] for k in WEIGHTS])
```

```python
import functools
import math

import jax
import jax.numpy as jnp
from jax import lax
from jax.experimental import pallas as pl
from jax.experimental.pallas import tpu as pltpu

F32 = jnp.float32
BF16 = jnp.bfloat16
HIGHEST = lax.Precision.HIGHEST

LANES = 128
CHUNK = 64
N_META = 16
N_PAD = CHUNK - N_META
CONV_K = 5
EPS = 1e-6
ROPE_BASE = 10000.0
N_DEV = 8
VMEM_LIMIT = 56 * 1024 * 1024
MATMUL_VMEM = 40 * 1024 * 1024

ADAM_LR, ADAM_B1, ADAM_B2, ADAM_EPS, ADAM_WD, ADAM_STEP = 0.001, 0.9, 0.999, 1e-08, 0.01, 10


def _tile(n, cap, mult):
    if n <= cap:
        return n
    best = None
    for t in range(mult, cap + 1, mult):
        if n % t == 0:
            best = t
    assert best is not None, (n, cap, mult)
    return best


def _params(sem):
    return pltpu.CompilerParams(dimension_semantics=sem, vmem_limit_bytes=VMEM_LIMIT)


def _raw_dot(a, b, ca, cb, exact):
    dims = (((ca,), (cb,)), ((), ()))
    a_hi, b_hi = a.astype(BF16), b.astype(BF16)
    out = lax.dot_general(a_hi, b_hi, dims, preferred_element_type=F32)
    if exact:
        a_lo = (a - a_hi.astype(F32)).astype(BF16)
        b_lo = (b - b_hi.astype(F32)).astype(BF16)
        out = out + lax.dot_general(a_hi, b_lo, dims, preferred_element_type=F32)
        out = out + lax.dot_general(a_lo, b_hi, dims, preferred_element_type=F32)
    return out


@functools.partial(jax.custom_vjp, nondiff_argnums=(2, 3, 4))
def _dot(a, b, ca, cb, exact):
    return _raw_dot(a, b, ca, cb, exact)


def _dot_fwd(a, b, ca, cb, exact):
    return _raw_dot(a, b, ca, cb, exact), (a, b)


def _dot_bwd(ca, cb, exact, res, g):
    a, b = res
    if ca == 1:
        da = _raw_dot(g, b, 1, 1 if cb == 0 else 0, exact)
    else:
        da = _raw_dot(b, g, 1 if cb == 0 else 0, 1, exact)
    if cb == 0:
        db = _raw_dot(a, g, 0 if ca == 1 else 1, 0, exact)
    else:
        db = _raw_dot(g, a, 0, 0 if ca == 1 else 1, exact)
    return da, db


_dot.defvjp(_dot_fwd, _dot_bwd)


@functools.partial(jax.custom_vjp, nondiff_argnums=(1, 2))
def _split(x, n, axis):
    return lax.slice_in_dim(x, 0, n, axis=axis), lax.slice_in_dim(x, n, x.shape[axis], axis=axis)


_split.defvjp(lambda x, n, axis: (_split(x, n, axis), None),
              lambda n, axis, _, g: (jnp.concatenate([g[0], g[1]], axis=axis),))


def _bdot(a, b, ca=1, cb=0):
    return _dot(a, b, ca, cb, False)


def _xdot(a, b, ca=1, cb=0):
    return _dot(a, b, ca, cb, True)


def matmul(a, b, *, ta=False, tb=False, add=None, name):
    m, k = (a.shape[1], a.shape[0]) if ta else a.shape
    k2, n = (b.shape[1], b.shape[0]) if tb else b.shape
    assert k == k2, (a.shape, b.shape, ta, tb)
    has_add = add is not None
    tm_cap, tn_cap = 2080, 1408
    while True:
        tm = _tile(m, tm_cap, 128 if ta else 16)
        tn = _tile(n, tn_cap, 128)
        tk = _tile(k, 1664, 128 if (not ta or tb) else 16)
        nk = k // tk
        need = 2 * (tm * tk * a.dtype.itemsize + tk * tn * b.dtype.itemsize + tm * tn * 4 * (2 if has_add else 1))
        need += tm * tn * 4 if nk > 1 else 0
        if need <= MATMUL_VMEM or (tm_cap <= 256 and tn_cap <= 256):
            break
        if tm_cap >= tn_cap:
            tm_cap //= 2
        else:
            tn_cap //= 2
    ca, cb = (0 if ta else 1), (1 if tb else 0)
    a_spec = pl.BlockSpec((tk, tm), lambda i, j, kk: (kk, i)) if ta else pl.BlockSpec((tm, tk), lambda i, j, kk: (i, kk))
    b_spec = pl.BlockSpec((tn, tk), lambda i, j, kk: (j, kk)) if tb else pl.BlockSpec((tk, tn), lambda i, j, kk: (kk, j))
    o_spec = pl.BlockSpec((tm, tn), lambda i, j, kk: (i, j))

    def body(*refs):
        a_ref, b_ref = refs[0], refs[1]
        add_ref = refs[2] if has_add else None
        o_ref = refs[3] if has_add else refs[2]
        def part():
            return _raw_dot(a_ref[...], b_ref[...], ca, cb, False)

        if nk == 1:
            o_ref[...] = part() + add_ref[...] if has_add else part()
            return
        acc_ref = refs[-1]
        kk = pl.program_id(2)

        @pl.when(kk == 0)
        def _():
            acc_ref[...] = part()

        @pl.when(jnp.logical_and(kk > 0, kk < nk - 1))
        def _():
            acc_ref[...] += part()

        @pl.when(kk == nk - 1)
        def _():
            o_ref[...] = acc_ref[...] + part() + add_ref[...] if has_add else acc_ref[...] + part()

    ins = [a, b] + ([add] if has_add else [])
    in_specs = [a_spec, b_spec] + ([o_spec] if has_add else [])
    return pl.pallas_call(
        body, name=name, grid=(m // tm, n // tn, nk), in_specs=in_specs, out_specs=o_spec,
        out_shape=jax.ShapeDtypeStruct((m, n), F32),
        scratch_shapes=[pltpu.VMEM((tm, tn), F32)] if nk > 1 else [],
        compiler_params=_params(("parallel", "parallel", "arbitrary")),
    )(*ins)


class Row:
    def __init__(self, arr, bc, off=0, per_head=True, diff=True):
        self.arr, self.bc, self.off, self.per_head, self.diff = arr, bc, off, per_head, diff


class Tab:
    def __init__(self, arr):
        self.arr = arr


def _row_specs(rows, tabs, pars, tm):
    specs = []
    for r in rows:
        specs.append(pl.BlockSpec((1, tm, r.bc), functools.partial(
            lambda b, i, h, off, ph: (b, i, off + (h if ph else 0)), off=r.off, ph=r.per_head)))
    for t in tabs:
        specs.append(pl.BlockSpec((tm, t.arr.shape[1]), lambda b, i, h: (i, 0)))
    for p in pars:
        specs.append(pl.BlockSpec(p.shape, lambda b, i, h: (0, 0)))
    return specs


def rowwise(name, fn, rows, tabs, pars, outs, nh, tm, out_dtype=F32):
    bsz, lp = rows[0].arr.shape[:2]
    nr, nt, npar = len(rows), len(tabs), len(pars)

    def body(*refs):
        t0 = pl.program_id(1) * tm
        ins = [refs[k][0] for k in range(nr)] + [refs[nr + k][...] for k in range(nt + npar)]
        res = fn(t0, *ins)
        for o_ref, o in zip(refs[nr + nt + npar:], res):
            o_ref[0] = o.astype(o_ref.dtype)

    return pl.pallas_call(
        body, name=name, grid=(bsz, lp // tm, nh),
        in_specs=_row_specs(rows, tabs, pars, tm),
        out_specs=[pl.BlockSpec((1, tm, bc), lambda b, i, h: (b, i, h)) for _, bc in outs],
        out_shape=[jax.ShapeDtypeStruct((bsz, lp, c), out_dtype) for c, _ in outs],
        compiler_params=_params(("parallel", "parallel", "parallel")),
    )(*[r.arr for r in rows], *[t.arr for t in tabs], *pars)


def rowwise_vjp(name, fn, rows, tabs, pars, couts, nh, tm, adds=None, narrow=()):
    bsz, lp = rows[0].arr.shape[:2]
    nr, nt, npar, nco = len(rows), len(tabs), len(pars), len(couts)
    adds = adds or {}
    add_keys = sorted(adds)
    diff_idx = [k for k, r in enumerate(rows) if r.diff]
    for k in diff_idx:
        assert rows[k].per_head or nh == 1

    def body(*refs):
        b, i, h = pl.program_id(0), pl.program_id(1), pl.program_id(2)
        t0 = i * tm
        pos = 0
        row_v = [refs[k][0] for k in range(nr)]
        pos += nr
        tab_v = [refs[pos + k][...] for k in range(nt)]
        pos += nt
        par_v = [refs[pos + k][...] for k in range(npar)]
        pos += npar
        co_v = [refs[pos + k][0] for k in range(nco)]
        pos += nco
        add_v = {key: refs[pos + k][0] for k, key in enumerate(add_keys)}
        pos += len(add_keys)
        drow_refs = refs[pos:pos + len(diff_idx)]
        dpar_refs = refs[pos + len(diff_idx):]

        def f(dvals, pvals):
            full = list(row_v)
            for k, v in zip(diff_idx, dvals):
                full[k] = v
            return tuple(fn(t0, *full, *tab_v, *pvals))

        _, pull = jax.vjp(f, [row_v[k] for k in diff_idx], par_v)
        d_rows, d_pars = pull(tuple(co_v))
        for ref, k, d in zip(drow_refs, diff_idx, d_rows):
            ref[0] = (d + add_v[k] if k in add_v else d).astype(ref.dtype)
        first = jnp.logical_and(jnp.logical_and(b == 0, i == 0), h == 0)
        for ref, d in zip(dpar_refs, d_pars):
            @pl.when(first)
            def _(ref=ref, d=d):
                ref[...] = d

            @pl.when(jnp.logical_not(first))
            def _(ref=ref, d=d):
                ref[...] += d

    out_block = lambda bc: pl.BlockSpec((1, tm, bc), lambda b, i, h: (b, i, h))
    in_specs = _row_specs(rows, tabs, pars, tm)
    in_specs += [out_block(c.shape[2] // nh) for c in couts]
    in_specs += [out_block(rows[k].bc) for k in add_keys]
    out_specs = [out_block(rows[k].bc) for k in diff_idx]
    out_specs += [pl.BlockSpec(p.shape, lambda b, i, h: (0, 0)) for p in pars]
    out_shape = [jax.ShapeDtypeStruct((bsz, lp, nh * rows[k].bc), BF16 if k in narrow else F32) for k in diff_idx]
    out_shape += [jax.ShapeDtypeStruct(p.shape, F32) for p in pars]
    res = pl.pallas_call(
        body, name=name, grid=(bsz, lp // tm, nh), in_specs=in_specs, out_specs=out_specs, out_shape=out_shape,
        compiler_params=_params(("arbitrary", "arbitrary", "arbitrary")),
    )(*[r.arr for r in rows], *[t.arr for t in tabs], *pars, *couts, *[adds[k] for k in add_keys])
    return res[:len(diff_idx)], res[len(diff_idx):]


def _real_rows(t0, tm):
    return (t0 + lax.broadcasted_iota(jnp.int32, (tm, 1), 0)) >= N_PAD


def f_rms(t0, x, gain):
    return (x * lax.rsqrt(jnp.mean(x * x, axis=-1, keepdims=True) + EPS) * gain,)


def make_f_gb(heads):
    def f_gb(t0, ab, alog, dtb):
        lane = lax.broadcasted_iota(jnp.int32, ab.shape, 1)
        g = -jnp.exp(alog) * jax.nn.softplus(ab + dtb)
        beta = jax.nn.sigmoid(ab)
        out = jnp.where(lane < 2 * heads, g, jnp.where(lane < 4 * heads, beta, 0.0))
        return (jnp.where(_real_rows(t0, ab.shape[0]), out, 0.0),)
    return f_gb


def f_gdn_out(t0, o, z, gain):
    on = o * lax.rsqrt(jnp.mean(o * o, axis=-1, keepdims=True) + EPS)
    return (on * gain * jax.nn.silu(z),)


def f_ret_out(t0, o, g):
    on = o * lax.rsqrt(jnp.mean(o * o, axis=-1, keepdims=True) + EPS)
    return (on * jax.nn.silu(g),)


def f_merge(t0, ga, gb, ya, yb):
    return (jax.nn.sigmoid(ga) * ya + jax.nn.sigmoid(gb) * yb,)


def f_swiglu(t0, x):
    gate, up = _split(x, x.shape[1] // 2, 1)
    return (jax.nn.silu(gate) * up,)


def make_f_rope(sign):
    def f_rope(t0, x, cos, sin):
        half = x.shape[1] // 2
        x1, x2 = x[:, :half], x[:, half:]
        s = sin * sign
        return (jnp.concatenate([x1 * cos - x2 * s, x1 * s + x2 * cos], axis=1),)
    return f_rope


def loss_head(h, gain, target, tm):
    bsz, lp, d = h.shape

    def body(h_ref, g_ref, t_ref, dh_ref, dg_ref, loss_ref):
        b, i = pl.program_id(0), pl.program_id(1)
        rows = (i * tm + lax.broadcasted_iota(jnp.int32, (tm, 1), 0)) >= CHUNK
        tgt = t_ref[0]

        def f(x, gain_v):
            y = f_rms(0, x, gain_v)[0]
            err = jnp.where(rows, y - tgt, 0.0)
            return 0.5 * jnp.sum(jnp.mean(err * err, axis=-1, keepdims=True), keepdims=True)

        val, pull = jax.vjp(f, h_ref[0], g_ref[...])
        dh, dg = pull(jnp.ones((1, 1), F32))
        dh_ref[0] = dh
        first = jnp.logical_and(b == 0, i == 0)
        val_row = jnp.broadcast_to(val, (1, LANES))

        @pl.when(first)
        def _():
            dg_ref[...] = dg
            loss_ref[...] = val_row

        @pl.when(jnp.logical_not(first))
        def _():
            dg_ref[...] += dg
            loss_ref[...] += val_row

    blk = pl.BlockSpec((1, tm, d), lambda b, i: (b, i, 0))
    return pl.pallas_call(
        body, name="loss_head", grid=(bsz, lp // tm),
        in_specs=[blk, pl.BlockSpec((1, d), lambda b, i: (0, 0)), blk],
        out_specs=[blk, pl.BlockSpec((1, d), lambda b, i: (0, 0)), pl.BlockSpec((1, LANES), lambda b, i: (0, 0))],
        out_shape=[jax.ShapeDtypeStruct((bsz, lp, d), F32), jax.ShapeDtypeStruct((1, d), F32),
                   jax.ShapeDtypeStruct((1, LANES), F32)],
        compiler_params=_params(("arbitrary", "arbitrary")),
    )(h, gain, target)


def _conv_pre(x, w_ref):
    lp = x.shape[0]
    acc = w_ref[2:3, :] * x
    for k in (0, 1, 3, 4):
        acc = acc + w_ref[k:k + 1, :] * pltpu.roll(x, (2 - k) % lp, 0)
    return acc


def conv_fwd(proj, off, w, l2, name):
    bsz, lp, _ = proj.shape
    d = w.shape[1]

    def body(x_ref, w_ref, o_ref):
        x = x_ref[0]
        s = jnp.where(_real_rows(0, lp), jax.nn.silu(_conv_pre(x, w_ref)), 0.0)
        if l2:
            s = s * lax.rsqrt(jnp.sum(s * s, axis=-1, keepdims=True) + EPS)
        o_ref[0] = s

    return pl.pallas_call(
        body, name=name, grid=(d // LANES, bsz),
        in_specs=[pl.BlockSpec((1, lp, LANES), lambda j, b: (b, 0, off + j)),
                  pl.BlockSpec((CONV_K, LANES), lambda j, b: (0, j))],
        out_specs=pl.BlockSpec((1, lp, LANES), lambda j, b: (b, 0, j)),
        out_shape=jax.ShapeDtypeStruct((bsz, lp, d), F32),
        compiler_params=_params(("parallel", "parallel")),
    )(proj, w)


def conv_bwd(proj, off, w, dy, l2, name):
    bsz, lp, _ = proj.shape
    d = w.shape[1]

    def body(x_ref, w_ref, dy_ref, dx_ref, dw_ref):
        b = pl.program_id(1)
        x, g = x_ref[0], dy_ref[0]
        real = _real_rows(0, lp)
        c = _conv_pre(x, w_ref)
        sg = jax.nn.sigmoid(c)
        s = jnp.where(real, c * sg, 0.0)
        if l2:
            r = lax.rsqrt(jnp.sum(s * s, axis=-1, keepdims=True) + EPS)
            g = r * g - s * (r * r * r) * jnp.sum(g * s, axis=-1, keepdims=True)
        dc = jnp.where(real, g * (sg * (1.0 + c * (1.0 - sg))), 0.0)
        dx = w_ref[2:3, :] * dc
        for k in (0, 1, 3, 4):
            dx = dx + w_ref[k:k + 1, :] * pltpu.roll(dc, (k - 2) % lp, 0)
        dx_ref[0] = jnp.where(real, dx, 0.0).astype(dx_ref.dtype)
        tap_row = lax.broadcasted_iota(jnp.int32, (CONV_K, LANES), 0)
        dw = jnp.zeros((CONV_K, LANES), F32)
        for k in range(CONV_K):
            xs = x if k == 2 else pltpu.roll(x, (2 - k) % lp, 0)
            dw = dw + jnp.where(tap_row == k, jnp.sum(dc * xs, axis=0, keepdims=True), 0.0)

        @pl.when(b == 0)
        def _():
            dw_ref[...] = dw

        @pl.when(b > 0)
        def _():
            dw_ref[...] += dw

    blk = pl.BlockSpec((1, lp, LANES), lambda j, b: (b, 0, j))
    return pl.pallas_call(
        body, name=name, grid=(d // LANES, bsz),
        in_specs=[pl.BlockSpec((1, lp, LANES), lambda j, b: (b, 0, off + j)),
                  pl.BlockSpec((CONV_K, LANES), lambda j, b: (0, j)), blk],
        out_specs=[blk, pl.BlockSpec((CONV_K, LANES), lambda j, b: (0, j))],
        out_shape=[jax.ShapeDtypeStruct((bsz, lp, d), BF16), jax.ShapeDtypeStruct((CONV_K, d), F32)],
        compiler_params=_params(("parallel", "arbitrary")),
    )(proj, w, dy)


def _tri_masks(rev):
    ii = lax.broadcasted_iota(jnp.int32, (CHUNK, CHUNK), 0)
    jj = lax.broadcasted_iota(jnp.int32, (CHUNK, CHUNK), 1)
    return ((ii <= jj), (ii < jj)) if rev else ((ii >= jj), (ii > jj))


def _lane_pick(block, lane):
    sel = lax.broadcasted_iota(jnp.int32, block.shape, 1) == lane
    return jnp.sum(jnp.where(sel, block, 0.0), axis=1, keepdims=True)


def _cumsum_impl(x, rev):
    n = x.shape[0]
    row = lax.broadcasted_iota(jnp.int32, x.shape, 0)
    step = 1
    while step < n:
        if rev:
            x = x + jnp.where(row < n - step, pltpu.roll(x, n - step, 0), 0.0)
        else:
            x = x + jnp.where(row >= step, pltpu.roll(x, step, 0), 0.0)
        step *= 2
    return x


@functools.partial(jax.custom_vjp, nondiff_argnums=(1,))
def _cumsum_rows(x, rev):
    return _cumsum_impl(x, rev)


_cumsum_rows.defvjp(lambda x, rev: (_cumsum_impl(x, rev), None),
                    lambda rev, _, g: (_cumsum_impl(g, not rev),))


def _unit_inv_impl(m):
    n = m.shape[0]
    eye = (lax.broadcasted_iota(jnp.int32, (n, n), 0) == lax.broadcasted_iota(jnp.int32, (n, n), 1)).astype(F32)
    p = -m
    inv = eye + p
    step = 2
    while step < n:
        p = _raw_dot(p, p, 1, 0, True)
        inv = inv + _raw_dot(inv, p, 1, 0, True)
        step *= 2
    return inv


@jax.custom_vjp
def _unit_inv(m):
    return _unit_inv_impl(m)


def _unit_inv_fwd(m):
    inv = _unit_inv_impl(m)
    return inv, inv


def _unit_inv_bwd(inv, g):
    return (-_raw_dot(_raw_dot(inv, g, 0, 0, True), inv, 1, 1, True),)


_unit_inv.defvjp(_unit_inv_fwd, _unit_inv_bwd)


GROUP = 2


def _unit_inv_all_impl(ms):
    n = ms[0].shape[0]
    width = GROUP * n
    dims = (((1,), (0,)), ((), ()))
    lane = lax.broadcasted_iota(jnp.int32, (n, width), 1)
    row = lax.broadcasted_iota(jnp.int32, (n, width), 0)
    blocks = [jnp.logical_and(lane >= t * n, lane < (t + 1) * n) for t in range(GROUP)]

    def halves(x):
        hi = x.astype(BF16)
        return hi, (x - hi.astype(F32)).astype(BF16)

    def xdot(a, b):
        diag = jnp.concatenate([jnp.where(blk, b, 0.0) for blk in blocks], axis=0)
        a_hi, a_lo = halves(a)
        d_hi, d_lo = halves(diag)
        out = lax.dot_general(a_hi, d_hi, dims, preferred_element_type=F32)
        out = out + lax.dot_general(a_hi, d_lo, dims, preferred_element_type=F32)
        return out + lax.dot_general(a_lo, d_hi, dims, preferred_element_type=F32)

    eye = (row == lane % n).astype(F32)
    groups = range(len(ms) // GROUP)
    p = [-jnp.concatenate(ms[t * GROUP:(t + 1) * GROUP], axis=1) for t in groups]
    inv = [eye + p[t] for t in groups]
    step = 2
    while step < n:
        p = [xdot(p[t], p[t]) for t in groups]
        inv = [inv[t] + xdot(inv[t], p[t]) for t in groups]
        step *= 2
    return [inv[t][:, u * n:(u + 1) * n] for t in groups for u in range(GROUP)]


@jax.custom_vjp
def _unit_inv_all(ms):
    return _unit_inv_all_impl(ms)


def _unit_inv_all_fwd(ms):
    out = _unit_inv_all_impl(ms)
    return out, out


_unit_inv_all.defvjp(_unit_inv_all_fwd, lambda invs, g: ([_unit_inv_bwd(i, gi)[0] for i, gi in zip(invs, g)],))


@jax.custom_vjp
def _unit_inv_known(m, inv):
    return inv


_unit_inv_known.defvjp(lambda m, inv: (inv, inv),
                       lambda inv, g: (_unit_inv_bwd(inv, g)[0], jnp.zeros_like(inv)))


def _gdn_chunk(qs, ks, vs, gb, ss, g_lanes, b_lanes, rev, invs=None, want_inv=False):
    nh = len(qs)
    dk = qs[0].shape[1]
    incl, strict = _tri_masks(rev)
    hs = range(nh)
    g = [_lane_pick(gb, l) for l in g_lanes]
    beta = [_lane_pick(gb, l) for l in b_lanes]
    qs = [q * (dk ** -0.5) for q in qs]
    gc_sq = [_cumsum_rows(jnp.broadcast_to(g[h], (CHUNK, CHUNK)), rev) for h in hs]
    gc = [_cumsum_rows(jnp.broadcast_to(g[h], (CHUNK, dk)), rev) for h in hs]
    g_last = [jnp.sum(g[h], axis=0, keepdims=True) for h in hs]
    decay = [jnp.where(incl, jnp.exp(jnp.where(incl, gc_sq[h] - gc_sq[h].T, 0.0)), 0.0) for h in hs]
    kb = [ks[h] * beta[h] for h in hs]
    kk = [_split(_bdot(jnp.concatenate([kb[h], qs[h]], axis=0), ks[h], 1, 1), CHUNK, 0) for h in hs]
    m = [jnp.where(strict, kk[h][0] * decay[h], 0.0) for h in hs]
    qk = [kk[h][1] * decay[h] for h in hs]
    if invs is not None:
        inv = [_unit_inv_known(m[h], invs[h]) for h in hs]
    elif nh % GROUP == 0:
        inv = _unit_inv_all(m)
    else:
        inv = [_unit_inv(m[h]) for h in hs]
    e_gc = [jnp.exp(gc[h]) for h in hs]
    uw = [_split(_bdot(inv[h], jnp.concatenate([vs[h] * beta[h], kb[h] * e_gc[h]], axis=1)), vs[h].shape[1], 1)
          for h in hs]
    u = [uw[h][0] for h in hs]
    w = [uw[h][1] for h in hs]
    q_dec = [qs[h] * e_gc[h] for h in hs]
    k_dec = [ks[h] * jnp.exp(g_last[h] - gc[h]) for h in hs]
    ws = [_split(_bdot(jnp.concatenate([w[h], q_dec[h]], axis=0), ss[h]), CHUNK, 0) for h in hs]
    v_new = [u[h] - ws[h][0] for h in hs]
    o = [ws[h][1] + _bdot(qk[h], v_new[h]) for h in hs]
    s_new = [ss[h] * jnp.exp(g_last[h]) + _bdot(k_dec[h], v_new[h], 0, 0) for h in hs]
    return (o, s_new, inv) if want_inv else (o, s_new)


def gdn_fwd(q, k, v, gb, heads, direction, o_prev, name):
    bsz, lp, d = q.shape
    nc = lp // CHUNK
    rev = direction == 1
    cm = (lambda n: nc - 1 - n) if rev else (lambda n: n)
    has_prev = o_prev is not None

    def body(*refs):
        q_ref, k_ref, v_ref, gb_ref = refs[:4]
        prev_ref = refs[4] if has_prev else None
        o_ref, st_ref, inv_ref, s_ref = refs[-4:]
        n = pl.program_id(1)

        @pl.when(n == 0)
        def _():
            s_ref[...] = jnp.zeros_like(s_ref)

        sls = [slice(h * LANES, (h + 1) * LANES) for h in range(heads)]
        ss = [s_ref[h] for h in range(heads)]
        for h in range(heads):
            st_ref[0, 0, h] = ss[h]
        os_, s_new, inv = _gdn_chunk([q_ref[0, :, sl] for sl in sls], [k_ref[0, :, sl] for sl in sls],
                                     [v_ref[0, :, sl] for sl in sls], gb_ref[0], ss,
                                     [direction * heads + h for h in range(heads)],
                                     [2 * heads + direction * heads + h for h in range(heads)], rev, want_inv=True)
        for h, sl in enumerate(sls):
            s_ref[h] = s_new[h]
            inv_ref[0, 0, h] = inv[h]
            o_ref[0, :, sl] = os_[h] + prev_ref[0, :, sl] if has_prev else os_[h]

    blk = pl.BlockSpec((1, CHUNK, d), lambda b, n: (b, cm(n), 0))
    gblk = pl.BlockSpec((1, CHUNK, LANES), lambda b, n: (b, cm(n), 0))
    st_blk = pl.BlockSpec((1, 1, heads, LANES, LANES), lambda b, n: (b, cm(n), 0, 0, 0))
    inv_blk = pl.BlockSpec((1, 1, heads, CHUNK, CHUNK), lambda b, n: (b, cm(n), 0, 0, 0))
    return pl.pallas_call(
        body, name=name, grid=(bsz, nc),
        in_specs=[blk, blk, blk, gblk] + ([blk] if has_prev else []),
        out_specs=[blk, st_blk, inv_blk],
        out_shape=[jax.ShapeDtypeStruct((bsz, lp, d), F32),
                   jax.ShapeDtypeStruct((bsz, nc, heads, LANES, LANES), F32),
                   jax.ShapeDtypeStruct((bsz, nc, heads, CHUNK, CHUNK), F32)],
        scratch_shapes=[pltpu.VMEM((heads, LANES, LANES), F32)],
        compiler_params=_params(("parallel", "arbitrary")),
    )(q, k, v, gb, *([o_prev] if has_prev else []))


def gdn_bwd(q, k, v, gb, states, invs, do, heads, direction, prev, name):
    bsz, lp, d = q.shape
    nc = lp // CHUNK
    rev = direction == 1
    cm = (lambda n: n) if rev else (lambda n: nc - 1 - n)
    has_prev = prev is not None

    def body(*refs):
        q_ref, k_ref, v_ref, gb_ref, st_ref, inv_ref, do_ref = refs[:7]
        prev_refs = refs[7:11] if has_prev else None
        dq_ref, dk_ref, dv_ref, dgb_ref, ds_ref = refs[-5:]
        n = pl.program_id(1)

        @pl.when(n == 0)
        def _():
            ds_ref[...] = jnp.zeros_like(ds_ref)

        sls = [slice(h * LANES, (h + 1) * LANES) for h in range(heads)]
        f = functools.partial(_gdn_chunk, g_lanes=[direction * heads + h for h in range(heads)],
                              b_lanes=[2 * heads + direction * heads + h for h in range(heads)], rev=rev,
                              invs=[inv_ref[0, 0, h] for h in range(heads)])
        _, pull = jax.vjp(f, [q_ref[0, :, sl] for sl in sls], [k_ref[0, :, sl] for sl in sls],
                          [v_ref[0, :, sl] for sl in sls], gb_ref[0], [st_ref[0, 0, h] for h in range(heads)])
        dq, dk, dv, dgb, ds = pull(([do_ref[0, :, sl] for sl in sls], [ds_ref[h] for h in range(heads)]))
        for h, sl in enumerate(sls):
            ds_ref[h] = ds[h]
            if has_prev:
                dq[h], dk[h], dv[h] = (dq[h] + prev_refs[0][0, :, sl], dk[h] + prev_refs[1][0, :, sl],
                                       dv[h] + prev_refs[2][0, :, sl])
            dq_ref[0, :, sl] = dq[h]
            dk_ref[0, :, sl] = dk[h]
            dv_ref[0, :, sl] = dv[h]
        dgb_ref[0] = dgb + prev_refs[3][0] if has_prev else dgb

    blk = pl.BlockSpec((1, CHUNK, d), lambda b, n: (b, cm(n), 0))
    gblk = pl.BlockSpec((1, CHUNK, LANES), lambda b, n: (b, cm(n), 0))
    st_blk = pl.BlockSpec((1, 1, heads, LANES, LANES), lambda b, n: (b, cm(n), 0, 0, 0))
    inv_blk = pl.BlockSpec((1, 1, heads, CHUNK, CHUNK), lambda b, n: (b, cm(n), 0, 0, 0))
    big = jax.ShapeDtypeStruct((bsz, lp, d), F32)
    return pl.pallas_call(
        body, name=name, grid=(bsz, nc),
        in_specs=[blk, blk, blk, gblk, st_blk, inv_blk, blk] + ([blk, blk, blk, gblk] if has_prev else []),
        out_specs=[blk, blk, blk, gblk],
        out_shape=[big, big, big, jax.ShapeDtypeStruct((bsz, lp, LANES), F32)],
        scratch_shapes=[pltpu.VMEM((heads, LANES, LANES), F32)],
        compiler_params=_params(("parallel", "arbitrary")),
    )(q, k, v, gb, states, invs, do, *(list(prev) if has_prev else []))


def _ret_chunk(q, k, v, r, logit, lane, rev):
    dk = q.shape[1]
    lg = jax.nn.log_sigmoid(_lane_pick(logit, lane))
    k = k * (dk ** -0.5)
    ii = lax.broadcasted_iota(jnp.int32, (CHUNK, CHUNK), 0)
    jj = lax.broadcasted_iota(jnp.int32, (CHUNK, CHUNK), 1)
    pos = lax.broadcasted_iota(jnp.int32, (CHUNK, 1), 0)
    if rev:
        incl, rel = ii <= jj, (jj - ii)
        seen = (CHUNK - 1 - pos)
    else:
        incl, rel = ii >= jj, (ii - jj)
        seen = pos
    relf = jnp.where(incl, rel, 0).astype(F32)
    seenf = seen.astype(F32)
    intra = jnp.where(incl, jnp.exp(relf * lg), 0.0)
    qk = _bdot(q, k, 1, 1) * intra
    q_dec = q * jnp.exp(lg * (seenf + 1.0))
    k_dec = k * jnp.exp(lg * (CHUNK - 1.0 - seenf))
    o = _bdot(q_dec, r) + _bdot(qk, v)
    r_new = r * jnp.exp(lg * CHUNK) + _bdot(k_dec, v, 0, 0)
    return o, r_new


def ret_fwd(qk, v_arr, v_off, logit, heads, direction, o_prev, name):
    bsz, lp, d2 = qk.shape
    d = d2 // 2
    dkh, dvh = d // heads, 2 * d // heads
    nc = lp // CHUNK
    rev = direction == 1
    cm = (lambda n: nc - 1 - n) if rev else (lambda n: n)
    has_prev = o_prev is not None
    v_cb = v_off * LANES // (2 * d)
    assert v_cb * 2 * d == v_off * LANES

    def body(*refs):
        q_ref, k_ref, v_ref, lg_ref = refs[:4]
        prev_ref = refs[4] if has_prev else None
        o_ref, st_ref, r_ref = refs[-3], refs[-2], refs[-1]
        n = pl.program_id(1)

        @pl.when(n == 0)
        def _():
            r_ref[...] = jnp.zeros_like(r_ref)

        lgv = lg_ref[...]
        for h in range(heads):
            ks, vs = slice(h * dkh, (h + 1) * dkh), slice(h * dvh, (h + 1) * dvh)
            r = r_ref[h]
            st_ref[0, 0, h] = r.astype(st_ref.dtype)
            o, r_new = _ret_chunk(q_ref[0, :, ks], k_ref[0, :, ks], v_ref[0, :, vs], r, lgv, h, rev)
            r_ref[h] = r_new
            o_ref[0, :, vs] = o + prev_ref[0, :, vs] if has_prev else o

    qblk = pl.BlockSpec((1, CHUNK, d), lambda b, n: (b, cm(n), 0))
    kblk = pl.BlockSpec((1, CHUNK, d), lambda b, n: (b, cm(n), 1))
    vblk = pl.BlockSpec((1, CHUNK, 2 * d), lambda b, n: (b, cm(n), v_cb))
    oblk = pl.BlockSpec((1, CHUNK, 2 * d), lambda b, n: (b, cm(n), 0))
    st_blk = pl.BlockSpec((1, 1, heads, dkh, dvh), lambda b, n: (b, cm(n), 0, 0, 0))
    return pl.pallas_call(
        body, name=name, grid=(bsz, nc),
        in_specs=[qblk, kblk, vblk, pl.BlockSpec((1, LANES), lambda b, n: (0, 0))] + ([oblk] if has_prev else []),
        out_specs=[oblk, st_blk],
        out_shape=[jax.ShapeDtypeStruct((bsz, lp, 2 * d), F32),
                   jax.ShapeDtypeStruct((bsz, nc, heads, dkh, dvh), BF16)],
        scratch_shapes=[pltpu.VMEM((heads, dkh, dvh), F32)],
        compiler_params=_params(("parallel", "arbitrary")),
    )(qk, qk, v_arr, logit, *([o_prev] if has_prev else []))


def ret_bwd(qk, v_arr, v_off, logit, states, do, heads, direction, prev, name, dv_dtype=F32):
    bsz, lp, d2 = qk.shape
    d = d2 // 2
    dkh, dvh = d // heads, 2 * d // heads
    nc = lp // CHUNK
    rev = direction == 1
    cm = (lambda n: n) if rev else (lambda n: nc - 1 - n)
    has_prev = prev is not None
    v_cb = v_off * LANES // (2 * d)

    def body(*refs):
        q_ref, k_ref, v_ref, lg_ref, st_ref, do_ref = refs[:6]
        prev_refs = refs[6:8] if has_prev else None
        dqk_ref, dv_ref, dlg_ref, dr_ref = refs[-4:]
        b, n = pl.program_id(0), pl.program_id(1)

        @pl.when(n == 0)
        def _():
            dr_ref[...] = jnp.zeros_like(dr_ref)

        lgv = lg_ref[...]
        dlg = jnp.zeros((1, LANES), F32)
        for h in range(heads):
            ks, vs = slice(h * dkh, (h + 1) * dkh), slice(h * dvh, (h + 1) * dvh)
            f = functools.partial(_ret_chunk, lane=h, rev=rev)
            _, pull = jax.vjp(f, q_ref[0, :, ks], k_ref[0, :, ks], v_ref[0, :, vs], st_ref[0, 0, h].astype(F32), lgv)
            dq, dk, dv, dr, dlg_h = pull((do_ref[0, :, vs], dr_ref[h]))
            dr_ref[h] = dr
            dlg = dlg + dlg_h
            kks = slice(d + h * dkh, d + (h + 1) * dkh)
            if has_prev:
                dq, dk, dv = dq + prev_refs[0][0, :, ks], dk + prev_refs[0][0, :, kks], dv + prev_refs[1][0, :, vs]
            dqk_ref[0, :, ks] = dq
            dqk_ref[0, :, kks] = dk
            dv_ref[0, :, vs] = dv.astype(dv_ref.dtype)
        first = jnp.logical_and(b == 0, n == 0)

        @pl.when(first)
        def _():
            dlg_ref[...] = dlg

        @pl.when(jnp.logical_not(first))
        def _():
            dlg_ref[...] += dlg

    qblk = pl.BlockSpec((1, CHUNK, d), lambda b, n: (b, cm(n), 0))
    kblk = pl.BlockSpec((1, CHUNK, d), lambda b, n: (b, cm(n), 1))
    vblk = pl.BlockSpec((1, CHUNK, 2 * d), lambda b, n: (b, cm(n), v_cb))
    oblk = pl.BlockSpec((1, CHUNK, 2 * d), lambda b, n: (b, cm(n), 0))
    lblk = pl.BlockSpec((1, LANES), lambda b, n: (0, 0))
    st_blk = pl.BlockSpec((1, 1, heads, dkh, dvh), lambda b, n: (b, cm(n), 0, 0, 0))
    return pl.pallas_call(
        body, name=name, grid=(bsz, nc),
        in_specs=[qblk, kblk, vblk, lblk, st_blk, oblk] + ([oblk, oblk] if has_prev else []),
        out_specs=[oblk, oblk, lblk],
        out_shape=[jax.ShapeDtypeStruct((bsz, lp, 2 * d), F32), jax.ShapeDtypeStruct((bsz, lp, 2 * d), dv_dtype),
                   jax.ShapeDtypeStruct((1, LANES), F32)],
        scratch_shapes=[pltpu.VMEM((heads, dkh, dvh), F32)],
        compiler_params=_params(("arbitrary", "arbitrary")),
    )(qk, qk, v_arr, logit, states, do, *(list(prev) if has_prev else []))


def _flip(v, bit):
    return 1 - v if bit else v


def _peer(x, y, c, off):
    return (_flip(x, off & 4), _flip(y, off & 2), _flip(c, off & 1))


def _slot(ref, axis, idx):
    return ref.at[(slice(None),) * axis + (idx,)]


def _slotted_shape(shape, axis):
    return tuple(shape[:axis]) + (N_DEV,) + tuple(shape[axis:])


def all_gather_hbm(blocks, axes, name):
    n = len(blocks)

    def body(*refs):
        x_refs, out_refs = refs[:n], refs[n:2 * n]
        send_sems, recv_sems, local_sems = refs[2 * n:]
        x, y, c = lax.axis_index("x"), lax.axis_index("y"), lax.axis_index("c")
        me, sibling = (x, y, c), (x, y, 1 - c)
        chips = [(1 - x, y), (x, 1 - y), (1 - x, 1 - y)]

        def slot(k, px, py, pc):
            return _slot(out_refs[k], axes[k], 4 * px + 2 * py + pc)

        def copy(k, j, block_of, to, src=None):
            return pltpu.make_async_remote_copy(
                src_ref=slot(k, *block_of) if src is None else src, dst_ref=slot(k, *block_of),
                send_sem=send_sems.at[7 * k + j], recv_sem=recv_sems.at[7 * k + j], device_id=to,
                device_id_type=pl.DeviceIdType.MESH)

        ks = range(n)
        mine = [pltpu.make_async_copy(x_refs[k], slot(k, *me), local_sems.at[k]) for k in ks]
        for cp in mine:
            cp.start()
        first = [copy(k, 0, me, sibling, src=x_refs[k]) for k in ks]
        first += [copy(k, 1 + j, me, (*chip, c), src=x_refs[k]) for k in ks for j, chip in enumerate(chips)]
        for cp in first:
            cp.start()
        passed = []
        for j, chip in enumerate(chips):
            for k in ks:
                copy(k, 1 + j, (*chip, c), me).wait_recv()
                passed.append(copy(k, 4 + j, (*chip, c), sibling))
                passed[-1].start()
        for k in ks:
            copy(k, 0, sibling, me).wait_recv()
        for j, chip in enumerate(chips):
            for k in ks:
                copy(k, 4 + j, (*chip, 1 - c), me).wait_recv()
        for cp in first + passed:
            cp.wait_send()
        for cp in mine:
            cp.wait()

    hbm = pl.BlockSpec(memory_space=pl.ANY)
    return pl.pallas_call(
        body, name=name,
        out_shape=[jax.ShapeDtypeStruct(_slotted_shape(b.shape, ax), b.dtype) for b, ax in zip(blocks, axes)],
        in_specs=[hbm] * n, out_specs=[hbm] * n,
        scratch_shapes=[pltpu.SemaphoreType.DMA((7 * n,)), pltpu.SemaphoreType.DMA((7 * n,)),
                        pltpu.SemaphoreType.DMA((n,))],
    )(*blocks)


def all_gather_small(block, name):
    r, lanes = block.shape

    def body(x_ref, out_ref, send_sems, recv_sems):
        x, y, c = lax.axis_index("x"), lax.axis_index("y"), lax.axis_index("c")
        me = 4 * x + 2 * y + c
        out_ref[me] = x_ref[...]
        copies = []
        for off in range(1, N_DEV):
            copies.append(pltpu.make_async_remote_copy(
                src_ref=x_ref, dst_ref=out_ref.at[me], send_sem=send_sems.at[off - 1], recv_sem=recv_sems.at[off - 1],
                device_id=_peer(x, y, c, off), device_id_type=pl.DeviceIdType.MESH))
        for cp in copies:
            cp.start()
        for cp in copies:
            cp.wait()

    return pl.pallas_call(
        body, name=name, out_shape=jax.ShapeDtypeStruct((N_DEV, r, lanes), block.dtype),
        in_specs=[pl.BlockSpec(memory_space=pltpu.VMEM)], out_specs=pl.BlockSpec(memory_space=pltpu.VMEM),
        scratch_shapes=[pltpu.SemaphoreType.DMA((7,)), pltpu.SemaphoreType.DMA((7,))],
    )(block)


def all_to_all_hbm(pieces, axes, name):
    n = len(pieces)

    def body(*refs):
        x_refs, out_refs = refs[:n], refs[n:2 * n]
        send_sems, recv_sems, local_sems = refs[2 * n:]
        x, y, c = lax.axis_index("x"), lax.axis_index("y"), lax.axis_index("c")
        me = 4 * x + 2 * y + c
        mine = [pltpu.make_async_copy(_slot(x_refs[k], axes[k], me), _slot(out_refs[k], axes[k], me), local_sems.at[k])
                for k in range(n)]
        for cp in mine:
            cp.start()
        copies = []
        for off in range(1, N_DEV):
            px, py, pc = _peer(x, y, c, off)
            for k in range(n):
                copies.append(pltpu.make_async_remote_copy(
                    src_ref=_slot(x_refs[k], axes[k], 4 * px + 2 * py + pc), dst_ref=_slot(out_refs[k], axes[k], me),
                    send_sem=send_sems.at[7 * k + off - 1], recv_sem=recv_sems.at[7 * k + off - 1],
                    device_id=(px, py, pc), device_id_type=pl.DeviceIdType.MESH))
        for cp in copies:
            cp.start()
        for cp in copies:
            cp.wait()
        for cp in mine:
            cp.wait()

    hbm = pl.BlockSpec(memory_space=pl.ANY)
    return pl.pallas_call(
        body, name=name, out_shape=[jax.ShapeDtypeStruct(p.shape, p.dtype) for p in pieces],
        in_specs=[hbm] * n, out_specs=[hbm] * n,
        scratch_shapes=[pltpu.SemaphoreType.DMA((7 * n,)), pltpu.SemaphoreType.DMA((7 * n,)),
                        pltpu.SemaphoreType.DMA((n,))],
    )(*pieces)


def _adamw_update(w, m, v, pieces):
    g = pieces[0].astype(F32)
    for piece in pieces[1:]:
        g = g + piece.astype(F32)
    mn = ADAM_B1 * m + (1.0 - ADAM_B1) * g
    vn = ADAM_B2 * v + (1.0 - ADAM_B2) * (g * g)
    m_hat = mn / (1.0 - ADAM_B1 ** ADAM_STEP)
    v_hat = vn / (1.0 - ADAM_B2 ** ADAM_STEP)
    return g, -ADAM_LR * (m_hat / (jnp.sqrt(v_hat) + ADAM_EPS) + ADAM_WD * w), mn, vn


def adamw_blocks(w, m, v, parts, slot_axis, name):
    nl, r, c = w.shape
    ns = parts.shape[slot_axis]
    tr = _tile(r, 128, 16)

    def body(w_ref, m_ref, v_ref, g_ref, g_out, d_out, m_out, v_out):
        pieces = [g_ref[s, 0] if slot_axis == 0 else g_ref[0, s] for s in range(ns)]
        for ref, val in zip((g_out, d_out, m_out, v_out), _adamw_update(w_ref[0], m_ref[0], v_ref[0], pieces)):
            ref[0] = val

    blk = pl.BlockSpec((1, tr, c), lambda l, i: (l, i, 0))
    gblk = (pl.BlockSpec((ns, 1, tr, c), lambda l, i: (0, l, i, 0)) if slot_axis == 0
            else pl.BlockSpec((1, ns, tr, c), lambda l, i: (l, 0, i, 0)))
    shp = jax.ShapeDtypeStruct(w.shape, F32)
    return pl.pallas_call(
        body, name=name, grid=(nl, r // tr), in_specs=[blk, blk, blk, gblk],
        out_specs=[blk, blk, blk, blk], out_shape=[shp, shp, shp, shp],
        compiler_params=_params(("parallel", "parallel")),
    )(w, m, v, parts)


def adamw(w, m, v, g8, name):
    r = w.shape[0]
    tm = _tile(r, 1024, 8)

    def body(w_ref, m_ref, v_ref, g_ref, g_out, d_out, m_out, v_out):
        res = _adamw_update(w_ref[...], m_ref[...], v_ref[...], [g_ref[s] for s in range(N_DEV)])
        for ref, val in zip((g_out, d_out, m_out, v_out), res):
            ref[...] = val

    blk = pl.BlockSpec((tm, LANES), lambda i: (i, 0))
    shp = jax.ShapeDtypeStruct((r, LANES), F32)
    return pl.pallas_call(
        body, name=name, grid=(r // tm,),
        in_specs=[blk, blk, blk, pl.BlockSpec((N_DEV, tm, LANES), lambda i: (0, i, 0))],
        out_specs=[blk, blk, blk, blk], out_shape=[shp, shp, shp, shp],
        compiler_params=_params(("parallel",)),
    )(w, m, v, g8)


def _pack(blocks, rows_mult):
    flat = jnp.concatenate([b.reshape(-1) for b in blocks])
    unit = rows_mult * LANES
    total = -(-flat.shape[0] // unit) * unit
    return jnp.pad(flat, (0, total - flat.shape[0])).reshape(-1, LANES)


def _unpack(packed, shapes):
    flat = packed.reshape(-1)
    out, pos = [], 0
    for s in shapes:
        n = math.prod(s)
        out.append(flat[pos:pos + n].reshape(s))
        pos += n
    return out


def _gathered_full(gathered, shapes, axes):
    per_dev = [_unpack(gathered[d], shapes) for d in range(N_DEV)]
    return [jnp.concatenate([per_dev[d][k] for d in range(N_DEV)], axis=axes[k]) for k in range(len(shapes))]


def _pieces_by_dest(fulls, axes, rows_mult):
    packs = []
    for d in range(N_DEV):
        blocks = []
        for f, ax in zip(fulls, axes):
            n = f.shape[ax] // N_DEV
            blocks.append(lax.slice_in_dim(f, d * n, (d + 1) * n, axis=ax))
        packs.append(_pack(blocks, rows_mult))
    return jnp.stack(packs)


class Layout:
    def __init__(self, d):
        self.d = d
        self.h = d // 128
        self.hr = d // 256
        self.z = 3 * d
        self.qb = 4 * d
        self.vb = 6 * d
        self.gb = 8 * d
        self.ga = 10 * d
        self.gbt = 11 * d
        self.ab = 12 * d
        self.used = 12 * d + LANES
        self.np = -(-self.used // 512) * 512

    def relayout_w_in(self, w):
        d, h4 = self.d, 4 * self.h
        return jnp.concatenate([w[:, :4 * d], w[:, 4 * d + h4:], w[:, 4 * d:4 * d + h4],
                                jnp.zeros((d, self.np - 12 * d - h4), w.dtype)], axis=1)

    def unlayout_w_in(self, w):
        d, h4 = self.d, 4 * self.h
        return jnp.concatenate([w[:, :4 * d], w[:, 12 * d:12 * d + h4], w[:, 4 * d:12 * d]], axis=1)


def _lane_row(vec):
    return jnp.pad(vec.reshape(-1), (0, LANES - vec.size)).reshape(1, LANES)


def _rope_tables(lp, half):
    inv = ROPE_BASE ** (-jnp.arange(half, dtype=F32) / half)
    pos = (jnp.arange(lp) - N_PAD).astype(F32)
    ang = pos[:, None] * inv[None, :]
    return jnp.cos(ang), jnp.sin(ang)


def local_step(x, target, meta, p):
    bsz, seq, d = x.shape
    lay = Layout(d)
    h_gdn, h_ret = lay.h, lay.hr
    lp = seq + CHUNK
    t_all = bsz * lp
    depth = p["w_up_a"].shape[0]
    ff = p["w_ffn_out"].shape[1]
    tm = _tile(lp, 512, 16)
    tmw = _tile(lp, 256, 16)
    tmn = _tile(lp, 1040, 16)
    tmr = _tile(lp, 832, 16)
    cb = lambda cols: cols // LANES
    flat = lambda a: a.reshape(t_all, a.shape[-1])
    unflat = lambda a: a.reshape(bsz, lp, a.shape[-1])
    cos, sin = _rope_tables(lp, LANES)
    f_gb = make_f_gb(h_gdn)
    rope_f, rope_b = make_f_rope(1.0), make_f_rope(-1.0)

    head = jnp.concatenate([jnp.zeros((N_PAD, d), F32), meta], axis=0)
    h = jnp.concatenate([jnp.broadcast_to(head[None], (bsz, CHUNK, d)), x], axis=1)
    tgt = jnp.pad(target, ((0, 0), (CHUNK, 0), (0, 0)))

    saved = []
    for l in range(depth):
        s = {"h_in": h}
        nm = lambda k: f"l{l}_{k}"
        g_mix, g_ffn = p["norm_mix"][l][None], p["norm_ffn"][l][None]
        alog, dtb = _lane_row(p["gdn_a_log"][l]), _lane_row(p["gdn_dt_bias"][l])
        gain_a = p["gdn_norm"][l][None]
        logits = [_lane_row(p["ret_decay_logit"][l][0]), _lane_row(p["ret_decay_logit"][l][1])]
        cw = p["conv_w"][l]
        (hn,) = rowwise(nm("rms_mix"), f_rms, [Row(h, d)], [], [g_mix], [(d, d)], 1, tm, BF16)
        proj = unflat(matmul(flat(hn), p["w_in"][l], name=nm("mm_in")))
        qa = conv_fwd(proj, 0, cw[:, :d], True, nm("conv_q"))
        ka = conv_fwd(proj, cb(d), cw[:, d:2 * d], True, nm("conv_k"))
        va = conv_fwd(proj, cb(2 * d), cw[:, 2 * d:], False, nm("conv_v"))
        (gb,) = rowwise(nm("gb"), f_gb, [Row(proj, LANES, cb(lay.ab))], [], [alog, dtb], [(LANES, LANES)], 1, tmn)
        o0, st_a0, iv_a0 = gdn_fwd(qa, ka, va, gb, h_gdn, 0, None, nm("gdn_f0"))
        oa, st_a1, iv_a1 = gdn_fwd(qa, ka, va, gb, h_gdn, 1, o0, nm("gdn_f1"))
        (oan,) = rowwise(nm("gdn_out"), f_gdn_out, [Row(oa, LANES), Row(proj, LANES, cb(lay.z))], [], [gain_a],
                         [(d, LANES)], h_gdn, tmn, BF16)
        ya = unflat(matmul(flat(oan), p["w_up_a"][l], name=nm("mm_up_a")))
        (qkr,) = rowwise(nm("rope"), rope_f, [Row(proj, 2 * LANES, cb(lay.qb) // 2)], [Tab(cos), Tab(sin)], [],
                         [(2 * d, 2 * LANES)], 2 * h_ret, tmn)
        r0, st_b0 = ret_fwd(qkr, proj, cb(lay.vb), logits[0], h_ret, 0, None, nm("ret_f0"))
        ob, st_b1 = ret_fwd(qkr, proj, cb(lay.vb), logits[1], h_ret, 1, r0, nm("ret_f1"))
        (obn,) = rowwise(nm("ret_out"), f_ret_out, [Row(ob, 4 * LANES), Row(proj, 4 * LANES, cb(lay.gb) // 4)], [], [],
                         [(2 * d, 4 * LANES)], h_ret, tmr, BF16)
        yb = unflat(matmul(flat(obn), p["w_up_b"][l], name=nm("mm_up_b")))
        (mg,) = rowwise(nm("merge"), f_merge,
                        [Row(proj, LANES, cb(lay.ga)), Row(proj, LANES, cb(lay.gbt)), Row(ya, LANES), Row(yb, LANES)],
                        [], [], [(d, LANES)], cb(d), tmn, BF16)
        h_mid = unflat(matmul(flat(mg), p["w_out"][l], add=flat(h), name=nm("mm_out")))
        (hn2,) = rowwise(nm("rms_ffn"), f_rms, [Row(h_mid, d)], [], [g_ffn], [(d, d)], 1, tm, BF16)
        ffp = unflat(matmul(flat(hn2), p["w_ffn_in"][l], name=nm("mm_ffn_in")))
        (act,) = rowwise(nm("swiglu"), f_swiglu, [Row(ffp, 2 * ff)], [], [], [(ff, ff)], 1, tmw, BF16)
        h = unflat(matmul(flat(act), p["w_ffn_out"][l], add=flat(h_mid), name=nm("mm_ffn_out")))
        s.update(hn=hn, proj=proj, qa=qa, ka=ka, va=va, gb=gb, st_a=(st_a0, st_a1), iv_a=(iv_a0, iv_a1), oa=oa,
                 oan=oan, ya=ya, qkr=qkr, st_b=(st_b0, st_b1), ob=ob, obn=obn, yb=yb, mg=mg, h_mid=h_mid, hn2=hn2,
                 ffp=ffp, act=act, logits=logits, alog=alog, dtb=dtb, gain_a=gain_a, cw=cw, g_mix=g_mix, g_ffn=g_ffn)
        saved.append(s)

    dh, d_final, loss_row = loss_head(h, p["norm_final"][None], tgt, tm)

    grads = {k: [None] * depth for k in ("norm_mix", "w_in", "conv_w", "gdn_a_log", "gdn_dt_bias", "gdn_norm",
                                          "ret_decay_logit", "w_up_a", "w_up_b", "w_out", "norm_ffn", "w_ffn_in",
                                          "w_ffn_out")}
    for l in reversed(range(depth)):
        s = saved[l]
        nm = lambda k: f"l{l}_{k}"
        proj = s["proj"]
        dhf = flat(dh)
        grads["w_ffn_out"][l] = matmul(flat(s["act"]), dhf, ta=True, name=nm("mmg_ffn_out"))
        dact = unflat(matmul(dhf, p["w_ffn_out"][l], tb=True, name=nm("mmb_ffn_out")))
        (dffp,), _ = rowwise_vjp(nm("swiglu_b"), f_swiglu, [Row(s["ffp"], 2 * ff)], [], [], [dact], 1, tmw, narrow=(0,))
        grads["w_ffn_in"][l] = matmul(flat(s["hn2"]), flat(dffp), ta=True, name=nm("mmg_ffn_in"))
        dhn2 = unflat(matmul(flat(dffp), p["w_ffn_in"][l], tb=True, name=nm("mmb_ffn_in")))
        (dh_mid,), (dg_ffn,) = rowwise_vjp(nm("rms_ffn_b"), f_rms, [Row(s["h_mid"], d)], [], [s["g_ffn"]], [dhn2], 1, tmw,
                                           adds={0: dh})
        grads["norm_ffn"][l] = dg_ffn[0]
        dmf = flat(dh_mid)
        grads["w_out"][l] = matmul(flat(s["mg"]), dmf, ta=True, name=nm("mmg_out"))
        dmg = unflat(matmul(dmf, p["w_out"][l], tb=True, name=nm("mmb_out")))
        (dga, dgbt, dya, dyb), _ = rowwise_vjp(
            nm("merge_b"), f_merge,
            [Row(proj, LANES, cb(lay.ga)), Row(proj, LANES, cb(lay.gbt)), Row(s["ya"], LANES), Row(s["yb"], LANES)],
            [], [], [dmg], cb(d), tmn, narrow=(0, 1, 2, 3))
        grads["w_up_b"][l] = matmul(flat(s["obn"]), flat(dyb), ta=True, name=nm("mmg_up_b"))
        dobn = unflat(matmul(flat(dyb), p["w_up_b"][l], tb=True, name=nm("mmb_up_b")))
        (dob, dg_b), _ = rowwise_vjp(nm("ret_out_b"), f_ret_out,
                                     [Row(s["ob"], 4 * LANES), Row(proj, 4 * LANES, cb(lay.gb) // 4)], [], [], [dobn],
                                     h_ret, tmr, narrow=(1,))
        r1 = ret_bwd(s["qkr"], proj, cb(lay.vb), s["logits"][1], s["st_b"][1], dob, h_ret, 1, None, nm("ret_b1"))
        r0 = ret_bwd(s["qkr"], proj, cb(lay.vb), s["logits"][0], s["st_b"][0], dob, h_ret, 0, r1[:2], nm("ret_b0"),
                     dv_dtype=BF16)
        (dqk,) = rowwise(nm("rope_b"), rope_b, [Row(r0[0], 2 * LANES)], [Tab(cos), Tab(sin)], [],
                         [(2 * d, 2 * LANES)], 2 * h_ret, tmn, BF16)
        dv_b = r0[1]
        grads["ret_decay_logit"][l] = jnp.stack([r0[2][0, :h_ret], r1[2][0, :h_ret]])
        grads["w_up_a"][l] = matmul(flat(s["oan"]), flat(dya), ta=True, name=nm("mmg_up_a"))
        doan = unflat(matmul(flat(dya), p["w_up_a"][l], tb=True, name=nm("mmb_up_a")))
        (doa, dz), (dgain_a,) = rowwise_vjp(nm("gdn_out_b"), f_gdn_out,
                                            [Row(s["oa"], LANES), Row(proj, LANES, cb(lay.z))], [], [s["gain_a"]],
                                            [doan], h_gdn, tmn, narrow=(1,))
        grads["gdn_norm"][l] = dgain_a[0]
        a1 = gdn_bwd(s["qa"], s["ka"], s["va"], s["gb"], s["st_a"][1], s["iv_a"][1], doa, h_gdn, 1, None, nm("gdn_b1"))
        a0 = gdn_bwd(s["qa"], s["ka"], s["va"], s["gb"], s["st_a"][0], s["iv_a"][0], doa, h_gdn, 0, a1, nm("gdn_b0"))
        (dab,), (dalog, ddtb) = rowwise_vjp(nm("gb_b"), f_gb, [Row(proj, LANES, cb(lay.ab))], [], [s["alog"], s["dtb"]],
                                            [a0[3]], 1, tmn, narrow=(0,))
        grads["gdn_a_log"][l] = dalog[0, :2 * h_gdn].reshape(2, h_gdn)
        grads["gdn_dt_bias"][l] = ddtb[0, :2 * h_gdn].reshape(2, h_gdn)
        cw = s["cw"]
        dxq, dwq = conv_bwd(proj, 0, cw[:, :d], a0[0], True, nm("conv_q_b"))
        dxk, dwk = conv_bwd(proj, cb(d), cw[:, d:2 * d], a0[1], True, nm("conv_k_b"))
        dxv, dwv = conv_bwd(proj, cb(2 * d), cw[:, 2 * d:], a0[2], False, nm("conv_v_b"))
        grads["conv_w"][l] = jnp.concatenate([dwq, dwk, dwv], axis=1)
        dproj = jnp.concatenate([dxq, dxk, dxv, dz, dqk, dv_b, dg_b, dga, dgbt, dab,
                                 jnp.zeros((bsz, lp, lay.np - lay.used), BF16)], axis=-1)
        grads["w_in"][l] = matmul(flat(s["hn"]), flat(dproj), ta=True, name=nm("mmg_in"))
        dhn = unflat(matmul(flat(dproj), p["w_in"][l], tb=True, name=nm("mmb_in")))
        (dh,), (dg_mix,) = rowwise_vjp(nm("rms_mix_b"), f_rms, [Row(s["h_in"], d)], [], [s["g_mix"]], [dhn], 1, tmw,
                                       adds={0: dh_mid})
        grads["norm_mix"][l] = dg_mix[0]

    out = {k: jnp.stack(v) for k, v in grads.items()}
    out["norm_final"] = d_final[0]
    grad_x = dh[:, CHUNK:]
    grad_meta = jnp.sum(dh[:, N_PAD:CHUNK], axis=0)
    return loss_row, grad_x, grad_meta, out


BIG = ("w_in", "w_up_a", "w_up_b", "w_out", "w_ffn_in", "w_ffn_out")
BIG_AXES = (2, 1, 1, 1, 2, 1)
SMALL_SHARDED = ("meta_tokens", "conv_w")
SMALL_AXES = (1, 2)
REPLICATED = ("norm_mix", "gdn_a_log", "gdn_dt_bias", "gdn_norm", "ret_decay_logit", "norm_ffn", "norm_final")
WEIGHTS = ("meta_tokens", "norm_mix", "w_in", "conv_w", "gdn_a_log", "gdn_dt_bias", "gdn_norm", "ret_decay_logit",
           "w_up_a", "w_up_b", "w_out", "norm_ffn", "w_ffn_in", "w_ffn_out", "norm_final")


def kernel(x, meta_tokens, norm_mix, w_in, conv_w, gdn_a_log, gdn_dt_bias, gdn_norm, ret_decay_logit, w_up_a, w_up_b, w_out, norm_ffn, w_ffn_in, w_ffn_out, norm_final, loss_target, m_meta_tokens, m_norm_mix, m_w_in, m_conv_w, m_gdn_a_log, m_gdn_dt_bias, m_gdn_norm, m_ret_decay_logit, m_w_up_a, m_w_up_b, m_w_out, m_norm_ffn, m_w_ffn_in, m_w_ffn_out, m_norm_final, v_meta_tokens, v_norm_mix, v_w_in, v_conv_w, v_gdn_a_log, v_gdn_dt_bias, v_gdn_norm, v_ret_decay_logit, v_w_up_a, v_w_up_b, v_w_out, v_norm_ffn, v_w_ffn_in, v_w_ffn_out, v_norm_final):
    w = dict(meta_tokens=meta_tokens, norm_mix=norm_mix, w_in=w_in, conv_w=conv_w, gdn_a_log=gdn_a_log,
             gdn_dt_bias=gdn_dt_bias, gdn_norm=gdn_norm, ret_decay_logit=ret_decay_logit, w_up_a=w_up_a,
             w_up_b=w_up_b, w_out=w_out, norm_ffn=norm_ffn, w_ffn_in=w_ffn_in, w_ffn_out=w_ffn_out,
             norm_final=norm_final)
    m = dict(meta_tokens=m_meta_tokens, norm_mix=m_norm_mix, w_in=m_w_in, conv_w=m_conv_w, gdn_a_log=m_gdn_a_log,
             gdn_dt_bias=m_gdn_dt_bias, gdn_norm=m_gdn_norm, ret_decay_logit=m_ret_decay_logit, w_up_a=m_w_up_a,
             w_up_b=m_w_up_b, w_out=m_w_out, norm_ffn=m_norm_ffn, w_ffn_in=m_w_ffn_in, w_ffn_out=m_w_ffn_out,
             norm_final=m_norm_final)
    v = dict(meta_tokens=v_meta_tokens, norm_mix=v_norm_mix, w_in=v_w_in, conv_w=v_conv_w, gdn_a_log=v_gdn_a_log,
             gdn_dt_bias=v_gdn_dt_bias, gdn_norm=v_gdn_norm, ret_decay_logit=v_ret_decay_logit, w_up_a=v_w_up_a,
             w_up_b=v_w_up_b, w_out=v_w_out, norm_ffn=v_norm_ffn, w_ffn_in=v_w_ffn_in, w_ffn_out=v_w_ffn_out,
             norm_final=v_norm_final)
    d = x.shape[-1]
    lay = Layout(d)
    slot_axes = [1 if ax == 1 else 0 for ax in BIG_AXES]

    gathered = all_gather_hbm([w[k].astype(BF16) for k in BIG], slot_axes, "gather_weights")
    full = {}
    for k, gk, sa in zip(BIG, gathered, slot_axes):
        if sa == 1:
            full[k] = gk.reshape(gk.shape[0], -1, gk.shape[3])
        else:
            full[k] = jnp.concatenate([gk[s] for s in range(N_DEV)], axis=2)
    small_shapes = [w[k].shape for k in SMALL_SHARDED]
    gathered_s = all_gather_small(_pack([w[k] for k in SMALL_SHARDED], 8), "gather_small")
    full.update(zip(SMALL_SHARDED, _gathered_full(gathered_s, small_shapes, SMALL_AXES)))
    p = {k: w[k] for k in REPLICATED}
    p.update({k: full[k] for k in BIG + ("conv_w",)})
    p["w_in"] = jnp.stack([lay.relayout_w_in(full["w_in"][l]) for l in range(full["w_in"].shape[0])])

    loss_row, grad_x, grad_meta, g = local_step(x, loss_target, full["meta_tokens"], p)
    g["meta_tokens"] = grad_meta
    g["w_in"] = jnp.stack([lay.unlayout_w_in(g["w_in"][l]) for l in range(g["w_in"].shape[0])])

    pieces = []
    for k, sa in zip(BIG, slot_axes):
        nl, rows, cols = g[k].shape
        if sa == 1:
            pieces.append(g[k].astype(BF16).reshape(nl, N_DEV, rows // N_DEV, cols))
        else:
            cs = cols // N_DEV
            pieces.append(jnp.stack([g[k][:, :, s * cs:(s + 1) * cs] for s in range(N_DEV)]).astype(BF16))
    small_pieces = _pieces_by_dest([g[k] for k in SMALL_SHARDED], SMALL_AXES, 8)
    received = all_to_all_hbm(pieces + [small_pieces], slot_axes + [0], "exchange_grads")
    outs = {kind: {} for kind in ("g", "d", "m", "v")}
    for k, sa, got in zip(BIG, slot_axes, received):
        for kind, r in zip(("g", "d", "m", "v"), adamw_blocks(w[k], m[k], v[k], got, sa, "adamw_" + k)):
            outs[kind][k] = r
    res = adamw(_pack([w[k] for k in SMALL_SHARDED], 8), _pack([m[k] for k in SMALL_SHARDED], 8),
                _pack([v[k] for k in SMALL_SHARDED], 8), received[-1], "adamw_small")
    for kind, r in zip(("g", "d", "m", "v"), res):
        outs[kind].update(zip(SMALL_SHARDED, _unpack(r, small_shapes)))

    rep_shapes = [w[k].shape for k in REPLICATED]
    part = _pack([g[k] for k in REPLICATED] + [loss_row[0, :1]], 8)
    parts = all_gather_small(part, "gather_replicated")
    pad1 = lambda a: _pack([a[k] for k in REPLICATED] + [jnp.zeros((1,), F32)], 8)
    res_r = adamw(pad1(w), pad1(m), pad1(v), parts, "adamw_replicated")
    for kind, r in zip(("g", "d", "m", "v"), res_r):
        outs[kind].update(zip(REPLICATED, _unpack(r, rep_shapes)))
    loss = res_r[0].reshape(-1)[sum(math.prod(s) for s in rep_shapes)]

    return (loss, grad_x, *[outs["g"][k] for k in WEIGHTS], *[outs["d"][k] for k in WEIGHTS],
            *[outs["m"][k] for k in WEIGHTS], *[outs["v"][k] for k in WEIGHTS])
```

```python
import functools
import math

import jax
import jax.numpy as jnp
from jax import lax
from jax.experimental import pallas as pl
from jax.experimental.pallas import tpu as pltpu

F32 = jnp.float32
BF16 = jnp.bfloat16
HIGHEST = lax.Precision.HIGHEST

LANES = 128
CHUNK = 64
N_META = 16
N_PAD = CHUNK - N_META
CONV_K = 5
EPS = 1e-6
ROPE_BASE = 10000.0
N_DEV = 8
VMEM_LIMIT = 56 * 1024 * 1024
MATMUL_VMEM = 40 * 1024 * 1024

ADAM_LR, ADAM_B1, ADAM_B2, ADAM_EPS, ADAM_WD, ADAM_STEP = 0.001, 0.9, 0.999, 1e-08, 0.01, 10


def _tile(n, cap, mult):
    if n <= cap:
        return n
    best = None
    for t in range(mult, cap + 1, mult):
        if n % t == 0:
            best = t
    assert best is not None, (n, cap, mult)
    return best


def _params(sem):
    return pltpu.CompilerParams(dimension_semantics=sem, vmem_limit_bytes=VMEM_LIMIT)


def _raw_dot(a, b, ca, cb, exact):
    dims = (((ca,), (cb,)), ((), ()))
    a_hi, b_hi = a.astype(BF16), b.astype(BF16)
    out = lax.dot_general(a_hi, b_hi, dims, preferred_element_type=F32)
    if exact:
        a_lo = (a - a_hi.astype(F32)).astype(BF16)
        b_lo = (b - b_hi.astype(F32)).astype(BF16)
        out = out + lax.dot_general(a_hi, b_lo, dims, preferred_element_type=F32)
        out = out + lax.dot_general(a_lo, b_hi, dims, preferred_element_type=F32)
    return out


@functools.partial(jax.custom_vjp, nondiff_argnums=(2, 3, 4))
def _dot(a, b, ca, cb, exact):
    return _raw_dot(a, b, ca, cb, exact)


def _dot_fwd(a, b, ca, cb, exact):
    return _raw_dot(a, b, ca, cb, exact), (a, b)


def _dot_bwd(ca, cb, exact, res, g):
    a, b = res
    if ca == 1:
        da = _raw_dot(g, b, 1, 1 if cb == 0 else 0, exact)
    else:
        da = _raw_dot(b, g, 1 if cb == 0 else 0, 1, exact)
    if cb == 0:
        db = _raw_dot(a, g, 0 if ca == 1 else 1, 0, exact)
    else:
        db = _raw_dot(g, a, 0, 0 if ca == 1 else 1, exact)
    return da, db


_dot.defvjp(_dot_fwd, _dot_bwd)


@functools.partial(jax.custom_vjp, nondiff_argnums=(1, 2))
def _split(x, n, axis):
    return lax.slice_in_dim(x, 0, n, axis=axis), lax.slice_in_dim(x, n, x.shape[axis], axis=axis)


_split.defvjp(lambda x, n, axis: (_split(x, n, axis), None),
              lambda n, axis, _, g: (jnp.concatenate([g[0], g[1]], axis=axis),))


def _bdot(a, b, ca=1, cb=0):
    return _dot(a, b, ca, cb, False)


def _xdot(a, b, ca=1, cb=0):
    return _dot(a, b, ca, cb, True)


def matmul(a, b, *, ta=False, tb=False, add=None, name):
    m, k = (a.shape[1], a.shape[0]) if ta else a.shape
    k2, n = (b.shape[1], b.shape[0]) if tb else b.shape
    assert k == k2, (a.shape, b.shape, ta, tb)
    has_add = add is not None
    tm_cap, tn_cap = 2080, 1408
    while True:
        tm = _tile(m, tm_cap, 128 if ta else 16)
        tn = _tile(n, tn_cap, 128)
        tk = _tile(k, 1664, 128 if (not ta or tb) else 16)
        nk = k // tk
        need = 2 * (tm * tk * a.dtype.itemsize + tk * tn * b.dtype.itemsize + tm * tn * 4 * (2 if has_add else 1))
        need += tm * tn * 4 if nk > 1 else 0
        if need <= MATMUL_VMEM or (tm_cap <= 256 and tn_cap <= 256):
            break
        if tm_cap >= tn_cap:
            tm_cap //= 2
        else:
            tn_cap //= 2
    ca, cb = (0 if ta else 1), (1 if tb else 0)
    a_spec = pl.BlockSpec((tk, tm), lambda i, j, kk: (kk, i)) if ta else pl.BlockSpec((tm, tk), lambda i, j, kk: (i, kk))
    b_spec = pl.BlockSpec((tn, tk), lambda i, j, kk: (j, kk)) if tb else pl.BlockSpec((tk, tn), lambda i, j, kk: (kk, j))
    o_spec = pl.BlockSpec((tm, tn), lambda i, j, kk: (i, j))

    def body(*refs):
        a_ref, b_ref = refs[0], refs[1]
        add_ref = refs[2] if has_add else None
        o_ref = refs[3] if has_add else refs[2]
        def part():
            return _raw_dot(a_ref[...], b_ref[...], ca, cb, False)

        if nk == 1:
            o_ref[...] = part() + add_ref[...] if has_add else part()
            return
        acc_ref = refs[-1]
        kk = pl.program_id(2)

        @pl.when(kk == 0)
        def _():
            acc_ref[...] = part()

        @pl.when(jnp.logical_and(kk > 0, kk < nk - 1))
        def _():
            acc_ref[...] += part()

        @pl.when(kk == nk - 1)
        def _():
            o_ref[...] = acc_ref[...] + part() + add_ref[...] if has_add else acc_ref[...] + part()

    ins = [a, b] + ([add] if has_add else [])
    in_specs = [a_spec, b_spec] + ([o_spec] if has_add else [])
    return pl.pallas_call(
        body, name=name, grid=(m // tm, n // tn, nk), in_specs=in_specs, out_specs=o_spec,
        out_shape=jax.ShapeDtypeStruct((m, n), F32),
        scratch_shapes=[pltpu.VMEM((tm, tn), F32)] if nk > 1 else [],
        compiler_params=_params(("parallel", "parallel", "arbitrary")),
    )(*ins)


class Row:
    def __init__(self, arr, bc, off=0, per_head=True, diff=True):
        self.arr, self.bc, self.off, self.per_head, self.diff = arr, bc, off, per_head, diff


class Tab:
    def __init__(self, arr):
        self.arr = arr


def _row_specs(rows, tabs, pars, tm):
    specs = []
    for r in rows:
        specs.append(pl.BlockSpec((1, tm, r.bc), functools.partial(
            lambda b, i, h, off, ph: (b, i, off + (h if ph else 0)), off=r.off, ph=r.per_head)))
    for t in tabs:
        specs.append(pl.BlockSpec((tm, t.arr.shape[1]), lambda b, i, h: (i, 0)))
    for p in pars:
        specs.append(pl.BlockSpec(p.shape, lambda b, i, h: (0, 0)))
    return specs


def rowwise(name, fn, rows, tabs, pars, outs, nh, tm, out_dtype=F32):
    bsz, lp = rows[0].arr.shape[:2]
    nr, nt, npar = len(rows), len(tabs), len(pars)

    def body(*refs):
        t0 = pl.program_id(1) * tm
        ins = [refs[k][0] for k in range(nr)] + [refs[nr + k][...] for k in range(nt + npar)]
        res = fn(t0, *ins)
        for o_ref, o in zip(refs[nr + nt + npar:], res):
            o_ref[0] = o.astype(o_ref.dtype)

    return pl.pallas_call(
        body, name=name, grid=(bsz, lp // tm, nh),
        in_specs=_row_specs(rows, tabs, pars, tm),
        out_specs=[pl.BlockSpec((1, tm, bc), lambda b, i, h: (b, i, h)) for _, bc in outs],
        out_shape=[jax.ShapeDtypeStruct((bsz, lp, c), out_dtype) for c, _ in outs],
        compiler_params=_params(("parallel", "parallel", "parallel")),
    )(*[r.arr for r in rows], *[t.arr for t in tabs], *pars)


def rowwise_vjp(name, fn, rows, tabs, pars, couts, nh, tm, adds=None, narrow=()):
    bsz, lp = rows[0].arr.shape[:2]
    nr, nt, npar, nco = len(rows), len(tabs), len(pars), len(couts)
    adds = adds or {}
    add_keys = sorted(adds)
    diff_idx = [k for k, r in enumerate(rows) if r.diff]
    for k in diff_idx:
        assert rows[k].per_head or nh == 1

    def body(*refs):
        b, i, h = pl.program_id(0), pl.program_id(1), pl.program_id(2)
        t0 = i * tm
        pos = 0
        row_v = [refs[k][0] for k in range(nr)]
        pos += nr
        tab_v = [refs[pos + k][...] for k in range(nt)]
        pos += nt
        par_v = [refs[pos + k][...] for k in range(npar)]
        pos += npar
        co_v = [refs[pos + k][0] for k in range(nco)]
        pos += nco
        add_v = {key: refs[pos + k][0] for k, key in enumerate(add_keys)}
        pos += len(add_keys)
        drow_refs = refs[pos:pos + len(diff_idx)]
        dpar_refs = refs[pos + len(diff_idx):]

        def f(dvals, pvals):
            full = list(row_v)
            for k, v in zip(diff_idx, dvals):
                full[k] = v
            return tuple(fn(t0, *full, *tab_v, *pvals))

        _, pull = jax.vjp(f, [row_v[k] for k in diff_idx], par_v)
        d_rows, d_pars = pull(tuple(co_v))
        for ref, k, d in zip(drow_refs, diff_idx, d_rows):
            ref[0] = (d + add_v[k] if k in add_v else d).astype(ref.dtype)
        first = jnp.logical_and(jnp.logical_and(b == 0, i == 0), h == 0)
        for ref, d in zip(dpar_refs, d_pars):
            @pl.when(first)
            def _(ref=ref, d=d):
                ref[...] = d

            @pl.when(jnp.logical_not(first))
            def _(ref=ref, d=d):
                ref[...] += d

    out_block = lambda bc: pl.BlockSpec((1, tm, bc), lambda b, i, h: (b, i, h))
    in_specs = _row_specs(rows, tabs, pars, tm)
    in_specs += [out_block(c.shape[2] // nh) for c in couts]
    in_specs += [out_block(rows[k].bc) for k in add_keys]
    out_specs = [out_block(rows[k].bc) for k in diff_idx]
    out_specs += [pl.BlockSpec(p.shape, lambda b, i, h: (0, 0)) for p in pars]
    out_shape = [jax.ShapeDtypeStruct((bsz, lp, nh * rows[k].bc), BF16 if k in narrow else F32) for k in diff_idx]
    out_shape += [jax.ShapeDtypeStruct(p.shape, F32) for p in pars]
    res = pl.pallas_call(
        body, name=name, grid=(bsz, lp // tm, nh), in_specs=in_specs, out_specs=out_specs, out_shape=out_shape,
        compiler_params=_params(("arbitrary", "arbitrary", "arbitrary")),
    )(*[r.arr for r in rows], *[t.arr for t in tabs], *pars, *couts, *[adds[k] for k in add_keys])
    return res[:len(diff_idx)], res[len(diff_idx):]


def _real_rows(t0, tm):
    return (t0 + lax.broadcasted_iota(jnp.int32, (tm, 1), 0)) >= N_PAD


def f_rms(t0, x, gain):
    return (x * lax.rsqrt(jnp.mean(x * x, axis=-1, keepdims=True) + EPS) * gain,)


def make_f_gb(heads):
    def f_gb(t0, ab, alog, dtb):
        lane = lax.broadcasted_iota(jnp.int32, ab.shape, 1)
        g = -jnp.exp(alog) * jax.nn.softplus(ab + dtb)
        beta = jax.nn.sigmoid(ab)
        out = jnp.where(lane < 2 * heads, g, jnp.where(lane < 4 * heads, beta, 0.0))
        return (jnp.where(_real_rows(t0, ab.shape[0]), out, 0.0),)
    return f_gb


def f_gdn_out(t0, o, z, gain):
    on = o * lax.rsqrt(jnp.mean(o * o, axis=-1, keepdims=True) + EPS)
    return (on * gain * jax.nn.silu(z),)


def f_ret_out(t0, o, g):
    on = o * lax.rsqrt(jnp.mean(o * o, axis=-1, keepdims=True) + EPS)
    return (on * jax.nn.silu(g),)


def f_merge(t0, ga, gb, ya, yb):
    return (jax.nn.sigmoid(ga) * ya + jax.nn.sigmoid(gb) * yb,)


def f_swiglu(t0, x):
    gate, up = _split(x, x.shape[1] // 2, 1)
    return (jax.nn.silu(gate) * up,)


def make_f_rope(sign):
    def f_rope(t0, x, cos, sin):
        half = x.shape[1] // 2
        x1, x2 = x[:, :half], x[:, half:]
        s = sin * sign
        return (jnp.concatenate([x1 * cos - x2 * s, x1 * s + x2 * cos], axis=1),)
    return f_rope


def loss_head(h, gain, target, tm):
    bsz, lp, d = h.shape

    def body(h_ref, g_ref, t_ref, dh_ref, dg_ref, loss_ref):
        b, i = pl.program_id(0), pl.program_id(1)
        rows = (i * tm + lax.broadcasted_iota(jnp.int32, (tm, 1), 0)) >= CHUNK
        tgt = t_ref[0]

        def f(x, gain_v):
            y = f_rms(0, x, gain_v)[0]
            err = jnp.where(rows, y - tgt, 0.0)
            return 0.5 * jnp.sum(jnp.mean(err * err, axis=-1, keepdims=True), keepdims=True)

        val, pull = jax.vjp(f, h_ref[0], g_ref[...])
        dh, dg = pull(jnp.ones((1, 1), F32))
        dh_ref[0] = dh
        first = jnp.logical_and(b == 0, i == 0)
        val_row = jnp.broadcast_to(val, (1, LANES))

        @pl.when(first)
        def _():
            dg_ref[...] = dg
            loss_ref[...] = val_row

        @pl.when(jnp.logical_not(first))
        def _():
            dg_ref[...] += dg
            loss_ref[...] += val_row

    blk = pl.BlockSpec((1, tm, d), lambda b, i: (b, i, 0))
    return pl.pallas_call(
        body, name="loss_head", grid=(bsz, lp // tm),
        in_specs=[blk, pl.BlockSpec((1, d), lambda b, i: (0, 0)), blk],
        out_specs=[blk, pl.BlockSpec((1, d), lambda b, i: (0, 0)), pl.BlockSpec((1, LANES), lambda b, i: (0, 0))],
        out_shape=[jax.ShapeDtypeStruct((bsz, lp, d), F32), jax.ShapeDtypeStruct((1, d), F32),
                   jax.ShapeDtypeStruct((1, LANES), F32)],
        compiler_params=_params(("arbitrary", "arbitrary")),
    )(h, gain, target)


def _conv_pre(x, w_ref):
    lp = x.shape[0]
    acc = w_ref[2:3, :] * x
    for k in (0, 1, 3, 4):
        acc = acc + w_ref[k:k + 1, :] * pltpu.roll(x, (2 - k) % lp, 0)
    return acc


def conv_fwd(proj, off, w, l2, name):
    bsz, lp, _ = proj.shape
    d = w.shape[1]

    def body(x_ref, w_ref, o_ref):
        x = x_ref[0]
        s = jnp.where(_real_rows(0, lp), jax.nn.silu(_conv_pre(x, w_ref)), 0.0)
        if l2:
            s = s * lax.rsqrt(jnp.sum(s * s, axis=-1, keepdims=True) + EPS)
        o_ref[0] = s

    return pl.pallas_call(
        body, name=name, grid=(d // LANES, bsz),
        in_specs=[pl.BlockSpec((1, lp, LANES), lambda j, b: (b, 0, off + j)),
                  pl.BlockSpec((CONV_K, LANES), lambda j, b: (0, j))],
        out_specs=pl.BlockSpec((1, lp, LANES), lambda j, b: (b, 0, j)),
        out_shape=jax.ShapeDtypeStruct((bsz, lp, d), F32),
        compiler_params=_params(("parallel", "parallel")),
    )(proj, w)


def conv_bwd(proj, off, w, dy, l2, name):
    bsz, lp, _ = proj.shape
    d = w.shape[1]

    def body(x_ref, w_ref, dy_ref, dx_ref, dw_ref):
        b = pl.program_id(1)
        x, g = x_ref[0], dy_ref[0]
        real = _real_rows(0, lp)
        c = _conv_pre(x, w_ref)
        sg = jax.nn.sigmoid(c)
        s = jnp.where(real, c * sg, 0.0)
        if l2:
            r = lax.rsqrt(jnp.sum(s * s, axis=-1, keepdims=True) + EPS)
            g = r * g - s * (r * r * r) * jnp.sum(g * s, axis=-1, keepdims=True)
        dc = jnp.where(real, g * (sg * (1.0 + c * (1.0 - sg))), 0.0)
        dx = w_ref[2:3, :] * dc
        for k in (0, 1, 3, 4):
            dx = dx + w_ref[k:k + 1, :] * pltpu.roll(dc, (k - 2) % lp, 0)
        dx_ref[0] = jnp.where(real, dx, 0.0).astype(dx_ref.dtype)
        tap_row = lax.broadcasted_iota(jnp.int32, (CONV_K, LANES), 0)
        dw = jnp.zeros((CONV_K, LANES), F32)
        for k in range(CONV_K):
            xs = x if k == 2 else pltpu.roll(x, (2 - k) % lp, 0)
            dw = dw + jnp.where(tap_row == k, jnp.sum(dc * xs, axis=0, keepdims=True), 0.0)

        @pl.when(b == 0)
        def _():
            dw_ref[...] = dw

        @pl.when(b > 0)
        def _():
            dw_ref[...] += dw

    blk = pl.BlockSpec((1, lp, LANES), lambda j, b: (b, 0, j))
    return pl.pallas_call(
        body, name=name, grid=(d // LANES, bsz),
        in_specs=[pl.BlockSpec((1, lp, LANES), lambda j, b: (b, 0, off + j)),
                  pl.BlockSpec((CONV_K, LANES), lambda j, b: (0, j)), blk],
        out_specs=[blk, pl.BlockSpec((CONV_K, LANES), lambda j, b: (0, j))],
        out_shape=[jax.ShapeDtypeStruct((bsz, lp, d), BF16), jax.ShapeDtypeStruct((CONV_K, d), F32)],
        compiler_params=_params(("parallel", "arbitrary")),
    )(proj, w, dy)


def _tri_masks(rev):
    ii = lax.broadcasted_iota(jnp.int32, (CHUNK, CHUNK), 0)
    jj = lax.broadcasted_iota(jnp.int32, (CHUNK, CHUNK), 1)
    return ((ii <= jj), (ii < jj)) if rev else ((ii >= jj), (ii > jj))


def _lane_pick(block, lane):
    sel = lax.broadcasted_iota(jnp.int32, block.shape, 1) == lane
    return jnp.sum(jnp.where(sel, block, 0.0), axis=1, keepdims=True)


def _cumsum_impl(x, rev):
    n = x.shape[0]
    row = lax.broadcasted_iota(jnp.int32, x.shape, 0)
    step = 1
    while step < n:
        if rev:
            x = x + jnp.where(row < n - step, pltpu.roll(x, n - step, 0), 0.0)
        else:
            x = x + jnp.where(row >= step, pltpu.roll(x, step, 0), 0.0)
        step *= 2
    return x


@functools.partial(jax.custom_vjp, nondiff_argnums=(1,))
def _cumsum_rows(x, rev):
    return _cumsum_impl(x, rev)


_cumsum_rows.defvjp(lambda x, rev: (_cumsum_impl(x, rev), None),
                    lambda rev, _, g: (_cumsum_impl(g, not rev),))


def _unit_inv_impl(m):
    n = m.shape[0]
    eye = (lax.broadcasted_iota(jnp.int32, (n, n), 0) == lax.broadcasted_iota(jnp.int32, (n, n), 1)).astype(F32)
    p = -m
    inv = eye + p
    step = 2
    while step < n:
        p = _raw_dot(p, p, 1, 0, True)
        inv = inv + _raw_dot(inv, p, 1, 0, True)
        step *= 2
    return inv


@jax.custom_vjp
def _unit_inv(m):
    return _unit_inv_impl(m)


def _unit_inv_fwd(m):
    inv = _unit_inv_impl(m)
    return inv, inv


def _unit_inv_bwd(inv, g):
    return (-_raw_dot(_raw_dot(inv, g, 0, 0, True), inv, 1, 1, True),)


_unit_inv.defvjp(_unit_inv_fwd, _unit_inv_bwd)


GROUP = 2


def _unit_inv_all_impl(ms):
    n = ms[0].shape[0]
    width = GROUP * n
    dims = (((1,), (0,)), ((), ()))
    lane = lax.broadcasted_iota(jnp.int32, (n, width), 1)
    row = lax.broadcasted_iota(jnp.int32, (n, width), 0)
    blocks = [jnp.logical_and(lane >= t * n, lane < (t + 1) * n) for t in range(GROUP)]

    def halves(x):
        hi = x.astype(BF16)
        return hi, (x - hi.astype(F32)).astype(BF16)

    def xdot(a, b):
        diag = jnp.concatenate([jnp.where(blk, b, 0.0) for blk in blocks], axis=0)
        a_hi, a_lo = halves(a)
        d_hi, d_lo = halves(diag)
        out = lax.dot_general(a_hi, d_hi, dims, preferred_element_type=F32)
        out = out + lax.dot_general(a_hi, d_lo, dims, preferred_element_type=F32)
        return out + lax.dot_general(a_lo, d_hi, dims, preferred_element_type=F32)

    eye = (row == lane % n).astype(F32)
    groups = range(len(ms) // GROUP)
    p = [-jnp.concatenate(ms[t * GROUP:(t + 1) * GROUP], axis=1) for t in groups]
    inv = [eye + p[t] for t in groups]
    step = 2
    while step < n:
        p = [xdot(p[t], p[t]) for t in groups]
        inv = [inv[t] + xdot(inv[t], p[t]) for t in groups]
        step *= 2
    return [inv[t][:, u * n:(u + 1) * n] for t in groups for u in range(GROUP)]


@jax.custom_vjp
def _unit_inv_all(ms):
    return _unit_inv_all_impl(ms)


def _unit_inv_all_fwd(ms):
    out = _unit_inv_all_impl(ms)
    return out, out


_unit_inv_all.defvjp(_unit_inv_all_fwd, lambda invs, g: ([_unit_inv_bwd(i, gi)[0] for i, gi in zip(invs, g)],))


@jax.custom_vjp
def _unit_inv_known(m, inv):
    return inv


_unit_inv_known.defvjp(lambda m, inv: (inv, inv),
                       lambda inv, g: (_unit_inv_bwd(inv, g)[0], jnp.zeros_like(inv)))


def _gdn_chunk(qs, ks, vs, gb, ss, g_lanes, b_lanes, rev, invs=None, want_inv=False):
    nh = len(qs)
    dk = qs[0].shape[1]
    incl, strict = _tri_masks(rev)
    hs = range(nh)
    g = [_lane_pick(gb, l) for l in g_lanes]
    beta = [_lane_pick(gb, l) for l in b_lanes]
    qs = [q * (dk ** -0.5) for q in qs]
    gc_sq = [_cumsum_rows(jnp.broadcast_to(g[h], (CHUNK, CHUNK)), rev) for h in hs]
    gc = [_cumsum_rows(jnp.broadcast_to(g[h], (CHUNK, dk)), rev) for h in hs]
    g_last = [jnp.sum(g[h], axis=0, keepdims=True) for h in hs]
    decay = [jnp.where(incl, jnp.exp(jnp.where(incl, gc_sq[h] - gc_sq[h].T, 0.0)), 0.0) for h in hs]
    kb = [ks[h] * beta[h] for h in hs]
    kk = [_split(_bdot(jnp.concatenate([kb[h], qs[h]], axis=0), ks[h], 1, 1), CHUNK, 0) for h in hs]
    m = [jnp.where(strict, kk[h][0] * decay[h], 0.0) for h in hs]
    qk = [kk[h][1] * decay[h] for h in hs]
    if invs is not None:
        inv = [_unit_inv_known(m[h], invs[h]) for h in hs]
    elif nh % GROUP == 0:
        inv = _unit_inv_all(m)
    else:
        inv = [_unit_inv(m[h]) for h in hs]
    e_gc = [jnp.exp(gc[h]) for h in hs]
    uw = [_split(_bdot(inv[h], jnp.concatenate([vs[h] * beta[h], kb[h] * e_gc[h]], axis=1)), vs[h].shape[1], 1)
          for h in hs]
    u = [uw[h][0] for h in hs]
    w = [uw[h][1] for h in hs]
    q_dec = [qs[h] * e_gc[h] for h in hs]
    k_dec = [ks[h] * jnp.exp(g_last[h] - gc[h]) for h in hs]
    ws = [_split(_bdot(jnp.concatenate([w[h], q_dec[h]], axis=0), ss[h]), CHUNK, 0) for h in hs]
    v_new = [u[h] - ws[h][0] for h in hs]
    o = [ws[h][1] + _bdot(qk[h], v_new[h]) for h in hs]
    s_new = [ss[h] * jnp.exp(g_last[h]) + _bdot(k_dec[h], v_new[h], 0, 0) for h in hs]
    return (o, s_new, inv) if want_inv else (o, s_new)


def gdn_fwd(q, k, v, gb, heads, direction, o_prev, name):
    bsz, lp, d = q.shape
    nc = lp // CHUNK
    rev = direction == 1
    cm = (lambda n: nc - 1 - n) if rev else (lambda n: n)
    has_prev = o_prev is not None

    def body(*refs):
        q_ref, k_ref, v_ref, gb_ref = refs[:4]
        prev_ref = refs[4] if has_prev else None
        o_ref, st_ref, inv_ref, s_ref = refs[-4:]
        n = pl.program_id(1)

        @pl.when(n == 0)
        def _():
            s_ref[...] = jnp.zeros_like(s_ref)

        sls = [slice(h * LANES, (h + 1) * LANES) for h in range(heads)]
        ss = [s_ref[h] for h in range(heads)]
        for h in range(heads):
            st_ref[0, 0, h] = ss[h]
        os_, s_new, inv = _gdn_chunk([q_ref[0, :, sl] for sl in sls], [k_ref[0, :, sl] for sl in sls],
                                     [v_ref[0, :, sl] for sl in sls], gb_ref[0], ss,
                                     [direction * heads + h for h in range(heads)],
                                     [2 * heads + direction * heads + h for h in range(heads)], rev, want_inv=True)
        for h, sl in enumerate(sls):
            s_ref[h] = s_new[h]
            inv_ref[0, 0, h] = inv[h]
            o_ref[0, :, sl] = os_[h] + prev_ref[0, :, sl] if has_prev else os_[h]

    blk = pl.BlockSpec((1, CHUNK, d), lambda b, n: (b, cm(n), 0))
    gblk = pl.BlockSpec((1, CHUNK, LANES), lambda b, n: (b, cm(n), 0))
    st_blk = pl.BlockSpec((1, 1, heads, LANES, LANES), lambda b, n: (b, cm(n), 0, 0, 0))
    inv_blk = pl.BlockSpec((1, 1, heads, CHUNK, CHUNK), lambda b, n: (b, cm(n), 0, 0, 0))
    return pl.pallas_call(
        body, name=name, grid=(bsz, nc),
        in_specs=[blk, blk, blk, gblk] + ([blk] if has_prev else []),
        out_specs=[blk, st_blk, inv_blk],
        out_shape=[jax.ShapeDtypeStruct((bsz, lp, d), F32),
                   jax.ShapeDtypeStruct((bsz, nc, heads, LANES, LANES), F32),
                   jax.ShapeDtypeStruct((bsz, nc, heads, CHUNK, CHUNK), F32)],
        scratch_shapes=[pltpu.VMEM((heads, LANES, LANES), F32)],
        compiler_params=_params(("parallel", "arbitrary")),
    )(q, k, v, gb, *([o_prev] if has_prev else []))


def gdn_bwd(q, k, v, gb, states, invs, do, heads, direction, prev, name):
    bsz, lp, d = q.shape
    nc = lp // CHUNK
    rev = direction == 1
    cm = (lambda n: n) if rev else (lambda n: nc - 1 - n)
    has_prev = prev is not None

    def body(*refs):
        q_ref, k_ref, v_ref, gb_ref, st_ref, inv_ref, do_ref = refs[:7]
        prev_refs = refs[7:11] if has_prev else None
        dq_ref, dk_ref, dv_ref, dgb_ref, ds_ref = refs[-5:]
        n = pl.program_id(1)

        @pl.when(n == 0)
        def _():
            ds_ref[...] = jnp.zeros_like(ds_ref)

        sls = [slice(h * LANES, (h + 1) * LANES) for h in range(heads)]
        f = functools.partial(_gdn_chunk, g_lanes=[direction * heads + h for h in range(heads)],
                              b_lanes=[2 * heads + direction * heads + h for h in range(heads)], rev=rev,
                              invs=[inv_ref[0, 0, h] for h in range(heads)])
        _, pull = jax.vjp(f, [q_ref[0, :, sl] for sl in sls], [k_ref[0, :, sl] for sl in sls],
                          [v_ref[0, :, sl] for sl in sls], gb_ref[0], [st_ref[0, 0, h] for h in range(heads)])
        dq, dk, dv, dgb, ds = pull(([do_ref[0, :, sl] for sl in sls], [ds_ref[h] for h in range(heads)]))
        for h, sl in enumerate(sls):
            ds_ref[h] = ds[h]
            if has_prev:
                dq[h], dk[h], dv[h] = (dq[h] + prev_refs[0][0, :, sl], dk[h] + prev_refs[1][0, :, sl],
                                       dv[h] + prev_refs[2][0, :, sl])
            dq_ref[0, :, sl] = dq[h]
            dk_ref[0, :, sl] = dk[h]
            dv_ref[0, :, sl] = dv[h]
        dgb_ref[0] = dgb + prev_refs[3][0] if has_prev else dgb

    blk = pl.BlockSpec((1, CHUNK, d), lambda b, n: (b, cm(n), 0))
    gblk = pl.BlockSpec((1, CHUNK, LANES), lambda b, n: (b, cm(n), 0))
    st_blk = pl.BlockSpec((1, 1, heads, LANES, LANES), lambda b, n: (b, cm(n), 0, 0, 0))
    inv_blk = pl.BlockSpec((1, 1, heads, CHUNK, CHUNK), lambda b, n: (b, cm(n), 0, 0, 0))
    big = jax.ShapeDtypeStruct((bsz, lp, d), F32)
    return pl.pallas_call(
        body, name=name, grid=(bsz, nc),
        in_specs=[blk, blk, blk, gblk, st_blk, inv_blk, blk] + ([blk, blk, blk, gblk] if has_prev else []),
        out_specs=[blk, blk, blk, gblk],
        out_shape=[big, big, big, jax.ShapeDtypeStruct((bsz, lp, LANES), F32)],
        scratch_shapes=[pltpu.VMEM((heads, LANES, LANES), F32)],
        compiler_params=_params(("parallel", "arbitrary")),
    )(q, k, v, gb, states, invs, do, *(list(prev) if has_prev else []))


def _ret_chunk(qs, ks, vs, rs, logit, rev):
    hs = range(len(qs))
    dk = qs[0].shape[1]
    lg = [jax.nn.log_sigmoid(_lane_pick(logit, h)) for h in hs]
    ks = [k * (dk ** -0.5) for k in ks]
    ii = lax.broadcasted_iota(jnp.int32, (CHUNK, CHUNK), 0)
    jj = lax.broadcasted_iota(jnp.int32, (CHUNK, CHUNK), 1)
    pos = lax.broadcasted_iota(jnp.int32, (CHUNK, 1), 0)
    if rev:
        incl, rel = ii <= jj, (jj - ii)
        seen = (CHUNK - 1 - pos)
    else:
        incl, rel = ii >= jj, (ii - jj)
        seen = pos
    relf = jnp.where(incl, rel, 0).astype(F32)
    seenf = seen.astype(F32)
    intra = [jnp.where(incl, jnp.exp(relf * lg[h]), 0.0) for h in hs]
    qk = [_bdot(qs[h], ks[h], 1, 1) * intra[h] for h in hs]
    q_dec = [qs[h] * jnp.exp(lg[h] * (seenf + 1.0)) for h in hs]
    k_dec = [ks[h] * jnp.exp(lg[h] * (CHUNK - 1.0 - seenf)) for h in hs]
    o = [_bdot(q_dec[h], rs[h]) + _bdot(qk[h], vs[h]) for h in hs]
    r_new = [rs[h] * jnp.exp(lg[h] * CHUNK) + _bdot(k_dec[h], vs[h], 0, 0) for h in hs]
    return o, r_new


def ret_fwd(qk, v_arr, v_off, logit, heads, direction, o_prev, name):
    bsz, lp, d2 = qk.shape
    d = d2 // 2
    dkh, dvh = d // heads, 2 * d // heads
    nc = lp // CHUNK
    rev = direction == 1
    cm = (lambda n: nc - 1 - n) if rev else (lambda n: n)
    has_prev = o_prev is not None
    v_cb = v_off * LANES // (2 * d)
    assert v_cb * 2 * d == v_off * LANES

    def body(*refs):
        q_ref, k_ref, v_ref, lg_ref = refs[:4]
        prev_ref = refs[4] if has_prev else None
        o_ref, st_ref, r_ref = refs[-3], refs[-2], refs[-1]
        n = pl.program_id(1)

        @pl.when(n == 0)
        def _():
            r_ref[...] = jnp.zeros_like(r_ref)

        ksl = [slice(h * dkh, (h + 1) * dkh) for h in range(heads)]
        vsl = [slice(h * dvh, (h + 1) * dvh) for h in range(heads)]
        rs = [r_ref[h] for h in range(heads)]
        for h in range(heads):
            st_ref[0, 0, h] = rs[h].astype(st_ref.dtype)
        o, r_new = _ret_chunk([q_ref[0, :, s] for s in ksl], [k_ref[0, :, s] for s in ksl],
                              [v_ref[0, :, s] for s in vsl], rs, lg_ref[...], rev)
        for h in range(heads):
            r_ref[h] = r_new[h]
            o_ref[0, :, vsl[h]] = o[h] + prev_ref[0, :, vsl[h]] if has_prev else o[h]

    qblk = pl.BlockSpec((1, CHUNK, d), lambda b, n: (b, cm(n), 0))
    kblk = pl.BlockSpec((1, CHUNK, d), lambda b, n: (b, cm(n), 1))
    vblk = pl.BlockSpec((1, CHUNK, 2 * d), lambda b, n: (b, cm(n), v_cb))
    oblk = pl.BlockSpec((1, CHUNK, 2 * d), lambda b, n: (b, cm(n), 0))
    st_blk = pl.BlockSpec((1, 1, heads, dkh, dvh), lambda b, n: (b, cm(n), 0, 0, 0))
    return pl.pallas_call(
        body, name=name, grid=(bsz, nc),
        in_specs=[qblk, kblk, vblk, pl.BlockSpec((1, LANES), lambda b, n: (0, 0))] + ([oblk] if has_prev else []),
        out_specs=[oblk, st_blk],
        out_shape=[jax.ShapeDtypeStruct((bsz, lp, 2 * d), F32),
                   jax.ShapeDtypeStruct((bsz, nc, heads, dkh, dvh), BF16)],
        scratch_shapes=[pltpu.VMEM((heads, dkh, dvh), F32)],
        compiler_params=_params(("parallel", "arbitrary")),
    )(qk, qk, v_arr, logit, *([o_prev] if has_prev else []))


def ret_bwd(qk, v_arr, v_off, logit, states, do, heads, direction, prev, name, dv_dtype=F32):
    bsz, lp, d2 = qk.shape
    d = d2 // 2
    dkh, dvh = d // heads, 2 * d // heads
    nc = lp // CHUNK
    rev = direction == 1
    cm = (lambda n: n) if rev else (lambda n: nc - 1 - n)
    has_prev = prev is not None
    v_cb = v_off * LANES // (2 * d)

    def body(*refs):
        q_ref, k_ref, v_ref, lg_ref, st_ref, do_ref = refs[:6]
        prev_refs = refs[6:8] if has_prev else None
        dqk_ref, dv_ref, dlg_ref, dr_ref = refs[-4:]
        b, n = pl.program_id(0), pl.program_id(1)

        @pl.when(n == 0)
        def _():
            dr_ref[...] = jnp.zeros_like(dr_ref)

        ksl = [slice(h * dkh, (h + 1) * dkh) for h in range(heads)]
        vsl = [slice(h * dvh, (h + 1) * dvh) for h in range(heads)]
        f = functools.partial(_ret_chunk, rev=rev)
        _, pull = jax.vjp(f, [q_ref[0, :, s] for s in ksl], [k_ref[0, :, s] for s in ksl],
                          [v_ref[0, :, s] for s in vsl], [st_ref[0, 0, h].astype(F32) for h in range(heads)],
                          lg_ref[...])
        dqs, dks, dvs, drs, dlg = pull(([do_ref[0, :, s] for s in vsl], [dr_ref[h] for h in range(heads)]))
        for h in range(heads):
            ks, vs = ksl[h], vsl[h]
            dq, dk, dv = dqs[h], dks[h], dvs[h]
            dr_ref[h] = drs[h]
            kks = slice(d + h * dkh, d + (h + 1) * dkh)
            if has_prev:
                dq, dk, dv = dq + prev_refs[0][0, :, ks], dk + prev_refs[0][0, :, kks], dv + prev_refs[1][0, :, vs]
            dqk_ref[0, :, ks] = dq
            dqk_ref[0, :, kks] = dk
            dv_ref[0, :, vs] = dv.astype(dv_ref.dtype)
        first = jnp.logical_and(b == 0, n == 0)

        @pl.when(first)
        def _():
            dlg_ref[...] = dlg

        @pl.when(jnp.logical_not(first))
        def _():
            dlg_ref[...] += dlg

    qblk = pl.BlockSpec((1, CHUNK, d), lambda b, n: (b, cm(n), 0))
    kblk = pl.BlockSpec((1, CHUNK, d), lambda b, n: (b, cm(n), 1))
    vblk = pl.BlockSpec((1, CHUNK, 2 * d), lambda b, n: (b, cm(n), v_cb))
    oblk = pl.BlockSpec((1, CHUNK, 2 * d), lambda b, n: (b, cm(n), 0))
    lblk = pl.BlockSpec((1, LANES), lambda b, n: (0, 0))
    st_blk = pl.BlockSpec((1, 1, heads, dkh, dvh), lambda b, n: (b, cm(n), 0, 0, 0))
    return pl.pallas_call(
        body, name=name, grid=(bsz, nc),
        in_specs=[qblk, kblk, vblk, lblk, st_blk, oblk] + ([oblk, oblk] if has_prev else []),
        out_specs=[oblk, oblk, lblk],
        out_shape=[jax.ShapeDtypeStruct((bsz, lp, 2 * d), F32), jax.ShapeDtypeStruct((bsz, lp, 2 * d), dv_dtype),
                   jax.ShapeDtypeStruct((1, LANES), F32)],
        scratch_shapes=[pltpu.VMEM((heads, dkh, dvh), F32)],
        compiler_params=_params(("arbitrary", "arbitrary")),
    )(qk, qk, v_arr, logit, states, do, *(list(prev) if has_prev else []))


def _flip(v, bit):
    return 1 - v if bit else v


def _peer(x, y, c, off):
    return (_flip(x, off & 4), _flip(y, off & 2), _flip(c, off & 1))


def _slot(ref, axis, idx):
    return ref.at[(slice(None),) * axis + (idx,)]


def _slotted_shape(shape, axis):
    return tuple(shape[:axis]) + (N_DEV,) + tuple(shape[axis:])


def all_gather_hbm(blocks, axes, name):
    n = len(blocks)

    def body(*refs):
        x_refs, out_refs = refs[:n], refs[n:2 * n]
        send_sems, recv_sems, local_sems = refs[2 * n:]
        x, y, c = lax.axis_index("x"), lax.axis_index("y"), lax.axis_index("c")
        me, sibling = (x, y, c), (x, y, 1 - c)
        chips = [(1 - x, y), (x, 1 - y), (1 - x, 1 - y)]

        def slot(k, px, py, pc):
            return _slot(out_refs[k], axes[k], 4 * px + 2 * py + pc)

        def copy(k, j, block_of, to, src=None):
            return pltpu.make_async_remote_copy(
                src_ref=slot(k, *block_of) if src is None else src, dst_ref=slot(k, *block_of),
                send_sem=send_sems.at[7 * k + j], recv_sem=recv_sems.at[7 * k + j], device_id=to,
                device_id_type=pl.DeviceIdType.MESH)

        ks = range(n)
        mine = [pltpu.make_async_copy(x_refs[k], slot(k, *me), local_sems.at[k]) for k in ks]
        for cp in mine:
            cp.start()
        first = [copy(k, 0, me, sibling, src=x_refs[k]) for k in ks]
        first += [copy(k, 1 + j, me, (*chip, c), src=x_refs[k]) for k in ks for j, chip in enumerate(chips)]
        for cp in first:
            cp.start()
        passed = []
        for j, chip in enumerate(chips):
            for k in ks:
                copy(k, 1 + j, (*chip, c), me).wait_recv()
                passed.append(copy(k, 4 + j, (*chip, c), sibling))
                passed[-1].start()
        for k in ks:
            copy(k, 0, sibling, me).wait_recv()
        for j, chip in enumerate(chips):
            for k in ks:
                copy(k, 4 + j, (*chip, 1 - c), me).wait_recv()
        for cp in first + passed:
            cp.wait_send()
        for cp in mine:
            cp.wait()

    hbm = pl.BlockSpec(memory_space=pl.ANY)
    return pl.pallas_call(
        body, name=name,
        out_shape=[jax.ShapeDtypeStruct(_slotted_shape(b.shape, ax), b.dtype) for b, ax in zip(blocks, axes)],
        in_specs=[hbm] * n, out_specs=[hbm] * n,
        scratch_shapes=[pltpu.SemaphoreType.DMA((7 * n,)), pltpu.SemaphoreType.DMA((7 * n,)),
                        pltpu.SemaphoreType.DMA((n,))],
    )(*blocks)


def all_gather_small(block, name):
    r, lanes = block.shape

    def body(x_ref, out_ref, send_sems, recv_sems):
        x, y, c = lax.axis_index("x"), lax.axis_index("y"), lax.axis_index("c")
        me = 4 * x + 2 * y + c
        out_ref[me] = x_ref[...]
        copies = []
        for off in range(1, N_DEV):
            copies.append(pltpu.make_async_remote_copy(
                src_ref=x_ref, dst_ref=out_ref.at[me], send_sem=send_sems.at[off - 1], recv_sem=recv_sems.at[off - 1],
                device_id=_peer(x, y, c, off), device_id_type=pl.DeviceIdType.MESH))
        for cp in copies:
            cp.start()
        for cp in copies:
            cp.wait()

    return pl.pallas_call(
        body, name=name, out_shape=jax.ShapeDtypeStruct((N_DEV, r, lanes), block.dtype),
        in_specs=[pl.BlockSpec(memory_space=pltpu.VMEM)], out_specs=pl.BlockSpec(memory_space=pltpu.VMEM),
        scratch_shapes=[pltpu.SemaphoreType.DMA((7,)), pltpu.SemaphoreType.DMA((7,))],
    )(block)


def all_to_all_hbm(pieces, axes, name):
    n = len(pieces)

    def body(*refs):
        x_refs, out_refs = refs[:n], refs[n:2 * n]
        send_sems, recv_sems, local_sems = refs[2 * n:]
        x, y, c = lax.axis_index("x"), lax.axis_index("y"), lax.axis_index("c")
        me = 4 * x + 2 * y + c
        mine = [pltpu.make_async_copy(_slot(x_refs[k], axes[k], me), _slot(out_refs[k], axes[k], me), local_sems.at[k])
                for k in range(n)]
        for cp in mine:
            cp.start()
        copies = []
        for off in range(1, N_DEV):
            px, py, pc = _peer(x, y, c, off)
            for k in range(n):
                copies.append(pltpu.make_async_remote_copy(
                    src_ref=_slot(x_refs[k], axes[k], 4 * px + 2 * py + pc), dst_ref=_slot(out_refs[k], axes[k], me),
                    send_sem=send_sems.at[7 * k + off - 1], recv_sem=recv_sems.at[7 * k + off - 1],
                    device_id=(px, py, pc), device_id_type=pl.DeviceIdType.MESH))
        for cp in copies:
            cp.start()
        for cp in copies:
            cp.wait()
        for cp in mine:
            cp.wait()

    hbm = pl.BlockSpec(memory_space=pl.ANY)
    return pl.pallas_call(
        body, name=name, out_shape=[jax.ShapeDtypeStruct(p.shape, p.dtype) for p in pieces],
        in_specs=[hbm] * n, out_specs=[hbm] * n,
        scratch_shapes=[pltpu.SemaphoreType.DMA((7 * n,)), pltpu.SemaphoreType.DMA((7 * n,)),
                        pltpu.SemaphoreType.DMA((n,))],
    )(*pieces)


def _adamw_update(w, m, v, pieces):
    g = pieces[0].astype(F32)
    for piece in pieces[1:]:
        g = g + piece.astype(F32)
    mn = ADAM_B1 * m + (1.0 - ADAM_B1) * g
    vn = ADAM_B2 * v + (1.0 - ADAM_B2) * (g * g)
    m_hat = mn / (1.0 - ADAM_B1 ** ADAM_STEP)
    v_hat = vn / (1.0 - ADAM_B2 ** ADAM_STEP)
    return g, -ADAM_LR * (m_hat / (jnp.sqrt(v_hat) + ADAM_EPS) + ADAM_WD * w), mn, vn


def adamw_blocks(w, m, v, parts, slot_axis, name):
    nl, r, c = w.shape
    ns = parts.shape[slot_axis]
    tr = _tile(r, 128, 16)

    def body(w_ref, m_ref, v_ref, g_ref, g_out, d_out, m_out, v_out):
        pieces = [g_ref[s, 0] if slot_axis == 0 else g_ref[0, s] for s in range(ns)]
        for ref, val in zip((g_out, d_out, m_out, v_out), _adamw_update(w_ref[0], m_ref[0], v_ref[0], pieces)):
            ref[0] = val

    blk = pl.BlockSpec((1, tr, c), lambda l, i: (l, i, 0))
    gblk = (pl.BlockSpec((ns, 1, tr, c), lambda l, i: (0, l, i, 0)) if slot_axis == 0
            else pl.BlockSpec((1, ns, tr, c), lambda l, i: (l, 0, i, 0)))
    shp = jax.ShapeDtypeStruct(w.shape, F32)
    return pl.pallas_call(
        body, name=name, grid=(nl, r // tr), in_specs=[blk, blk, blk, gblk],
        out_specs=[blk, blk, blk, blk], out_shape=[shp, shp, shp, shp],
        compiler_params=_params(("parallel", "parallel")),
    )(w, m, v, parts)


def adamw(w, m, v, g8, name):
    r = w.shape[0]
    tm = _tile(r, 1024, 8)

    def body(w_ref, m_ref, v_ref, g_ref, g_out, d_out, m_out, v_out):
        res = _adamw_update(w_ref[...], m_ref[...], v_ref[...], [g_ref[s] for s in range(N_DEV)])
        for ref, val in zip((g_out, d_out, m_out, v_out), res):
            ref[...] = val

    blk = pl.BlockSpec((tm, LANES), lambda i: (i, 0))
    shp = jax.ShapeDtypeStruct((r, LANES), F32)
    return pl.pallas_call(
        body, name=name, grid=(r // tm,),
        in_specs=[blk, blk, blk, pl.BlockSpec((N_DEV, tm, LANES), lambda i: (0, i, 0))],
        out_specs=[blk, blk, blk, blk], out_shape=[shp, shp, shp, shp],
        compiler_params=_params(("parallel",)),
    )(w, m, v, g8)


def _pack(blocks, rows_mult):
    flat = jnp.concatenate([b.reshape(-1) for b in blocks])
    unit = rows_mult * LANES
    total = -(-flat.shape[0] // unit) * unit
    return jnp.pad(flat, (0, total - flat.shape[0])).reshape(-1, LANES)


def _unpack(packed, shapes):
    flat = packed.reshape(-1)
    out, pos = [], 0
    for s in shapes:
        n = math.prod(s)
        out.append(flat[pos:pos + n].reshape(s))
        pos += n
    return out


def _gathered_full(gathered, shapes, axes):
    per_dev = [_unpack(gathered[d], shapes) for d in range(N_DEV)]
    return [jnp.concatenate([per_dev[d][k] for d in range(N_DEV)], axis=axes[k]) for k in range(len(shapes))]


def _pieces_by_dest(fulls, axes, rows_mult):
    packs = []
    for d in range(N_DEV):
        blocks = []
        for f, ax in zip(fulls, axes):
            n = f.shape[ax] // N_DEV
            blocks.append(lax.slice_in_dim(f, d * n, (d + 1) * n, axis=ax))
        packs.append(_pack(blocks, rows_mult))
    return jnp.stack(packs)


class Layout:
    def __init__(self, d):
        self.d = d
        self.h = d // 128
        self.hr = d // 256
        self.z = 3 * d
        self.qb = 4 * d
        self.vb = 6 * d
        self.gb = 8 * d
        self.ga = 10 * d
        self.gbt = 11 * d
        self.ab = 12 * d
        self.used = 12 * d + LANES
        self.np = -(-self.used // 512) * 512

    def relayout_w_in(self, w):
        d, h4 = self.d, 4 * self.h
        return jnp.concatenate([w[:, :4 * d], w[:, 4 * d + h4:], w[:, 4 * d:4 * d + h4],
                                jnp.zeros((d, self.np - 12 * d - h4), w.dtype)], axis=1)

    def unlayout_w_in(self, w):
        d, h4 = self.d, 4 * self.h
        return jnp.concatenate([w[:, :4 * d], w[:, 12 * d:12 * d + h4], w[:, 4 * d:12 * d]], axis=1)


def _lane_row(vec):
    return jnp.pad(vec.reshape(-1), (0, LANES - vec.size)).reshape(1, LANES)


def _rope_tables(lp, half):
    inv = ROPE_BASE ** (-jnp.arange(half, dtype=F32) / half)
    pos = (jnp.arange(lp) - N_PAD).astype(F32)
    ang = pos[:, None] * inv[None, :]
    return jnp.cos(ang), jnp.sin(ang)


def local_step(x, target, meta, p):
    bsz, seq, d = x.shape
    lay = Layout(d)
    h_gdn, h_ret = lay.h, lay.hr
    lp = seq + CHUNK
    t_all = bsz * lp
    depth = p["w_up_a"].shape[0]
    ff = p["w_ffn_out"].shape[1]
    tm = _tile(lp, 512, 16)
    tmw = _tile(lp, 256, 16)
    tmn = _tile(lp, 1040, 16)
    tmr = _tile(lp, 832, 16)
    cb = lambda cols: cols // LANES
    flat = lambda a: a.reshape(t_all, a.shape[-1])
    unflat = lambda a: a.reshape(bsz, lp, a.shape[-1])
    cos, sin = _rope_tables(lp, LANES)
    f_gb = make_f_gb(h_gdn)
    rope_f, rope_b = make_f_rope(1.0), make_f_rope(-1.0)

    head = jnp.concatenate([jnp.zeros((N_PAD, d), F32), meta], axis=0)
    h = jnp.concatenate([jnp.broadcast_to(head[None], (bsz, CHUNK, d)), x], axis=1)
    tgt = jnp.pad(target, ((0, 0), (CHUNK, 0), (0, 0)))

    saved = []
    for l in range(depth):
        s = {"h_in": h}
        nm = lambda k: f"l{l}_{k}"
        g_mix, g_ffn = p["norm_mix"][l][None], p["norm_ffn"][l][None]
        alog, dtb = _lane_row(p["gdn_a_log"][l]), _lane_row(p["gdn_dt_bias"][l])
        gain_a = p["gdn_norm"][l][None]
        logits = [_lane_row(p["ret_decay_logit"][l][0]), _lane_row(p["ret_decay_logit"][l][1])]
        cw = p["conv_w"][l]
        (hn,) = rowwise(nm("rms_mix"), f_rms, [Row(h, d)], [], [g_mix], [(d, d)], 1, tm, BF16)
        proj = unflat(matmul(flat(hn), p["w_in"][l], name=nm("mm_in")))
        qa = conv_fwd(proj, 0, cw[:, :d], True, nm("conv_q"))
        ka = conv_fwd(proj, cb(d), cw[:, d:2 * d], True, nm("conv_k"))
        va = conv_fwd(proj, cb(2 * d), cw[:, 2 * d:], False, nm("conv_v"))
        (gb,) = rowwise(nm("gb"), f_gb, [Row(proj, LANES, cb(lay.ab))], [], [alog, dtb], [(LANES, LANES)], 1, tmn)
        o0, st_a0, iv_a0 = gdn_fwd(qa, ka, va, gb, h_gdn, 0, None, nm("gdn_f0"))
        oa, st_a1, iv_a1 = gdn_fwd(qa, ka, va, gb, h_gdn, 1, o0, nm("gdn_f1"))
        (oan,) = rowwise(nm("gdn_out"), f_gdn_out, [Row(oa, LANES), Row(proj, LANES, cb(lay.z))], [], [gain_a],
                         [(d, LANES)], h_gdn, tmn, BF16)
        ya = unflat(matmul(flat(oan), p["w_up_a"][l], name=nm("mm_up_a")))
        (qkr,) = rowwise(nm("rope"), rope_f, [Row(proj, 2 * LANES, cb(lay.qb) // 2)], [Tab(cos), Tab(sin)], [],
                         [(2 * d, 2 * LANES)], 2 * h_ret, tmn)
        r0, st_b0 = ret_fwd(qkr, proj, cb(lay.vb), logits[0], h_ret, 0, None, nm("ret_f0"))
        ob, st_b1 = ret_fwd(qkr, proj, cb(lay.vb), logits[1], h_ret, 1, r0, nm("ret_f1"))
        (obn,) = rowwise(nm("ret_out"), f_ret_out, [Row(ob, 4 * LANES), Row(proj, 4 * LANES, cb(lay.gb) // 4)], [], [],
                         [(2 * d, 4 * LANES)], h_ret, tmr, BF16)
        yb = unflat(matmul(flat(obn), p["w_up_b"][l], name=nm("mm_up_b")))
        (mg,) = rowwise(nm("merge"), f_merge,
                        [Row(proj, LANES, cb(lay.ga)), Row(proj, LANES, cb(lay.gbt)), Row(ya, LANES), Row(yb, LANES)],
                        [], [], [(d, LANES)], cb(d), tmn, BF16)
        h_mid = unflat(matmul(flat(mg), p["w_out"][l], add=flat(h), name=nm("mm_out")))
        (hn2,) = rowwise(nm("rms_ffn"), f_rms, [Row(h_mid, d)], [], [g_ffn], [(d, d)], 1, tm, BF16)
        ffp = unflat(matmul(flat(hn2), p["w_ffn_in"][l], name=nm("mm_ffn_in")))
        (act,) = rowwise(nm("swiglu"), f_swiglu, [Row(ffp, 2 * ff)], [], [], [(ff, ff)], 1, tmw, BF16)
        h = unflat(matmul(flat(act), p["w_ffn_out"][l], add=flat(h_mid), name=nm("mm_ffn_out")))
        s.update(hn=hn, proj=proj, qa=qa, ka=ka, va=va, gb=gb, st_a=(st_a0, st_a1), iv_a=(iv_a0, iv_a1), oa=oa,
                 oan=oan, ya=ya, qkr=qkr, st_b=(st_b0, st_b1), ob=ob, obn=obn, yb=yb, mg=mg, h_mid=h_mid, hn2=hn2,
                 ffp=ffp, act=act, logits=logits, alog=alog, dtb=dtb, gain_a=gain_a, cw=cw, g_mix=g_mix, g_ffn=g_ffn)
        saved.append(s)

    dh, d_final, loss_row = loss_head(h, p["norm_final"][None], tgt, tm)

    grads = {k: [None] * depth for k in ("norm_mix", "w_in", "conv_w", "gdn_a_log", "gdn_dt_bias", "gdn_norm",
                                          "ret_decay_logit", "w_up_a", "w_up_b", "w_out", "norm_ffn", "w_ffn_in",
                                          "w_ffn_out")}
    for l in reversed(range(depth)):
        s = saved[l]
        nm = lambda k: f"l{l}_{k}"
        proj = s["proj"]
        dhf = flat(dh)
        grads["w_ffn_out"][l] = matmul(flat(s["act"]), dhf, ta=True, name=nm("mmg_ffn_out"))
        dact = unflat(matmul(dhf, p["w_ffn_out"][l], tb=True, name=nm("mmb_ffn_out")))
        (dffp,), _ = rowwise_vjp(nm("swiglu_b"), f_swiglu, [Row(s["ffp"], 2 * ff)], [], [], [dact], 1, tmw, narrow=(0,))
        grads["w_ffn_in"][l] = matmul(flat(s["hn2"]), flat(dffp), ta=True, name=nm("mmg_ffn_in"))
        dhn2 = unflat(matmul(flat(dffp), p["w_ffn_in"][l], tb=True, name=nm("mmb_ffn_in")))
        (dh_mid,), (dg_ffn,) = rowwise_vjp(nm("rms_ffn_b"), f_rms, [Row(s["h_mid"], d)], [], [s["g_ffn"]], [dhn2], 1, tmw,
                                           adds={0: dh})
        grads["norm_ffn"][l] = dg_ffn[0]
        dmf = flat(dh_mid)
        grads["w_out"][l] = matmul(flat(s["mg"]), dmf, ta=True, name=nm("mmg_out"))
        dmg = unflat(matmul(dmf, p["w_out"][l], tb=True, name=nm("mmb_out")))
        (dga, dgbt, dya, dyb), _ = rowwise_vjp(
            nm("merge_b"), f_merge,
            [Row(proj, LANES, cb(lay.ga)), Row(proj, LANES, cb(lay.gbt)), Row(s["ya"], LANES), Row(s["yb"], LANES)],
            [], [], [dmg], cb(d), tmn, narrow=(0, 1, 2, 3))
        grads["w_up_b"][l] = matmul(flat(s["obn"]), flat(dyb), ta=True, name=nm("mmg_up_b"))
        dobn = unflat(matmul(flat(dyb), p["w_up_b"][l], tb=True, name=nm("mmb_up_b")))
        (dob, dg_b), _ = rowwise_vjp(nm("ret_out_b"), f_ret_out,
                                     [Row(s["ob"], 4 * LANES), Row(proj, 4 * LANES, cb(lay.gb) // 4)], [], [], [dobn],
                                     h_ret, tmr, narrow=(1,))
        r1 = ret_bwd(s["qkr"], proj, cb(lay.vb), s["logits"][1], s["st_b"][1], dob, h_ret, 1, None, nm("ret_b1"))
        r0 = ret_bwd(s["qkr"], proj, cb(lay.vb), s["logits"][0], s["st_b"][0], dob, h_ret, 0, r1[:2], nm("ret_b0"),
                     dv_dtype=BF16)
        (dqk,) = rowwise(nm("rope_b"), rope_b, [Row(r0[0], 2 * LANES)], [Tab(cos), Tab(sin)], [],
                         [(2 * d, 2 * LANES)], 2 * h_ret, tmn, BF16)
        dv_b = r0[1]
        grads["ret_decay_logit"][l] = jnp.stack([r0[2][0, :h_ret], r1[2][0, :h_ret]])
        grads["w_up_a"][l] = matmul(flat(s["oan"]), flat(dya), ta=True, name=nm("mmg_up_a"))
        doan = unflat(matmul(flat(dya), p["w_up_a"][l], tb=True, name=nm("mmb_up_a")))
        (doa, dz), (dgain_a,) = rowwise_vjp(nm("gdn_out_b"), f_gdn_out,
                                            [Row(s["oa"], LANES), Row(proj, LANES, cb(lay.z))], [], [s["gain_a"]],
                                            [doan], h_gdn, tmn, narrow=(1,))
        grads["gdn_norm"][l] = dgain_a[0]
        a1 = gdn_bwd(s["qa"], s["ka"], s["va"], s["gb"], s["st_a"][1], s["iv_a"][1], doa, h_gdn, 1, None, nm("gdn_b1"))
        a0 = gdn_bwd(s["qa"], s["ka"], s["va"], s["gb"], s["st_a"][0], s["iv_a"][0], doa, h_gdn, 0, a1, nm("gdn_b0"))
        (dab,), (dalog, ddtb) = rowwise_vjp(nm("gb_b"), f_gb, [Row(proj, LANES, cb(lay.ab))], [], [s["alog"], s["dtb"]],
                                            [a0[3]], 1, tmn, narrow=(0,))
        grads["gdn_a_log"][l] = dalog[0, :2 * h_gdn].reshape(2, h_gdn)
        grads["gdn_dt_bias"][l] = ddtb[0, :2 * h_gdn].reshape(2, h_gdn)
        cw = s["cw"]
        dxq, dwq = conv_bwd(proj, 0, cw[:, :d], a0[0], True, nm("conv_q_b"))
        dxk, dwk = conv_bwd(proj, cb(d), cw[:, d:2 * d], a0[1], True, nm("conv_k_b"))
        dxv, dwv = conv_bwd(proj, cb(2 * d), cw[:, 2 * d:], a0[2], False, nm("conv_v_b"))
        grads["conv_w"][l] = jnp.concatenate([dwq, dwk, dwv], axis=1)
        dproj = jnp.concatenate([dxq, dxk, dxv, dz, dqk, dv_b, dg_b, dga, dgbt, dab,
                                 jnp.zeros((bsz, lp, lay.np - lay.used), BF16)], axis=-1)
        grads["w_in"][l] = matmul(flat(s["hn"]), flat(dproj), ta=True, name=nm("mmg_in"))
        dhn = unflat(matmul(flat(dproj), p["w_in"][l], tb=True, name=nm("mmb_in")))
        (dh,), (dg_mix,) = rowwise_vjp(nm("rms_mix_b"), f_rms, [Row(s["h_in"], d)], [], [s["g_mix"]], [dhn], 1, tmw,
                                       adds={0: dh_mid})
        grads["norm_mix"][l] = dg_mix[0]

    out = {k: jnp.stack(v) for k, v in grads.items()}
    out["norm_final"] = d_final[0]
    grad_x = dh[:, CHUNK:]
    grad_meta = jnp.sum(dh[:, N_PAD:CHUNK], axis=0)
    return loss_row, grad_x, grad_meta, out


BIG = ("w_in", "w_up_a", "w_up_b", "w_out", "w_ffn_in", "w_ffn_out")
BIG_AXES = (2, 1, 1, 1, 2, 1)
SMALL_SHARDED = ("meta_tokens", "conv_w")
SMALL_AXES = (1, 2)
REPLICATED = ("norm_mix", "gdn_a_log", "gdn_dt_bias", "gdn_norm", "ret_decay_logit", "norm_ffn", "norm_final")
WEIGHTS = ("meta_tokens", "norm_mix", "w_in", "conv_w", "gdn_a_log", "gdn_dt_bias", "gdn_norm", "ret_decay_logit",
           "w_up_a", "w_up_b", "w_out", "norm_ffn", "w_ffn_in", "w_ffn_out", "norm_final")


def kernel(x, meta_tokens, norm_mix, w_in, conv_w, gdn_a_log, gdn_dt_bias, gdn_norm, ret_decay_logit, w_up_a, w_up_b, w_out, norm_ffn, w_ffn_in, w_ffn_out, norm_final, loss_target, m_meta_tokens, m_norm_mix, m_w_in, m_conv_w, m_gdn_a_log, m_gdn_dt_bias, m_gdn_norm, m_ret_decay_logit, m_w_up_a, m_w_up_b, m_w_out, m_norm_ffn, m_w_ffn_in, m_w_ffn_out, m_norm_final, v_meta_tokens, v_norm_mix, v_w_in, v_conv_w, v_gdn_a_log, v_gdn_dt_bias, v_gdn_norm, v_ret_decay_logit, v_w_up_a, v_w_up_b, v_w_out, v_norm_ffn, v_w_ffn_in, v_w_ffn_out, v_norm_final):
    w = dict(meta_tokens=meta_tokens, norm_mix=norm_mix, w_in=w_in, conv_w=conv_w, gdn_a_log=gdn_a_log,
             gdn_dt_bias=gdn_dt_bias, gdn_norm=gdn_norm, ret_decay_logit=ret_decay_logit, w_up_a=w_up_a,
             w_up_b=w_up_b, w_out=w_out, norm_ffn=norm_ffn, w_ffn_in=w_ffn_in, w_ffn_out=w_ffn_out,
             norm_final=norm_final)
    m = dict(meta_tokens=m_meta_tokens, norm_mix=m_norm_mix, w_in=m_w_in, conv_w=m_conv_w, gdn_a_log=m_gdn_a_log,
             gdn_dt_bias=m_gdn_dt_bias, gdn_norm=m_gdn_norm, ret_decay_logit=m_ret_decay_logit, w_up_a=m_w_up_a,
             w_up_b=m_w_up_b, w_out=m_w_out, norm_ffn=m_norm_ffn, w_ffn_in=m_w_ffn_in, w_ffn_out=m_w_ffn_out,
             norm_final=m_norm_final)
    v = dict(meta_tokens=v_meta_tokens, norm_mix=v_norm_mix, w_in=v_w_in, conv_w=v_conv_w, gdn_a_log=v_gdn_a_log,
             gdn_dt_bias=v_gdn_dt_bias, gdn_norm=v_gdn_norm, ret_decay_logit=v_ret_decay_logit, w_up_a=v_w_up_a,
             w_up_b=v_w_up_b, w_out=v_w_out, norm_ffn=v_norm_ffn, w_ffn_in=v_w_ffn_in, w_ffn_out=v_w_ffn_out,
             norm_final=v_norm_final)
    d = x.shape[-1]
    lay = Layout(d)
    slot_axes = [1 if ax == 1 else 0 for ax in BIG_AXES]

    gathered = all_gather_hbm([w[k].astype(BF16) for k in BIG], slot_axes, "gather_weights")
    full = {}
    for k, gk, sa in zip(BIG, gathered, slot_axes):
        if sa == 1:
            full[k] = gk.reshape(gk.shape[0], -1, gk.shape[3])
        else:
            full[k] = jnp.concatenate([gk[s] for s in range(N_DEV)], axis=2)
    small_shapes = [w[k].shape for k in SMALL_SHARDED]
    gathered_s = all_gather_small(_pack([w[k] for k in SMALL_SHARDED], 8), "gather_small")
    full.update(zip(SMALL_SHARDED, _gathered_full(gathered_s, small_shapes, SMALL_AXES)))
    p = {k: w[k] for k in REPLICATED}
    p.update({k: full[k] for k in BIG + ("conv_w",)})
    p["w_in"] = jnp.stack([lay.relayout_w_in(full["w_in"][l]) for l in range(full["w_in"].shape[0])])

    loss_row, grad_x, grad_meta, g = local_step(x, loss_target, full["meta_tokens"], p)
    g["meta_tokens"] = grad_meta
    g["w_in"] = jnp.stack([lay.unlayout_w_in(g["w_in"][l]) for l in range(g["w_in"].shape[0])])

    pieces = []
    for k, sa in zip(BIG, slot_axes):
        nl, rows, cols = g[k].shape
        if sa == 1:
            pieces.append(g[k].astype(BF16).reshape(nl, N_DEV, rows // N_DEV, cols))
        else:
            cs = cols // N_DEV
            pieces.append(jnp.stack([g[k][:, :, s * cs:(s + 1) * cs] for s in range(N_DEV)]).astype(BF16))
    small_pieces = _pieces_by_dest([g[k] for k in SMALL_SHARDED], SMALL_AXES, 8)
    received = all_to_all_hbm(pieces + [small_pieces], slot_axes + [0], "exchange_grads")
    outs = {kind: {} for kind in ("g", "d", "m", "v")}
    for k, sa, got in zip(BIG, slot_axes, received):
        for kind, r in zip(("g", "d", "m", "v"), adamw_blocks(w[k], m[k], v[k], got, sa, "adamw_" + k)):
            outs[kind][k] = r
    res = adamw(_pack([w[k] for k in SMALL_SHARDED], 8), _pack([m[k] for k in SMALL_SHARDED], 8),
                _pack([v[k] for k in SMALL_SHARDED], 8), received[-1], "adamw_small")
    for kind, r in zip(("g", "d", "m", "v"), res):
        outs[kind].update(zip(SMALL_SHARDED, _unpack(r, small_shapes)))

    rep_shapes = [w[k].shape for k in REPLICATED]
    part = _pack([g[k] for k in REPLICATED] + [loss_row[0, :1]], 8)
    parts = all_gather_small(part, "gather_replicated")
    pad1 = lambda a: _pack([a[k] for k in REPLICATED] + [jnp.zeros((1,), F32)], 8)
    res_r = adamw(pad1(w), pad1(m), pad1(v), parts, "adamw_replicated")
    for kind, r in zip(("g", "d", "m", "v"), res_r):
        outs[kind].update(zip(REPLICATED, _unpack(r, rep_shapes)))
    loss = res_r[0].reshape(-1)[sum(math.prod(s) for s in rep_shapes)]

    return (loss, grad_x, *[outs["g"][k] for k in WEIGHTS], *[outs["d"][k] for k in WEIGHTS],
            *[outs["m"][k] for k in WEIGHTS], *[outs["v"][k] for k in WEIGHTS])
```
